```python
import math
import jax
import jax.numpy as jnp
from jax import lax
import numpy as np

D_MODEL = 1024
BATCH = 8
SEQ = 2048
DEPTH = 4
DEC_BATCH = 128
DEC_SEQ = 1
PAST_LEN = 2048
PAGE_SIZE = 128

NSA_HEADS = 8
NSA_GROUPS = 2
NSA_HPG = NSA_HEADS // NSA_GROUPS
HEAD_DIM = 64
SCALE = HEAD_DIM ** -0.5
CMP_LEN = 32
CMP_STRIDE = 16
CMP_HID = 2 * HEAD_DIM
SEL_LEN = 64
N_SEL = 16
WINDOW = 512
Q_BLOCK = 128
GLA_HEADS = 4
GLA_DK = D_MODEL // 4 // GLA_HEADS
GLA_DV = D_MODEL // 2 // GLA_HEADS
GLA_RANK = 16
GLA_TAU = 16.0
GLA_CHUNK = 64
D_FF = 4 * D_MODEL
N_KV_SLOTS = 4
ROPE_THETA = 10000.0
EPS = 1e-5
BIG = 1e6
DN_ALPHA = (2.0 * DEPTH) ** 0.25
DN_BETA = (8.0 * DEPTH) ** -0.25

SPLIT_SIZES = (NSA_HEADS * HEAD_DIM, 6 * NSA_GROUPS * HEAD_DIM, 3 * NSA_HEADS,
               GLA_HEADS * GLA_DK, GLA_HEADS * GLA_DK, GLA_HEADS * GLA_DV,
               GLA_RANK, GLA_HEADS * GLA_DV, 2 * D_MODEL)
SPLIT_POINTS = tuple(int(v) for v in np.cumsum(SPLIT_SIZES)[:-1])
D_IN = sum(SPLIT_SIZES)

kernel_name = 'nsa_gla_parallel_deepnorm_decode_step'


def layer_norm(x, g, b):
    xf = x.astype(jnp.float32)
    mu = jnp.mean(xf, -1, keepdims=True)
    var = jnp.mean(jnp.square(xf - mu), -1, keepdims=True)
    return ((xf - mu) * lax.rsqrt(var + EPS)).astype(x.dtype) * g + b


def rms_norm(x, g):
    xf = x.astype(jnp.float32)
    return (xf * lax.rsqrt(jnp.mean(jnp.square(xf), -1, keepdims=True) + EPS)).astype(x.dtype) * g


def rope(x, pos):
    half = x.shape[-1] // 2
    inv = ROPE_THETA ** (-jnp.arange(half, dtype=jnp.float32) / half)
    ang = pos.astype(jnp.float32)[:, None] * inv[None, :]
    cos = jnp.cos(ang)[:, None, :].astype(x.dtype)
    sin = jnp.sin(ang)[:, None, :].astype(x.dtype)
    x1, x2 = x[..., :half], x[..., half:]
    return jnp.concatenate([x1 * cos - x2 * sin, x2 * cos + x1 * sin], axis=-1)


def masked_softmax(s, mask):
    s = jnp.where(mask, s.astype(jnp.float32), -jnp.inf)
    m = jnp.max(s, axis=-1, keepdims=True)
    p = jnp.exp(s - jnp.where(jnp.isfinite(m), m, 0.0))
    return p / jnp.maximum(jnp.sum(p, axis=-1, keepdims=True), 1e-30)


def compress_blocks(rows, pe, w1, w2):
    B, T, G, dh = rows.shape
    nc = (T - CMP_LEN) // CMP_STRIDE + 1
    idx = jnp.arange(nc)[:, None] * CMP_STRIDE + jnp.arange(CMP_LEN)[None, :]
    blk = rows[:, idx] + pe[None, None, :, None, :]
    blk = blk.transpose(0, 1, 3, 2, 4).reshape(B, nc, G, CMP_LEN * dh)
    return jax.nn.gelu(blk @ w1) @ w2


def nsa_compressed(q, k_cmp, v_cmp, qpos):
    nc = k_cmp.shape[1]
    s = jnp.einsum('btghd,bcgd->bghtc', q, k_cmp) * SCALE
    valid = (jnp.arange(nc) * CMP_STRIDE + CMP_LEN - 1)[None, :] <= qpos[:, None]
    p = masked_softmax(s, valid)
    o = jnp.einsum('bghtc,bcgd->btghd', p.astype(v_cmp.dtype), v_cmp)
    return o, p


def select_blocks(p_cmp, qpos, n_blocks):
    nc = p_cmp.shape[-1]
    r, w = SEL_LEN // CMP_STRIDE, CMP_LEN // CMP_STRIDE
    off = (jnp.arange(r)[:, None] + jnp.arange(w)[None, :]).reshape(-1)
    blk = jnp.arange(n_blocks)
    overlap = jnp.sum(jnp.arange(nc)[:, None, None] == (r * blk[None, :, None] + off[None, None, :]),
                      axis=-1).astype(jnp.float32)
    imp = jnp.einsum('bghtc,cj->bgtj', p_cmp, overlap)
    cur = qpos[:, None] // SEL_LEN
    causal = blk[None, :] * SEL_LEN <= qpos[:, None]
    forced = (blk[None, :] == 0) | (blk[None, :] == cur) | (blk[None, :] == cur - 1)
    score = jnp.where(forced, BIG, jnp.where(causal, imp, -jnp.inf))
    top, idx = lax.top_k(score, min(N_SEL, n_blocks))
    return idx, jnp.isfinite(top)


def nsa_selected(q, k_rows, v_rows, idx, ok, qpos):
    B, Tk = k_rows.shape[:2]
    n_blocks = -(-Tk // SEL_LEN)
    pad = ((0, 0), (0, n_blocks * SEL_LEN - Tk), (0, 0), (0, 0))
    kb = jnp.pad(k_rows, pad).reshape(B, n_blocks, SEL_LEN, NSA_GROUPS, HEAD_DIM).transpose(0, 3, 1, 2, 4)
    vb = jnp.pad(v_rows, pad).reshape(B, n_blocks, SEL_LEN, NSA_GROUPS, HEAD_DIM).transpose(0, 3, 1, 2, 4)
    Tq, n = q.shape[1], idx.shape[-1]
    qb = math.gcd(Tq, Q_BLOCK)
    nqb = Tq // qb
    qs = q.reshape(B, nqb, qb, NSA_GROUPS, NSA_HPG, HEAD_DIM).transpose(1, 0, 2, 3, 4, 5)
    idxs = idx.reshape(B, NSA_GROUPS, nqb, qb, n).transpose(2, 0, 1, 3, 4)
    oks = ok.reshape(B, NSA_GROUPS, nqb, qb, n).transpose(2, 0, 1, 3, 4)
    poss = qpos.reshape(nqb, qb)
    bi = jnp.arange(B)[:, None, None, None]
    gi = jnp.arange(NSA_GROUPS)[None, :, None, None]

    def one_block(args):
        qc, ic, okc, pc = args
        kg = kb[bi, gi, ic]
        vg = vb[bi, gi, ic]
        s = jnp.einsum('bqghd,bgqnld->bghqnl', qc, kg) * SCALE
        kpos = ic[..., None] * SEL_LEN + jnp.arange(SEL_LEN)
        mask = okc[..., None] & (kpos <= pc[None, None, :, None, None])
        p = masked_softmax(s.reshape(B, NSA_GROUPS, NSA_HPG, qb, n * SEL_LEN),
                           mask.reshape(B, NSA_GROUPS, 1, qb, n * SEL_LEN))
        return jnp.einsum('bghqm,bgqmd->bqghd', p.astype(vg.dtype),
                          vg.reshape(B, NSA_GROUPS, qb, n * SEL_LEN, HEAD_DIM))

    out = lax.map(one_block, (qs, idxs, oks, poss))
    return out.transpose(1, 0, 2, 3, 4, 5).reshape(B, Tq, NSA_GROUPS, NSA_HPG, HEAD_DIM)


def window_prompt(q, k, v):
    B, T = q.shape[:2]
    qb = math.gcd(T, Q_BLOCK)
    nqb = T // qb
    span = WINDOW + qb
    pad = ((0, 0), (WINDOW, 0), (0, 0), (0, 0))
    idx = jnp.arange(nqb)[:, None] * qb + jnp.arange(span)[None, :]
    k_band = jnp.pad(k, pad)[:, idx]
    v_band = jnp.pad(v, pad)[:, idx]
    qpos = jnp.arange(T).reshape(nqb, qb)
    kpos = idx - WINDOW
    d = qpos[:, :, None] - kpos[:, None, :]
    mask = (kpos[:, None, :] >= 0) & (d >= 0) & (d <= WINDOW)
    s = jnp.einsum('bnqghd,bnkgd->bghnqk',
                   q.reshape(B, nqb, qb, NSA_GROUPS, NSA_HPG, HEAD_DIM), k_band) * SCALE
    p = masked_softmax(s, mask)
    o = jnp.einsum('bghnqk,bnkgd->bnqghd', p.astype(v_band.dtype), v_band)
    return o.reshape(B, T, NSA_GROUPS, NSA_HPG, HEAD_DIM)


def window_decode(q, buf, k_new, v_new, qpos):
    wb = buf.shape[1]
    k = jnp.concatenate([buf[:, :, 0], k_new], axis=1)
    v = jnp.concatenate([buf[:, :, 1], v_new], axis=1)
    kpos = qpos[0] - wb + jnp.arange(wb + q.shape[1])
    d = qpos[:, None] - kpos[None, :]
    mask = (d >= 0) & (d <= WINDOW)
    s = jnp.einsum('btghd,bkgd->bghtk', q, k) * SCALE
    p = masked_softmax(s, mask)
    o = jnp.einsum('bghtk,bkgd->btghd', p.astype(v.dtype), v)
    return o, jnp.stack([k[:, -wb:], v[:, -wb:]], axis=2)


def gla_chunked(q, k, v, log_a, s0):
    B, T = q.shape[:2]
    C = math.gcd(T, GLA_CHUNK)
    nch = T // C

    def to_chunks(a):
        return a.reshape(B, nch, C, a.shape[2], a.shape[3]).transpose(1, 0, 3, 2, 4)

    tri = jnp.tril(jnp.ones((C, C), dtype=bool))

    def step(S, inp):
        qc, kc, vc, lc = inp
        b = jnp.cumsum(lc, axis=2)
        o_inter = jnp.einsum('bhtk,bhkv->bhtv', qc * jnp.exp(b).astype(qc.dtype), S)
        diff = jnp.where(tri[:, :, None], b[:, :, :, None, :] - b[:, :, None, :, :], -jnp.inf)
        A = jnp.einsum('bhtk,bhsk,bhtsk->bhts', qc, kc, jnp.exp(diff).astype(qc.dtype))
        o = o_inter + jnp.einsum('bhts,bhsv->bhtv', A, vc)
        b_last = b[:, :, -1, :]
        k_dec = kc * jnp.exp(b_last[:, :, None, :] - b).astype(kc.dtype)
        S_new = jnp.exp(b_last)[..., None].astype(S.dtype) * S + jnp.einsum('bhsk,bhsv->bhkv', k_dec, vc)
        return S_new.astype(S.dtype), o

    S, o = lax.scan(step, s0, (to_chunks(q), to_chunks(k), to_chunks(v), to_chunks(log_a)))
    return o.transpose(1, 0, 3, 2, 4).reshape(B, T, q.shape[2], v.shape[3]), S


def token_mixer(x, pos, past_kv, win_buf, gla_s0, p):
    B, T, _ = x.shape
    z = x @ p['w_in']
    q, kv6, nsa_g, gq, gk, gv, g_lr, g_r, mg = jnp.split(z, SPLIT_POINTS, axis=-1)
    q = q.reshape(B, T, NSA_HEADS, HEAD_DIM)
    kv6 = kv6.reshape(B, T, 6, NSA_GROUPS, HEAD_DIM)
    k_win = rope(kv6[:, :, 4], pos)
    v_win = kv6[:, :, 5]
    kv_rows = jnp.stack([kv6[:, :, 0], kv6[:, :, 1], rope(kv6[:, :, 2], pos), kv6[:, :, 3]], axis=2)
    full = kv_rows if past_kv is None else jnp.concatenate([past_kv, kv_rows], axis=1)
    k_cmp = compress_blocks(full[:, :, 0], p['cmp_k_pe'], p['cmp_k_w1'], p['cmp_k_w2'])
    v_cmp = compress_blocks(full[:, :, 1], p['cmp_v_pe'], p['cmp_v_w1'], p['cmp_v_w2'])
    q_plain = q.reshape(B, T, NSA_GROUPS, NSA_HPG, HEAD_DIM)
    q_rot = rope(q, pos).reshape(B, T, NSA_GROUPS, NSA_HPG, HEAD_DIM)
    o_cmp, p_cmp = nsa_compressed(q_plain, k_cmp, v_cmp, pos)
    n_blocks = -(-full.shape[1] // SEL_LEN)
    sel_idx, sel_ok = select_blocks(p_cmp, pos, n_blocks)
    o_sel = nsa_selected(q_rot, full[:, :, 2], full[:, :, 3], sel_idx, sel_ok, pos)
    if win_buf is None:
        o_win = window_prompt(q_rot, k_win, v_win)
        wb = min(WINDOW, T)
        win_new = jnp.stack([k_win[:, T - wb:], v_win[:, T - wb:]], axis=2)
    else:
        o_win, win_new = window_decode(q_rot, win_buf, k_win, v_win, pos)
    g = jax.nn.sigmoid(nsa_g.reshape(B, T, NSA_GROUPS, NSA_HPG, 3))
    o_nsa = g[..., 0:1] * o_cmp + g[..., 1:2] * o_sel + g[..., 2:3] * o_win
    gq = gq.reshape(B, T, GLA_HEADS, GLA_DK) * (GLA_DK ** -0.5)
    gk = gk.reshape(B, T, GLA_HEADS, GLA_DK)
    gv = gv.reshape(B, T, GLA_HEADS, GLA_DV)
    log_a = jax.nn.log_sigmoid((g_lr @ p['gla_w_a2'] + p['gla_b_a']).astype(jnp.float32)) / GLA_TAU
    log_a = log_a.reshape(B, T, GLA_HEADS, GLA_DK)
    if gla_s0 is None:
        gla_s0 = jnp.zeros((B, GLA_HEADS, GLA_DK, GLA_DV), x.dtype)
    o_gla, gla_s = gla_chunked(gq, gk, gv, log_a, gla_s0)
    o_gla = rms_norm(o_gla, p['gla_norm_g']) * jax.nn.silu(g_r.reshape(B, T, GLA_HEADS, GLA_DV))
    a = o_nsa.reshape(B, T, NSA_HEADS * HEAD_DIM) @ p['w_nsa_up']
    c = o_gla.reshape(B, T, GLA_HEADS * GLA_DV) @ p['w_gla_up']
    m_a, m_c = jnp.split(mg, 2, axis=-1)
    y = (jax.nn.sigmoid(m_a) * a + jax.nn.sigmoid(m_c) * c) @ p['w_out']
    return y, kv_rows, win_new, gla_s


def trunk_layer(h, pos, past_kv, win_buf, gla_s0, p):
    y, kv_rows, win_new, gla_s = token_mixer(h, pos, past_kv, win_buf, gla_s0, p)
    h = layer_norm(DN_ALPHA * h + y, p['ln1_g'], p['ln1_b'])
    f = jnp.square(jax.nn.relu(h @ p['mlp_w1'])) @ p['mlp_w2']
    h = layer_norm(DN_ALPHA * h + f, p['ln2_g'], p['ln2_b'])
    return h, kv_rows, win_new, gla_s


def setup_inputs(seed: int = 0) -> dict:
    key = jax.random.key(seed)
    ks = iter(jax.random.split(key, 40))

    def nrm(shape, scale):
        return jax.random.normal(next(ks), shape, jnp.float32) * scale

    n_pages = PAST_LEN // PAGE_SIZE
    used = DEC_BATCH * n_pages
    n_phys = used + max(1, used // 4)
    win_buf = min(WINDOW, PAST_LEN)
    page_table = jax.random.permutation(next(ks), n_phys)[:used].reshape(DEC_BATCH, n_pages).astype(jnp.int32)
    qw = NSA_HEADS * HEAD_DIM
    vw = GLA_HEADS * GLA_DV
    return {
        'x_prompt': nrm((BATCH, SEQ, D_MODEL), 1.0),
        'x_sample': nrm((DEC_BATCH, DEC_SEQ, D_MODEL), 1.0),
        'cache_nsa_kv': nrm((n_phys, DEPTH, PAGE_SIZE, N_KV_SLOTS, NSA_GROUPS, HEAD_DIM), 1.0),
        'cache_win_kv': nrm((DEC_BATCH, DEPTH, win_buf, 2, NSA_GROUPS, HEAD_DIM), 1.0),
        'state_gla': nrm((DEC_BATCH, DEPTH, GLA_HEADS, GLA_DK, GLA_DV), 0.1),
        'page_table': page_table,
        'ln_in_g': 1.0 + nrm((D_MODEL,), 0.02),
        'ln_in_b': nrm((D_MODEL,), 0.02),
        'w_in': nrm((DEPTH, D_MODEL, D_IN), D_MODEL ** -0.5),
        'cmp_k_pe': nrm((DEPTH, CMP_LEN, HEAD_DIM), 0.02),
        'cmp_k_w1': nrm((DEPTH, CMP_LEN * HEAD_DIM, CMP_HID), (CMP_LEN * HEAD_DIM) ** -0.5),
        'cmp_k_w2': nrm((DEPTH, CMP_HID, HEAD_DIM), CMP_HID ** -0.5),
        'cmp_v_pe': nrm((DEPTH, CMP_LEN, HEAD_DIM), 0.02),
        'cmp_v_w1': nrm((DEPTH, CMP_LEN * HEAD_DIM, CMP_HID), (CMP_LEN * HEAD_DIM) ** -0.5),
        'cmp_v_w2': nrm((DEPTH, CMP_HID, HEAD_DIM), CMP_HID ** -0.5),
        'gla_w_a2': nrm((DEPTH, GLA_RANK, GLA_HEADS * GLA_DK), GLA_RANK ** -0.5),
        'gla_b_a': nrm((DEPTH, GLA_HEADS * GLA_DK), 0.5),
        'gla_norm_g': 1.0 + nrm((DEPTH, GLA_DV), 0.02),
        'w_nsa_up': nrm((DEPTH, qw, D_MODEL), qw ** -0.5 * DN_BETA),
        'w_gla_up': nrm((DEPTH, vw, D_MODEL), vw ** -0.5 * DN_BETA),
        'w_out': nrm((DEPTH, D_MODEL, D_MODEL), D_MODEL ** -0.5 * DN_BETA),
        'ln1_g': 1.0 + nrm((DEPTH, D_MODEL), 0.02),
        'ln1_b': nrm((DEPTH, D_MODEL), 0.02),
        'mlp_w1': nrm((DEPTH, D_MODEL, D_FF), D_MODEL ** -0.5),
        'mlp_w2': nrm((DEPTH, D_FF, D_MODEL), D_FF ** -0.5 * DN_BETA),
        'ln2_g': 1.0 + nrm((DEPTH, D_MODEL), 0.02),
        'ln2_b': nrm((DEPTH, D_MODEL), 0.02),
    }


def reference(x_prompt, x_sample, cache_nsa_kv, cache_win_kv, state_gla, page_table,
              ln_in_g, ln_in_b, w_in, cmp_k_pe, cmp_k_w1, cmp_k_w2, cmp_v_pe, cmp_v_w1, cmp_v_w2,
              gla_w_a2, gla_b_a, gla_norm_g, w_nsa_up, w_gla_up, w_out,
              ln1_g, ln1_b, mlp_w1, mlp_w2, ln2_g, ln2_b):
    dec_b, n_pages = page_table.shape
    pos_p = jnp.arange(x_prompt.shape[1])
    pos_s = PAST_LEN + jnp.arange(x_sample.shape[1])
    hp = layer_norm(x_prompt, ln_in_g, ln_in_b)
    hs = layer_norm(x_sample, ln_in_g, ln_in_b)
    kv_p, win_p, gla_p, kv_s, win_s, gla_s = [], [], [], [], [], []
    for l in range(DEPTH):
        p = dict(w_in=w_in[l], cmp_k_pe=cmp_k_pe[l], cmp_k_w1=cmp_k_w1[l], cmp_k_w2=cmp_k_w2[l],
                 cmp_v_pe=cmp_v_pe[l], cmp_v_w1=cmp_v_w1[l], cmp_v_w2=cmp_v_w2[l],
                 gla_w_a2=gla_w_a2[l], gla_b_a=gla_b_a[l], gla_norm_g=gla_norm_g[l],
                 w_nsa_up=w_nsa_up[l], w_gla_up=w_gla_up[l], w_out=w_out[l],
                 ln1_g=ln1_g[l], ln1_b=ln1_b[l], mlp_w1=mlp_w1[l], mlp_w2=mlp_w2[l],
                 ln2_g=ln2_g[l], ln2_b=ln2_b[l])
        hp, kv, win, st = trunk_layer(hp, pos_p, None, None, None, p)
        kv_p.append(kv); win_p.append(win); gla_p.append(st)
        past = cache_nsa_kv[page_table, l].reshape(dec_b, n_pages * PAGE_SIZE, N_KV_SLOTS, NSA_GROUPS, HEAD_DIM)
        hs, kv, win, st = trunk_layer(hs, pos_s, past, cache_win_kv[:, l], state_gla[:, l], p)
        kv_s.append(kv); win_s.append(win); gla_s.append(st)
    nsa_kv_prompt = jnp.stack(kv_p, axis=1)
    win_kv_prompt = jnp.stack(win_p, axis=1)
    gla_state_prompt = jnp.stack(gla_p, axis=1)
    nsa_kv_sample = jnp.stack(kv_s, axis=1)
    win_kv_sample = jnp.stack(win_s, axis=1)
    gla_state_sample = jnp.stack(gla_s, axis=1)
    return (hp, hs, nsa_kv_prompt, win_kv_prompt, gla_state_prompt, nsa_kv_sample, win_kv_sample, gla_state_sample)
```

```python
import functools
import math

import numpy as np
import jax
import jax.numpy as jnp
from jax import lax
from jax.experimental import pallas as pl
from jax.experimental.pallas import tpu as pltpu

F32 = jnp.float32
BF16 = jnp.bfloat16

NSA_HEADS = 8
NSA_GROUPS = 2
NSA_HPG = NSA_HEADS // NSA_GROUPS
HEAD_DIM = 64
SCALE = HEAD_DIM ** -0.5
CMP_LEN = 32
CMP_STRIDE = 16
SEL_LEN = 64
N_SEL = 16
WINDOW = 512
GLA_HEADS = 4
GLA_RANK = 16
GLA_TAU = 16.0
ROPE_THETA = 10000.0
EPS = 1e-5
BIG = 1e6
NEG = -1e30

LANES = 128
ROW_TILE = 256
Q_TILE = 128
KV_TILE = 256
GLA_CHUNK = 256
GLA_DEC_TILE = 16
VMEM_LIMIT = 56 * 1024 * 1024


def _cparams(n_axes):
    return pltpu.CompilerParams(dimension_semantics=("arbitrary",) * n_axes,
                                vmem_limit_bytes=VMEM_LIMIT)


def _dot(a, b):
    return jnp.dot(a, b, preferred_element_type=F32)


def _dot_nt(a, b):
    return lax.dot_general(a, b, (((1,), (1,)), ((), ())), preferred_element_type=F32)


def _split3(x):
    hi = x.astype(BF16)
    r = x - hi.astype(F32)
    mid = r.astype(BF16)
    lo = (r - mid.astype(F32)).astype(BF16)
    return hi, mid, lo


def _layer_norm(x, g, b):
    mu = jnp.mean(x, axis=-1, keepdims=True)
    xc = x - mu
    var = jnp.mean(xc * xc, axis=-1, keepdims=True)
    return xc * lax.rsqrt(var + EPS) * g + b


def _sigmoid(x):
    return 1.0 / (1.0 + jnp.exp(-x))


def _ln_kernel(x_ref, g_ref, b_ref, o_ref):
    o_ref[...] = _layer_norm(x_ref[...], g_ref[...], b_ref[...])


def _entry_norm(x, g, b):
    n, d = x.shape
    return pl.pallas_call(
        _ln_kernel,
        grid=(n // ROW_TILE,),
        in_specs=[pl.BlockSpec((ROW_TILE, d), lambda i: (i, 0)),
                  pl.BlockSpec((1, d), lambda i: (0, 0)),
                  pl.BlockSpec((1, d), lambda i: (0, 0))],
        out_specs=pl.BlockSpec((ROW_TILE, d), lambda i: (i, 0)),
        out_shape=jax.ShapeDtypeStruct((n, d), F32),
        compiler_params=_cparams(1),
    )(x, g.reshape(1, d), b.reshape(1, d))


C_Q, C_KV, C_GQ, C_GK, C_GV, C_MISC, C_END = 0, 512, 1280, 1536, 1792, 2304, 2432


def _rope128(x, cs, sn):
    lane = lax.broadcasted_iota(jnp.int32, x.shape, 1)
    first = (lane % HEAD_DIM) < (HEAD_DIM // 2)
    swapped = jnp.where(first, pltpu.roll(x, LANES - HEAD_DIM // 2, 1), pltpu.roll(x, HEAD_DIM // 2, 1))
    return x * cs + swapped * sn


def _inproj_kernel(h_ref, w_ref, cs_ref, sn_ref, wa_ref, ba_ref,
                   qp_ref, qr_ref, kv_ref, win_ref, gq_ref, gk_ref, gv_ref, la_ref, ng_ref):
    hb = h_ref[...].astype(BF16)
    cs = cs_ref[...]
    sn = sn_ref[...]

    def seg(lo, hi):
        return _dot(hb, w_ref[:, lo:hi])

    for j in range(4):
        qj = seg(C_Q + j * LANES, C_Q + (j + 1) * LANES)
        qp_ref[:, j * LANES:(j + 1) * LANES] = qj
        qr_ref[:, j * LANES:(j + 1) * LANES] = _rope128(qj, cs, sn)
    for s in range(6):
        x = seg(C_KV + s * LANES, C_KV + (s + 1) * LANES)
        if s in (2, 4):
            x = _rope128(x, cs, sn)
        if s < 4:
            kv_ref[:, s * LANES:(s + 1) * LANES] = x
        else:
            win_ref[:, (s - 4) * LANES:(s - 3) * LANES] = x
    gq_ref[...] = seg(C_GQ, C_GK) * (HEAD_DIM ** -0.5)
    gk_ref[...] = seg(C_GK, C_GV)
    gv_ref[...] = seg(C_GV, C_MISC)
    misc = seg(C_MISC, C_END)
    ng_ref[...] = misc
    x = _dot(misc.astype(BF16), wa_ref[...]) + ba_ref[...]
    la_ref[...] = (jnp.minimum(x, 0.0) - jnp.log(1.0 + jnp.exp(-jnp.abs(x)))) * (1.0 / GLA_TAU)


def _inproj(h, w_a, cs_tab, sn_tab, wa_pad, ba, tiles_per_seq, n_prompt_tiles):
    n, d = h.shape
    tm = ROW_TILE

    def tab_map(i):
        return (jnp.where(i < n_prompt_tiles, i % tiles_per_seq, tiles_per_seq), 0)

    row = lambda w: pl.BlockSpec((tm, w), lambda i: (i, 0))
    full = lambda a: pl.BlockSpec(a.shape, lambda i: (0,) * a.ndim)
    widths = (512, 512, 512, 256, 256, 256, 512, 256, 128)
    return pl.pallas_call(
        _inproj_kernel,
        grid=(n // tm,),
        in_specs=[row(d), full(w_a), pl.BlockSpec((tm, LANES), tab_map), pl.BlockSpec((tm, LANES), tab_map),
                  full(wa_pad), full(ba)],
        out_specs=[row(w) for w in widths],
        out_shape=[jax.ShapeDtypeStruct((n, w), F32) for w in widths],
        compiler_params=_cparams(1),
    )(h, w_a, cs_tab, sn_tab, wa_pad, ba)


def _gelu_tanh(x):
    return 0.5 * x * (1.0 + jnp.tanh(math.sqrt(2.0 / math.pi) * (x + 0.044715 * x * x * x)))


def _compress_rows(x_refs, wc_ref, pe_ref, w2_ref):
    n_chunks = x_refs[0].shape[0] // CMP_STRIDE
    outs = []
    for s, x_ref in enumerate(x_refs):
        acc = [jnp.zeros((n_chunks, 2 * LANES), F32) for _ in range(NSA_GROUPS)]
        for lp in range(CMP_STRIDE // 2):
            a = jnp.concatenate(
                [x_ref[pl.ds(2 * lp, n_chunks, stride=CMP_STRIDE), :],
                 x_ref[pl.ds(2 * lp + 1, n_chunks, stride=CMP_STRIDE), :]], axis=1)
            a = a.astype(BF16)
            for g in range(NSA_GROUPS):
                acc[g] = acc[g] + _dot(a, wc_ref[s, g, lp])
        row = lax.broadcasted_iota(jnp.int32, (n_chunks, LANES), 0)
        parts = []
        for g in range(NSA_GROUPS):
            hid = acc[g][:, :LANES] + pltpu.roll(acc[g][:, LANES:], n_chunks - 1, 0) + pe_ref[s]
            parts.append(_dot(_gelu_tanh(hid).astype(BF16), w2_ref[s]))
        out = jnp.concatenate(parts, axis=1)
        outs.append(jnp.where(row < n_chunks - 1, out, 0.0))
    return outs


def _pe_bias_kernel(pe_ref, w1_ref, o_ref):
    for s in range(2):
        o_ref[s] = _dot(pe_ref[s].astype(BF16), w1_ref[s].astype(BF16))[0:1]


def _pe_bias(pe_flat, w1):
    hid = w1.shape[2]
    return pl.pallas_call(
        _pe_bias_kernel,
        out_shape=jax.ShapeDtypeStruct((2, 1, hid), F32),
    )(pe_flat, w1)


def _compress_kernel(xk_ref, xv_ref, wc_ref, pe_ref, w2_ref, kc_ref, vc_ref):
    kc, vc = _compress_rows((xk_ref, xv_ref), wc_ref, pe_ref, w2_ref)
    kc_ref[0] = kc
    vc_ref[0] = vc


def _compress_prompt(kv, wc, pe_pair, w2, batch, seq):
    nc = seq // CMP_STRIDE
    full = lambda a: pl.BlockSpec(a.shape, lambda b: (0,) * a.ndim)
    return pl.pallas_call(
        _compress_kernel,
        grid=(batch,),
        in_specs=[pl.BlockSpec((seq, LANES), lambda b: (b, 0)), pl.BlockSpec((seq, LANES), lambda b: (b, 1)),
                  full(wc), full(pe_pair), full(w2)],
        out_specs=[pl.BlockSpec((1, nc, LANES), lambda b: (b, 0, 0))] * 2,
        out_shape=[jax.ShapeDtypeStruct((batch, nc, LANES), F32)] * 2,
        compiler_params=_cparams(1),
    )(kv, kv, wc, pe_pair, w2)


def _select_mask(imp_t, qpos, n_blocks):
    nbp, nq = imp_t.shape
    blk = lax.broadcasted_iota(jnp.int32, (nbp, nq), 0)
    cur = qpos // SEL_LEN
    causal = blk * SEL_LEN <= qpos
    forced = (blk == 0) | (blk == cur) | (blk == cur - 1)
    score = jnp.where(forced, BIG, jnp.where(causal, imp_t, -jnp.inf))
    score = jnp.where(blk < n_blocks, score, -jnp.inf)
    rank = jnp.zeros((nbp, nq), jnp.int32)
    for j in range(n_blocks):
        other = score[j:j + 1, :]
        rank = rank + jnp.where(blk > j, jnp.where(other >= score, 1, 0), jnp.where(other > score, 1, 0))
    return (rank < N_SEL) & (score > -jnp.inf)


def _flash_t(k_ref, vt_ref, qh, kt_lo, kt_hi, valid_fn):
    nq = qh.shape[0]

    def body(kt, carry):
        m, l, acc = carry
        s = _dot_nt(k_ref[kt], qh)
        s = jnp.where(valid_fn(kt), s, NEG)
        m_new = jnp.maximum(m, jnp.max(s, axis=0, keepdims=True))
        alpha = jnp.exp(m - m_new)
        p = jnp.exp(s - m_new)
        l = alpha * l + jnp.sum(p, axis=0, keepdims=True)
        acc = alpha * acc + _dot(vt_ref[kt], p.astype(BF16))
        return m_new, l, acc

    init = (jnp.full((1, nq), NEG, F32), jnp.zeros((1, nq), F32), jnp.zeros((LANES, nq), F32))
    _, l, acc = lax.fori_loop(kt_lo, kt_hi, body, init)
    return acc, l


def _nsa_prompt_kernel(qp_ref, qr_ref, ng_ref, kc_ref, vc_ref, ks_ref, vs_ref, kw_ref, vw_ref, ovl_ref,
                       o_ref, ksa_ref, vst_ref, kwa_ref, vwt_ref, *, seq, n_blocks):
    g = pl.program_id(1)
    qi = pl.program_id(2)
    tq = Q_TILE
    n_kt = seq // KV_TILE
    nq = NSA_HPG * tq

    @pl.when((g == 0) & (qi == 0))
    def _():
        for kt in range(n_kt):
            rows = pl.ds(kt * KV_TILE, KV_TILE)
            key = kt * KV_TILE + lax.broadcasted_iota(jnp.int32, (KV_TILE, LANES), 0)
            lane = lax.broadcasted_iota(jnp.int32, (KV_TILE, LANES), 1)
            onehot = jnp.where(key // SEL_LEN == lane, 1.0, 0.0).astype(BF16)
            ksa_ref[kt] = jnp.concatenate([ks_ref[rows, :].astype(BF16), onehot], axis=1)
            vst_ref[kt] = vs_ref[rows, :].T.astype(BF16)
            kwa_ref[kt] = kw_ref[rows, :].astype(BF16)
            vwt_ref[kt] = vw_ref[rows, :].T.astype(BF16)

    qs = qi * tq
    lane_q = lax.broadcasted_iota(jnp.int32, (1, nq), 1)
    qpos = qs + lane_q % tq
    qpos1 = qs + lax.broadcasted_iota(jnp.int32, (1, tq), 1)
    lane = lax.broadcasted_iota(jnp.int32, (tq, LANES), 1)
    own = (lane // HEAD_DIM) == g

    def stack_heads(ref):
        parts = []
        for h in range(NSA_HPG):
            qh = ref[:, h * HEAD_DIM:(h + 1) * HEAD_DIM] * SCALE
            parts.append(jnp.where(own, jnp.concatenate([qh, qh], axis=1), 0.0))
        return parts

    def own_rows(x):
        return jnp.where(g == 0, x[:HEAD_DIM], x[HEAD_DIM:])

    nc = kc_ref.shape[1]
    qp = jnp.concatenate(stack_heads(qp_ref), axis=0).astype(BF16)
    s = _dot_nt(kc_ref[0].astype(BF16), qp)
    cblk = lax.broadcasted_iota(jnp.int32, (nc, nq), 0)
    valid = cblk * CMP_STRIDE + (CMP_LEN - 1) <= qpos
    s = jnp.where(valid, s, NEG)
    m = jnp.max(s, axis=0, keepdims=True)
    p = jnp.where(valid, jnp.exp(s - m), 0.0)
    p = p / jnp.maximum(jnp.sum(p, axis=0, keepdims=True), 1e-30)
    o_cmp = own_rows(_dot(vc_ref[0].T.astype(BF16), p.astype(BF16)))
    psum = p[:, 0:tq]
    for h in range(1, NSA_HPG):
        psum = psum + p[:, h * tq:(h + 1) * tq]
    ovl = ovl_ref[...]
    imp_t = sum(_dot(ovl, piece) for piece in _split3(psum))
    nbp = -(-n_blocks // 8) * 8
    sel = _select_mask(imp_t[:nbp], qpos1, n_blocks)
    bias_t = jnp.where(sel, 0.0, NEG)
    bias_t = jnp.concatenate([bias_t, jnp.zeros((LANES - nbp, tq), F32)], axis=0)
    bias = bias_t.T

    qr_parts = stack_heads(qr_ref)
    q_sel = jnp.concatenate([jnp.concatenate([q, bias], axis=1) for q in qr_parts], axis=0).astype(BF16)
    q_win = jnp.concatenate(qr_parts, axis=0).astype(BF16)
    krow = lax.broadcasted_iota(jnp.int32, (KV_TILE, nq), 0)

    def causal_ok(kt):
        return kt * KV_TILE + krow <= qpos

    def band_ok(kt):
        d = qpos - (kt * KV_TILE + krow)
        return (d >= 0) & (d <= WINDOW)

    kt_hi = (qs + tq - 1) // KV_TILE + 1
    acc, l = _flash_t(ksa_ref, vst_ref, q_sel, 0, kt_hi, causal_ok)
    o_sel = own_rows(acc / l)
    kt_lo = jnp.maximum(qs - WINDOW, 0) // KV_TILE
    acc, l = _flash_t(kwa_ref, vwt_ref, q_win, kt_lo, kt_hi, band_ok)
    o_win = own_rows(acc / l)

    ng_t = ng_ref[...].T
    outs = []
    for h in range(NSA_HPG):
        cols = slice(h * tq, (h + 1) * tq)
        tot = jnp.zeros((HEAD_DIM, tq), F32)
        for r, o in enumerate((o_cmp, o_sel, o_win)):
            i0 = h * 3 + r
            i1 = (NSA_HPG + h) * 3 + r
            gate = _sigmoid(jnp.where(g == 0, ng_t[i0:i0 + 1], ng_t[i1:i1 + 1]))
            tot = tot + gate * o[:, cols]
        outs.append(tot.T)
    o_ref[...] = jnp.concatenate(outs, axis=1)


def _nsa_prompt(qp, qr, ng, kc, vc, kv, win, ovl, batch, seq):
    n_blocks = -(-seq // SEL_LEN)
    nqt = seq // Q_TILE
    n_kt = seq // KV_TILE
    gw = NSA_HPG * HEAD_DIM
    qspec = pl.BlockSpec((Q_TILE, gw), lambda b, g, i: (b * nqt + i, g))
    cspec = pl.BlockSpec((1,) + kc.shape[1:], lambda b, g, i: (b, 0, 0))

    def kvspec(col):
        return pl.BlockSpec((seq, LANES), lambda b, g, i: (b, col))

    return pl.pallas_call(
        functools.partial(_nsa_prompt_kernel, seq=seq, n_blocks=n_blocks),
        grid=(batch, NSA_GROUPS, nqt),
        in_specs=[qspec, qspec, pl.BlockSpec((Q_TILE, LANES), lambda b, g, i: (b * nqt + i, 0)),
                  cspec, cspec, kvspec(2), kvspec(3), kvspec(0), kvspec(1),
                  pl.BlockSpec(ovl.shape, lambda b, g, i: (0, 0))],
        out_specs=qspec,
        out_shape=jax.ShapeDtypeStruct(qp.shape, F32),
        scratch_shapes=[pltpu.VMEM((n_kt, KV_TILE, 2 * LANES), BF16), pltpu.VMEM((n_kt, LANES, KV_TILE), BF16),
                        pltpu.VMEM((n_kt, KV_TILE, LANES), BF16), pltpu.VMEM((n_kt, LANES, KV_TILE), BF16)],
        compiler_params=_cparams(3),
    )(qp, qr, ng, kc, vc, kv, kv, win, win, ovl)


def _gla_level_matrix(c):
    t = np.arange(c)[:, None]
    u = np.arange(c)[None, :]
    mats = [(u <= t), (u > t)]
    m = c
    while m >= 2:
        split = (t // m) * m + m // 2
        upper = (t % m) >= m // 2
        mats.append(np.where(upper, (u >= split) & (u <= t), (u > t) & (u < split)))
        m //= 2
    return np.concatenate(mats, axis=0).astype(np.float32)


def _gla_prompt_kernel(q_ref, k_ref, v_ref, la_ref, w_ref, o_ref, st_ref, e_ref, s_ref, *, seq):
    c = GLA_CHUNK
    n_levels = int(math.log2(c))
    s_ref[...] = jnp.zeros_like(s_ref)
    row = lax.broadcasted_iota(jnp.int32, (c, c), 0)
    col = lax.broadcasted_iota(jnp.int32, (c, c), 1)
    rowl = lax.broadcasted_iota(jnp.int32, (c, LANES), 0)
    lanel = lax.broadcasted_iota(jnp.int32, (c, LANES), 1)

    def chunk(ci, _):
        rows = pl.ds(pl.multiple_of(ci * c, c), c)
        la = la_ref[rows, :]
        hi = la.astype(BF16)
        lo = (la - hi.astype(F32)).astype(BF16)
        e_ref[...] = jnp.exp(_dot(w_ref[...], hi) + _dot(w_ref[...], lo))
        for pair in range(GLA_HEADS // 2):
            lanes = slice(pair * LANES, (pair + 1) * LANES)
            q = q_ref[rows, lanes]
            k = k_ref[rows, lanes]
            q0 = (q * e_ref[0:c, lanes]).astype(BF16)
            kdec = k * e_ref[c:2 * c, lanes]
            a_last = e_ref[c - 1:c, lanes]
            qls, kls = [], []
            for lv in range(n_levels):
                m = c >> lv
                x = e_ref[(2 + lv) * c:(3 + lv) * c, lanes]
                upper = (rowl % m) >= (m // 2)
                qls.append(jnp.where(upper, q * x, 0.0).astype(BF16))
                kls.append(jnp.where(upper, 0.0, k * x))
            for hh in range(2):
                head = pair * 2 + hh
                mine = (lanel // HEAD_DIM) == hh
                a = jnp.where(row == col, _dot_nt(q.astype(BF16), jnp.where(mine, k, 0.0).astype(BF16)), 0.0)
                for lv in range(n_levels):
                    m = c >> lv
                    same = (row // m) == (col // m)
                    a = a + jnp.where(same, _dot_nt(qls[lv], jnp.where(mine, kls[lv], 0.0).astype(BF16)), 0.0)
                v = v_ref[rows, head * LANES:(head + 1) * LANES]
                st = s_ref[head]
                o = _dot(a.astype(BF16), v.astype(BF16)) + _dot_nt(q0, st.astype(BF16))
                o_ref[rows, head * LANES:(head + 1) * LANES] = o
                kd = jnp.where(mine, kdec, 0.0).astype(BF16)
                s_ref[head] = st * a_last + _dot(v.T.astype(BF16), kd)
        return 0

    lax.fori_loop(0, seq // c, chunk, 0)
    for head in range(GLA_HEADS):
        st = s_ref[head].T
        off = (head % 2) * HEAD_DIM
        st_ref[0, head] = st[off:off + HEAD_DIM]


def _gla_prompt(gq, gk, gv, la, wlev, batch, seq):
    dk2 = gq.shape[1]
    dv4 = gv.shape[1]
    return pl.pallas_call(
        functools.partial(_gla_prompt_kernel, seq=seq),
        grid=(batch,),
        in_specs=[pl.BlockSpec((seq, dk2), lambda b: (b, 0)), pl.BlockSpec((seq, dk2), lambda b: (b, 0)),
                  pl.BlockSpec((seq, dv4), lambda b: (b, 0)), pl.BlockSpec((seq, dk2), lambda b: (b, 0)),
                  pl.BlockSpec(wlev.shape, lambda b: (0, 0))],
        out_specs=[pl.BlockSpec((seq, dv4), lambda b: (b, 0)),
                   pl.BlockSpec((1, GLA_HEADS, HEAD_DIM, LANES), lambda b: (b, 0, 0, 0))],
        out_shape=[jax.ShapeDtypeStruct((gq.shape[0], dv4), F32),
                   jax.ShapeDtypeStruct((batch, GLA_HEADS, HEAD_DIM, LANES), F32)],
        scratch_shapes=[pltpu.VMEM((wlev.shape[0], dk2), F32), pltpu.VMEM((GLA_HEADS, LANES, LANES), F32)],
        compiler_params=_cparams(1),
    )(gq, gk, gv, la, wlev)


def _outproj_kernel(h_ref, on_ref, og_ref, wb_ref, gn_ref, wn_ref, wg_ref, wo_ref, g1_ref, b1_ref, o_ref,
                    *, alpha):
    h = h_ref[...]
    hb = h.astype(BF16)
    dm = h.shape[1]
    gw = og_ref.shape[1]
    g_r = _dot(hb, wb_ref[:, :gw])
    parts = []
    for head in range(GLA_HEADS):
        x = og_ref[:, head * LANES:(head + 1) * LANES]
        x = x * lax.rsqrt(jnp.mean(x * x, axis=-1, keepdims=True) + EPS) * gn_ref[...]
        gr = g_r[:, head * LANES:(head + 1) * LANES]
        parts.append(x * (gr * _sigmoid(gr)))
    og = jnp.concatenate(parts, axis=1).astype(BF16)
    a = _dot(on_ref[...].astype(BF16), wn_ref[...])
    c = _dot(og, wg_ref[...])
    m_a = _dot(hb, wb_ref[:, gw:gw + dm])
    m_c = _dot(hb, wb_ref[:, gw + dm:gw + 2 * dm])
    mix = (_sigmoid(m_a) * a + _sigmoid(m_c) * c).astype(BF16)
    y = _dot(mix, wo_ref[...])
    o_ref[...] = _layer_norm(alpha * h + y, g1_ref[...], b1_ref[...])


def _outproj(h, o_nsa, o_gla, w_b, gn, w_nsa, w_gla, w_out, g1, b1, alpha):
    n, d = h.shape
    tm = ROW_TILE
    row = lambda w: pl.BlockSpec((tm, w), lambda i: (i, 0))
    full = lambda a: pl.BlockSpec(a.shape, lambda i: (0,) * a.ndim)
    return pl.pallas_call(
        functools.partial(_outproj_kernel, alpha=alpha),
        grid=(n // tm,),
        in_specs=[row(d), row(o_nsa.shape[1]), row(o_gla.shape[1]), full(w_b), full(gn), full(w_nsa),
                  full(w_gla), full(w_out), full(g1), full(b1)],
        out_specs=row(d),
        out_shape=jax.ShapeDtypeStruct((n, d), F32),
        compiler_params=_cparams(1),
    )(h, o_nsa, o_gla, w_b, gn, w_nsa, w_gla, w_out, g1, b1)


def _mlp_kernel(h_ref, w1_ref, w2_ref, g_ref, b_ref, o_ref, *, alpha):
    h = h_ref[...]
    hb = h.astype(BF16)
    dff = w1_ref.shape[1]
    step = 1024
    f = jnp.zeros(h.shape, F32)
    for c0 in range(0, dff, step):
        u = jnp.maximum(_dot(hb, w1_ref[:, c0:c0 + step]), 0.0)
        f = f + _dot((u * u).astype(BF16), w2_ref[c0:c0 + step, :])
    o_ref[...] = _layer_norm(alpha * h + f, g_ref[...], b_ref[...])


def _mlp(h, w1, w2, g, b, alpha):
    n, d = h.shape
    tm = ROW_TILE
    row = pl.BlockSpec((tm, d), lambda i: (i, 0))
    full = lambda a: pl.BlockSpec(a.shape, lambda i: (0,) * a.ndim)
    return pl.pallas_call(
        functools.partial(_mlp_kernel, alpha=alpha),
        grid=(n // tm,),
        in_specs=[row, full(w1), full(w2), full(g), full(b)],
        out_specs=row,
        out_shape=jax.ShapeDtypeStruct((n, d), F32),
        compiler_params=_cparams(1),
    )(h, w1, w2, g, b)


def _nsa_decode_kernel(pt_ref, *refs, n_pages, n_blocks, past, wbuf):
    del pt_ref
    pages = refs[:n_pages]
    (wc_ref, pe_ref, w2_ref, qp_ref, qr_ref, ng_ref, kvn_ref, winn_ref, wb_ref, ovl_ref, gg_ref, selr_ref,
     o_ref, xk_ref, xv_ref, ksa_ref, vst_ref, kwa_ref, vwt_ref) = refs[n_pages:]
    page = pages[0].shape[0]
    nq = LANES
    n_kt = past // KV_TILE
    n_wt = wbuf // KV_TILE
    tile_row = lax.broadcasted_iota(jnp.int32, (KV_TILE, LANES), 0)
    tile_lane = lax.broadcasted_iota(jnp.int32, (KV_TILE, LANES), 1)

    for p in range(n_pages):
        xk_ref[p * page:(p + 1) * page, :] = pages[p][:, 0:LANES]
        xv_ref[p * page:(p + 1) * page, :] = pages[p][:, LANES:2 * LANES]
    kc, vc = _compress_rows((xk_ref, xv_ref), wc_ref, pe_ref, w2_ref)

    def first_row_tile(rowvec):
        return jnp.where(tile_row == 0, jnp.broadcast_to(rowvec, (KV_TILE, LANES)), 0.0)

    kvn = kvn_ref[0]
    for kt in range(n_kt + 1):
        if kt < n_kt:
            src = [pages[(kt * KV_TILE) // page + i] for i in range(KV_TILE // page)]
            k = jnp.concatenate([r[:, 2 * LANES:3 * LANES] for r in src], axis=0)
            v = jnp.concatenate([r[:, 3 * LANES:4 * LANES] for r in src], axis=0)
        else:
            k = first_row_tile(kvn[:, 2 * LANES:3 * LANES])
            v = first_row_tile(kvn[:, 3 * LANES:4 * LANES])
        onehot = jnp.where((kt * KV_TILE + tile_row) // SEL_LEN == tile_lane, 1.0, 0.0).astype(BF16)
        ksa_ref[kt] = jnp.concatenate([k.astype(BF16), onehot], axis=1)
        vst_ref[kt] = v.T.astype(BF16)
    winn = winn_ref[0]
    for kt in range(n_wt + 1):
        if kt < n_wt:
            rows = pl.ds(kt * KV_TILE, KV_TILE)
            k = wb_ref[rows, 0:LANES]
            v = wb_ref[rows, LANES:2 * LANES]
        else:
            k = first_row_tile(winn[:, 0:LANES])
            v = first_row_tile(winn[:, LANES:2 * LANES])
        kwa_ref[kt] = k.astype(BF16)
        vwt_ref[kt] = v.T.astype(BF16)

    n_rows = qp_ref.shape[1]
    zpad = jnp.zeros((nq - n_rows, LANES), F32)
    qp = jnp.concatenate([qp_ref[0] * SCALE, zpad], axis=0)
    qr = jnp.concatenate([qr_ref[0] * SCALE, zpad], axis=0)
    qpos = jnp.full((1, nq), past, jnp.int32)

    nc = kc.shape[0]
    s = _dot_nt(kc.astype(BF16), qp.astype(BF16))
    cblk = lax.broadcasted_iota(jnp.int32, (nc, nq), 0)
    valid = cblk * CMP_STRIDE + (CMP_LEN - 1) <= qpos
    s = jnp.where(valid, s, NEG)
    m = jnp.max(s, axis=0, keepdims=True)
    p = jnp.where(valid, jnp.exp(s - m), 0.0)
    p = p / jnp.maximum(jnp.sum(p, axis=0, keepdims=True), 1e-30)
    o_cmp = _dot(vc.T.astype(BF16), p.astype(BF16))
    imp_h = sum(_dot(ovl_ref[...], piece) for piece in _split3(p))
    imp_t = sum(_dot(piece, gg_ref[...]) for piece in _split3(imp_h))
    nbp = -(-n_blocks // 8) * 8
    sel = _select_mask(imp_t[:nbp], qpos, n_blocks)
    bias_t = jnp.where(sel, 0.0, NEG)
    bias_t = jnp.concatenate([bias_t, jnp.zeros((LANES - nbp, nq), F32)], axis=0)
    q_sel = jnp.concatenate([qr, bias_t.T], axis=1).astype(BF16)
    q_win = qr.astype(BF16)
    krow = lax.broadcasted_iota(jnp.int32, (KV_TILE, nq), 0)

    def causal_ok(kt):
        return kt * KV_TILE + krow <= qpos

    def band_ok(kt):
        d = wbuf - (kt * KV_TILE + krow)
        return (d >= 0) & (d <= WINDOW)

    acc, l = _flash_t(ksa_ref, vst_ref, q_sel, 0, n_kt + 1, causal_ok)
    o_sel = acc / l
    acc, l = _flash_t(kwa_ref, vwt_ref, q_win, 0, n_wt + 1, band_ok)
    o_win = acc / l

    sg = jnp.broadcast_to(_sigmoid(ng_ref[0]), (8, LANES))
    pieces = _split3(sg)
    tot = jnp.zeros((LANES, nq), F32)
    for r, o in enumerate((o_cmp, o_sel, o_win)):
        gate = sum(_dot(piece, selr_ref[r]) for piece in pieces)[0:1]
        tot = tot + gate * o
    o_ref[0] = tot.T[:n_rows]


def _nsa_decode(page_table, cache, layer, wc, pe_pair, w2, qp8, qr8, ng3, kvn3, winn3, wbuf_arr, ovl, gg, selr):
    dec_b, n_pages = page_table.shape
    page = cache.shape[2]
    past = n_pages * page
    wbuf = wbuf_arr.shape[2]
    n_blocks = -(-(past + 1) // SEL_LEN)
    n_kt = past // KV_TILE
    n_wt = wbuf // KV_TILE
    full = lambda a: pl.BlockSpec(a.shape, lambda b, pt: (0,) * a.ndim)
    per_b = lambda a: pl.BlockSpec((1,) + a.shape[1:], lambda b, pt: (b,) + (0,) * (a.ndim - 1))

    def page_spec(p):
        return pl.BlockSpec((None, None, page, cache.shape[3]), lambda b, pt: (pt[b, p], layer, 0, 0))

    grid_spec = pltpu.PrefetchScalarGridSpec(
        num_scalar_prefetch=1,
        grid=(dec_b,),
        in_specs=[page_spec(p) for p in range(n_pages)] + [
            full(wc), full(pe_pair), full(w2), per_b(qp8), per_b(qr8), per_b(ng3), per_b(kvn3), per_b(winn3),
            pl.BlockSpec((None, None, wbuf, wbuf_arr.shape[3]), lambda b, pt: (b, layer, 0, 0)),
            full(ovl), full(gg), full(selr)],
        out_specs=per_b(qp8),
        scratch_shapes=[pltpu.VMEM((past, LANES), F32), pltpu.VMEM((past, LANES), F32),
                        pltpu.VMEM((n_kt + 1, KV_TILE, 2 * LANES), BF16), pltpu.VMEM((n_kt + 1, LANES, KV_TILE), BF16),
                        pltpu.VMEM((n_wt + 1, KV_TILE, LANES), BF16), pltpu.VMEM((n_wt + 1, LANES, KV_TILE), BF16)])
    return pl.pallas_call(
        functools.partial(_nsa_decode_kernel, n_pages=n_pages, n_blocks=n_blocks, past=past, wbuf=wbuf),
        grid_spec=grid_spec,
        out_shape=jax.ShapeDtypeStruct(qp8.shape, F32),
        compiler_params=_cparams(1),
    )(page_table, *([cache] * n_pages), wc, pe_pair, w2, qp8, qr8, ng3, kvn3, winn3, wbuf_arr, ovl, gg, selr)


def _gla_decode_kernel(q_ref, k_ref, la_ref, v_ref, s_ref, o_ref, so_ref, qt_ref, kt_ref, at_ref):
    i = pl.program_id(0)
    bt = GLA_DEC_TILE
    n_tiles = qt_ref.shape[0]

    @pl.when(i == 0)
    def _():
        qt = q_ref[...].T
        kt = k_ref[...].T
        at = jnp.exp(la_ref[...]).T
        for j in range(n_tiles):
            qt_ref[j] = qt[:, j * bt:(j + 1) * bt]
            kt_ref[j] = kt[:, j * bt:(j + 1) * bt]
            at_ref[j] = at[:, j * bt:(j + 1) * bt]

    qt = qt_ref[i]
    kt = kt_ref[i]
    at = at_ref[i]
    for bb in range(bt):
        for head in range(GLA_HEADS):
            rows = slice(head * HEAD_DIM, (head + 1) * HEAD_DIM)
            v = v_ref[bb:bb + 1, head * LANES:(head + 1) * LANES]
            st = at[rows, bb:bb + 1] * s_ref[bb, 0, head] + kt[rows, bb:bb + 1] * v
            so_ref[bb, head] = st
            o_ref[bb:bb + 1, head * LANES:(head + 1) * LANES] = jnp.sum(qt[rows, bb:bb + 1] * st, axis=0,
                                                                        keepdims=True)


def _gla_decode(gq_s, gk_s, la_s, gv_s, state, layer):
    dec_b, dk4 = gq_s.shape
    bt = GLA_DEC_TILE
    n_tiles = dec_b // bt
    full = lambda a: pl.BlockSpec(a.shape, lambda i: (0,) * a.ndim)
    sblk = (bt, None, GLA_HEADS, HEAD_DIM, LANES)
    return pl.pallas_call(
        _gla_decode_kernel,
        grid=(n_tiles,),
        in_specs=[full(gq_s), full(gk_s), full(la_s), pl.BlockSpec((bt, gv_s.shape[1]), lambda i: (i, 0)),
                  pl.BlockSpec((bt, 1, GLA_HEADS, HEAD_DIM, LANES), lambda i: (i, layer, 0, 0, 0))],
        out_specs=[pl.BlockSpec((bt, gv_s.shape[1]), lambda i: (i, 0)),
                   pl.BlockSpec((bt, GLA_HEADS, HEAD_DIM, LANES), lambda i: (i, 0, 0, 0))],
        out_shape=[jax.ShapeDtypeStruct(gv_s.shape, F32),
                   jax.ShapeDtypeStruct((dec_b, GLA_HEADS, HEAD_DIM, LANES), F32)],
        scratch_shapes=[pltpu.VMEM((n_tiles, dk4, bt), F32)] * 3,
        compiler_params=_cparams(1),
    )(gq_s, gk_s, la_s, gv_s, state)


def _overlap_t():
    r, w = SEL_LEN // CMP_STRIDE, CMP_LEN // CMP_STRIDE
    off = (np.arange(r)[:, None] + np.arange(w)[None, :]).reshape(-1)
    j = np.arange(LANES)
    c = np.arange(LANES)
    ov = np.sum(c[None, :, None] == (r * j[:, None, None] + off[None, None, :]), axis=-1)
    return ov.astype(np.float32)


def _rope_tables(seq, past, n_tab_rows):
    half = HEAD_DIM // 2
    inv = ROPE_THETA ** (-jnp.arange(half, dtype=F32) / half)
    pos = jnp.concatenate([jnp.arange(seq, dtype=F32), jnp.full((n_tab_rows - seq,), past, F32)])
    ang = pos[:, None] * inv[None, :]
    cos, sin = jnp.cos(ang), jnp.sin(ang)
    cs = jnp.concatenate([cos, cos, cos, cos], axis=1)
    sn = jnp.concatenate([-sin, sin, -sin, sin], axis=1)
    return cs, sn


def _compress_weights(pe, w1, w2):
    dh = HEAD_DIM
    w1r = w1.reshape(CMP_LEN, dh, -1)
    hid = w1r.shape[-1]
    z = jnp.zeros((dh, hid), w1.dtype)
    groups = []
    for g in range(NSA_GROUPS):
        mats = []
        for lp in range(CMP_STRIDE // 2):
            halves = []
            for base in (0, CMP_STRIDE):
                blocks = []
                for l in (2 * lp, 2 * lp + 1):
                    blocks += [w1r[base + l], z] if g == 0 else [z, w1r[base + l]]
                halves.append(jnp.concatenate(blocks, axis=0))
            mats.append(jnp.concatenate(halves, axis=1))
        groups.append(jnp.stack(mats))
    wc = jnp.stack(groups).astype(BF16)
    pe_flat = jnp.broadcast_to(pe.reshape(1, -1), (8, pe.size))
    return wc, pe_flat, w2.astype(BF16)


def _pair_rows(x, n_rows):
    b = x.shape[0]
    xh = x.reshape(b, NSA_GROUPS, NSA_HPG, HEAD_DIM)
    z = jnp.zeros_like(xh[:, 0])
    g0 = jnp.concatenate([xh[:, 0], z], axis=-1)
    g1 = jnp.concatenate([z, xh[:, 1]], axis=-1)
    return jnp.concatenate([g0, g1], axis=1)[:, :n_rows]


def kernel(x_prompt, x_sample, cache_nsa_kv, cache_win_kv, state_gla, page_table, ln_in_g, ln_in_b, w_in, cmp_k_pe, cmp_k_w1, cmp_k_w2, cmp_v_pe, cmp_v_w1, cmp_v_w2, gla_w_a2, gla_b_a, gla_norm_g, w_nsa_up, w_gla_up, w_out, ln1_g, ln1_b, mlp_w1, mlp_w2, ln2_g, ln2_b):
    batch, seq, dm = x_prompt.shape
    dec_b = x_sample.shape[0]
    depth = w_in.shape[0]
    n_phys, _, page = cache_nsa_kv.shape[:3]
    past = page_table.shape[1] * page
    wbuf = cache_win_kv.shape[2]
    alpha = (2.0 * depth) ** 0.25
    n_p = batch * seq
    n_s_pad = -(-dec_b // ROW_TILE) * ROW_TILE
    n = n_p + n_s_pad
    qw = NSA_HEADS * HEAD_DIM
    kvw = 4 * NSA_GROUPS * HEAD_DIM

    x = jnp.concatenate([x_prompt.reshape(n_p, dm), x_sample.reshape(dec_b, dm),
                         jnp.zeros((n_s_pad - dec_b, dm), F32)], axis=0)
    h = _entry_norm(x, ln_in_g, ln_in_b)

    cs_tab, sn_tab = _rope_tables(seq, past, seq + ROW_TILE)
    ovl = jnp.asarray(_overlap_t())
    wlev = jnp.asarray(_gla_level_matrix(GLA_CHUNK)).astype(BF16)
    col = np.arange(LANES)
    gg = jnp.asarray(((col[:, None] // NSA_HPG == col[None, :] // NSA_HPG)
                      & (col[:, None] < NSA_HEADS) & (col[None, :] < NSA_HEADS)).astype(np.float32)).astype(BF16)
    selr = jnp.asarray(np.stack([(col[:, None] == col[None, :] * 3 + r) & (col[None, :] < NSA_HEADS)
                                 for r in range(3)]).astype(np.float32)).astype(BF16)
    cache = cache_nsa_kv.reshape(n_phys, depth, page, kvw)
    wbuf_arr = cache_win_kv.reshape(dec_b, depth, wbuf, 2 * NSA_GROUPS * HEAD_DIM)

    sizes = (qw, 6 * NSA_GROUPS * HEAD_DIM, 3 * NSA_HEADS, GLA_HEADS * HEAD_DIM, GLA_HEADS * HEAD_DIM,
             GLA_HEADS * LANES, GLA_RANK, GLA_HEADS * LANES, 2 * dm)
    pts = np.concatenate([[0], np.cumsum(sizes)])
    seg = lambda w, i: w[:, pts[i]:pts[i + 1]]

    kv_p, win_p, gla_p, kv_s, win_s, gla_s = [], [], [], [], [], []
    for l in range(depth):
        wl = w_in[l]
        misc = jnp.concatenate([seg(wl, 2), seg(wl, 6), jnp.zeros((dm, LANES - 3 * NSA_HEADS - GLA_RANK), F32)], 1)
        w_a = jnp.concatenate([seg(wl, 0), seg(wl, 1), seg(wl, 3), seg(wl, 4), seg(wl, 5), misc], 1).astype(BF16)
        w_b = jnp.concatenate([seg(wl, 7), seg(wl, 8)], axis=1).astype(BF16)
        wa_pad = jnp.zeros((LANES, GLA_HEADS * HEAD_DIM), F32).at[3 * NSA_HEADS:3 * NSA_HEADS + GLA_RANK].set(
            gla_w_a2[l]).astype(BF16)
        qp, qr, kv, win, gq, gk, gv, la, ng = _inproj(h, w_a, cs_tab, sn_tab, wa_pad, gla_b_a[l][None, :],
                                                      seq // ROW_TILE, n_p // ROW_TILE)

        wck, pek, w2k = _compress_weights(cmp_k_pe[l], cmp_k_w1[l], cmp_k_w2[l])
        wcv, pev, w2v = _compress_weights(cmp_v_pe[l], cmp_v_w1[l], cmp_v_w2[l])
        wc = jnp.stack([wck, wcv])
        pe_pair = _pe_bias(jnp.stack([pek, pev]), jnp.stack([cmp_k_w1[l], cmp_v_w1[l]]))
        w2c = jnp.stack([w2k, w2v])

        kc, vc = _compress_prompt(kv, wc, pe_pair, w2c, batch, seq)
        o_nsa_p = _nsa_prompt(qp, qr, ng, kc, vc, kv, win, ovl, batch, seq)
        o_gla_p, st_p = _gla_prompt(gq, gk, gv, la, wlev, batch, seq)
        kv_p.append(kv[:n_p].reshape(batch, seq, 4, NSA_GROUPS, HEAD_DIM))
        wn = min(WINDOW, seq)
        win_p.append(win[:n_p].reshape(batch, seq, 2, NSA_GROUPS, HEAD_DIM)[:, seq - wn:])
        gla_p.append(st_p)

        sl = slice(n_p, n_p + dec_b)
        o8 = _nsa_decode(page_table, cache, l, wc, pe_pair, w2c, _pair_rows(qp[sl], NSA_HEADS),
                         _pair_rows(qr[sl], NSA_HEADS), ng[sl][:, None, :], kv[sl][:, None, :], win[sl][:, None, :],
                         wbuf_arr, ovl, gg, selr)
        o8 = o8.reshape(dec_b, NSA_GROUPS, NSA_HPG, NSA_GROUPS, HEAD_DIM)
        o_nsa_s = jnp.stack([o8[:, g, :, g] for g in range(NSA_GROUPS)], axis=1).reshape(dec_b, qw)
        o_gla_s, st_s = _gla_decode(gq[sl], gk[sl], la[sl], gv[sl], state_gla, l)
        kv_s.append(kv[sl].reshape(dec_b, 1, 4, NSA_GROUPS, HEAD_DIM))
        win_new = win[sl].reshape(dec_b, 1, 2, NSA_GROUPS, HEAD_DIM)
        win_s.append(jnp.concatenate([cache_win_kv[:, l], win_new], axis=1)[:, -wbuf:])
        gla_s.append(st_s)

        pad = jnp.zeros((n_s_pad - dec_b, qw), F32)
        o_nsa = jnp.concatenate([o_nsa_p[:n_p], o_nsa_s, pad], axis=0)
        o_gla = jnp.concatenate([o_gla_p[:n_p], o_gla_s, pad], axis=0)
        h = _outproj(h, o_nsa, o_gla, w_b, gla_norm_g[l][None, :], w_nsa_up[l].astype(BF16),
                     w_gla_up[l].astype(BF16), w_out[l].astype(BF16), ln1_g[l][None, :], ln1_b[l][None, :], alpha)
        h = _mlp(h, mlp_w1[l].astype(BF16), mlp_w2[l].astype(BF16), ln2_g[l][None, :], ln2_b[l][None, :], alpha)

    y_prompt = h[:n_p].reshape(batch, seq, dm)
    y_sample = h[n_p:n_p + dec_b].reshape(dec_b, 1, dm)
    return (y_prompt, y_sample, jnp.stack(kv_p, axis=1), jnp.stack(win_p, axis=1), jnp.stack(gla_p, axis=1),
            jnp.stack(kv_s, axis=1), jnp.stack(win_s, axis=1), jnp.stack(gla_s, axis=1))
```

```python
import functools
import math

import numpy as np
import jax
import jax.numpy as jnp
from jax import lax
from jax.experimental import pallas as pl
from jax.experimental.pallas import tpu as pltpu

F32 = jnp.float32
BF16 = jnp.bfloat16

NSA_HEADS = 8
NSA_GROUPS = 2
NSA_HPG = NSA_HEADS // NSA_GROUPS
HEAD_DIM = 64
SCALE = HEAD_DIM ** -0.5
CMP_LEN = 32
CMP_STRIDE = 16
SEL_LEN = 64
N_SEL = 16
WINDOW = 512
GLA_HEADS = 4
GLA_RANK = 16
GLA_TAU = 16.0
ROPE_THETA = 10000.0
EPS = 1e-5
BIG = 1e6
NEG = -1e30
LOG2E = math.log2(math.e)

LANES = 128
ROW_TILE = 256
Q_TILE = 128
KV_TILE = 256
GLA_CHUNK = 256
GLA_DEC_TILE = 16
NSA_DEC_TILE = 4
VMEM_LIMIT = 56 * 1024 * 1024


def _cparams(n_axes):
    return pltpu.CompilerParams(dimension_semantics=("arbitrary",) * n_axes,
                                vmem_limit_bytes=VMEM_LIMIT)


def _dot(a, b):
    return jnp.dot(a, b, preferred_element_type=F32)


def _dot_nt(a, b):
    return lax.dot_general(a, b, (((1,), (1,)), ((), ())), preferred_element_type=F32)


def _split3(x):
    hi = x.astype(BF16)
    r = x - hi.astype(F32)
    mid = r.astype(BF16)
    lo = (r - mid.astype(F32)).astype(BF16)
    return hi, mid, lo


def _layer_norm(x, g, b):
    mu = jnp.mean(x, axis=-1, keepdims=True)
    xc = x - mu
    var = jnp.mean(xc * xc, axis=-1, keepdims=True)
    return xc * lax.rsqrt(var + EPS) * g + b


def _sigmoid(x):
    return 1.0 / (1.0 + jnp.exp(-x))


def _ln_kernel(x_ref, g_ref, b_ref, o_ref):
    o_ref[...] = _layer_norm(x_ref[...], g_ref[...], b_ref[...])


def _entry_norm(x, g, b):
    n, d = x.shape
    return pl.pallas_call(
        _ln_kernel,
        name="entry_norm",
        grid=(n // ROW_TILE,),
        in_specs=[pl.BlockSpec((ROW_TILE, d), lambda i: (i, 0)),
                  pl.BlockSpec((1, d), lambda i: (0, 0)),
                  pl.BlockSpec((1, d), lambda i: (0, 0))],
        out_specs=pl.BlockSpec((ROW_TILE, d), lambda i: (i, 0)),
        out_shape=jax.ShapeDtypeStruct((n, d), F32),
        compiler_params=_cparams(1),
    )(x, g.reshape(1, d), b.reshape(1, d))


C_Q, C_KV, C_GQ, C_GK, C_GV, C_MISC, C_END = 0, 512, 1280, 1536, 1792, 2304, 2432


def _rope128(x, cs, sn):
    lane = lax.broadcasted_iota(jnp.int32, x.shape, 1)
    first = (lane % HEAD_DIM) < (HEAD_DIM // 2)
    swapped = jnp.where(first, pltpu.roll(x, LANES - HEAD_DIM // 2, 1), pltpu.roll(x, HEAD_DIM // 2, 1))
    return x * cs + swapped * sn


def _inproj_kernel(h_ref, w_ref, cs_ref, sn_ref, wa_ref, ba_ref,
                   qp_ref, qr_ref, kv_ref, win_ref, gq_ref, gk_ref, gv_ref, la_ref, ng_ref):
    hb = h_ref[...].astype(BF16)
    cs = cs_ref[...]
    sn = sn_ref[...]

    def seg(lo, hi):
        return _dot(hb, w_ref[:, lo:hi])

    for j in range(4):
        qj = seg(C_Q + j * LANES, C_Q + (j + 1) * LANES)
        qp_ref[:, j * LANES:(j + 1) * LANES] = qj
        qr_ref[:, j * LANES:(j + 1) * LANES] = _rope128(qj, cs, sn)
    for s in range(6):
        x = seg(C_KV + s * LANES, C_KV + (s + 1) * LANES)
        if s in (2, 4):
            x = _rope128(x, cs, sn)
        if s < 4:
            kv_ref[:, s * LANES:(s + 1) * LANES] = x
        else:
            win_ref[:, (s - 4) * LANES:(s - 3) * LANES] = x
    gq_ref[...] = seg(C_GQ, C_GK) * (HEAD_DIM ** -0.5)
    gk_ref[...] = seg(C_GK, C_GV)
    gv_ref[...] = seg(C_GV, C_MISC)
    misc = seg(C_MISC, C_END)
    ng_ref[...] = misc
    x = _dot(misc.astype(BF16), wa_ref[...]) + ba_ref[...]
    la_ref[...] = (jnp.minimum(x, 0.0) - jnp.log(1.0 + jnp.exp(-jnp.abs(x)))) * (1.0 / GLA_TAU)


def _inproj(h, w_a, cs_tab, sn_tab, wa_pad, ba, tiles_per_seq, n_prompt_tiles):
    n, d = h.shape
    tm = ROW_TILE

    def tab_map(i):
        return (jnp.where(i < n_prompt_tiles, i % tiles_per_seq, tiles_per_seq), 0)

    row = lambda w: pl.BlockSpec((tm, w), lambda i: (i, 0))
    full = lambda a: pl.BlockSpec(a.shape, lambda i: (0,) * a.ndim)
    widths = (512, 512, 512, 256, 256, 256, 512, 256, 128)
    return pl.pallas_call(
        _inproj_kernel,
        name="inproj",
        grid=(n // tm,),
        in_specs=[row(d), full(w_a), pl.BlockSpec((tm, LANES), tab_map), pl.BlockSpec((tm, LANES), tab_map),
                  full(wa_pad), full(ba)],
        out_specs=[row(w) for w in widths],
        out_shape=[jax.ShapeDtypeStruct((n, w), F32) for w in widths],
        compiler_params=_cparams(1),
    )(h, w_a, cs_tab, sn_tab, wa_pad, ba)


def _gelu_tanh(x):
    return 0.5 * x * (1.0 + jnp.tanh(math.sqrt(2.0 / math.pi) * (x + 0.044715 * x * x * x)))


def _compress_rows(chunk_pair, n_chunks, wc_ref, pe_ref, w2_ref, seq_chunks=None):
    seq_chunks = n_chunks if seq_chunks is None else seq_chunks
    outs = []
    for s in range(2):
        acc = [jnp.zeros((n_chunks, 2 * LANES), F32) for _ in range(NSA_GROUPS)]
        for lp in range(CMP_STRIDE // 2):
            a = chunk_pair(s, lp).astype(BF16)
            for g in range(NSA_GROUPS):
                acc[g] = acc[g] + _dot(a, wc_ref[s, g, lp])
        row = lax.broadcasted_iota(jnp.int32, (n_chunks, LANES), 0)
        parts = []
        for g in range(NSA_GROUPS):
            hid = acc[g][:, :LANES] + pltpu.roll(acc[g][:, LANES:], n_chunks - 1, 0) + pe_ref[s]
            parts.append(_dot(_gelu_tanh(hid).astype(BF16), w2_ref[s]))
        out = jnp.concatenate(parts, axis=1)
        outs.append(jnp.where(row % seq_chunks < seq_chunks - 1, out, 0.0))
    return outs


def _pe_bias_kernel(pe_ref, w1_ref, o_ref):
    for s in range(2):
        o_ref[s] = _dot(pe_ref[s].astype(BF16), w1_ref[s].astype(BF16))[0:1]


def _pe_bias(pe_flat, w1):
    hid = w1.shape[2]
    return pl.pallas_call(
        _pe_bias_kernel,
        name="pe_bias",
        out_shape=jax.ShapeDtypeStruct((2, 1, hid), F32),
    )(pe_flat, w1)


def _compress_kernel(xk_ref, xv_ref, wc_ref, pe_ref, w2_ref, kc_ref, vc_ref):
    x_refs = (xk_ref, xv_ref)
    n_chunks = xk_ref.shape[0] // CMP_STRIDE

    def chunk_pair(slot, lp):
        return jnp.concatenate([x_refs[slot][pl.ds(2 * lp, n_chunks, stride=CMP_STRIDE), :],
                                x_refs[slot][pl.ds(2 * lp + 1, n_chunks, stride=CMP_STRIDE), :]], axis=1)

    kc, vc = _compress_rows(chunk_pair, n_chunks, wc_ref, pe_ref, w2_ref)
    kc_ref[0] = kc
    vc_ref[0] = vc


def _compress_prompt(kv, wc, pe_pair, w2, batch, seq):
    nc = seq // CMP_STRIDE
    full = lambda a: pl.BlockSpec(a.shape, lambda b: (0,) * a.ndim)
    return pl.pallas_call(
        _compress_kernel,
        name="compress",
        grid=(batch,),
        in_specs=[pl.BlockSpec((seq, LANES), lambda b: (b, 0)), pl.BlockSpec((seq, LANES), lambda b: (b, 1)),
                  full(wc), full(pe_pair), full(w2)],
        out_specs=[pl.BlockSpec((1, nc, LANES), lambda b: (b, 0, 0))] * 2,
        out_shape=[jax.ShapeDtypeStruct((batch, nc, LANES), F32)] * 2,
        compiler_params=_cparams(1),
    )(kv, kv, wc, pe_pair, w2)


def _select_mask(imp_t, qpos, n_blocks):
    nbp, nq = imp_t.shape
    blk = lax.broadcasted_iota(jnp.int32, (nbp, nq), 0)
    cur = qpos // SEL_LEN
    causal = blk * SEL_LEN <= qpos
    forced = (blk == 0) | (blk == cur) | (blk == cur - 1)
    score = jnp.where(forced, BIG, jnp.where(causal, imp_t, -jnp.inf))
    score = jnp.where(blk < n_blocks, score, -jnp.inf)
    rank = jnp.zeros((nbp, nq), jnp.int32)
    for j in range(n_blocks):
        other = score[j:j + 1, :]
        rank = rank + jnp.where(blk > j, jnp.where(other >= score, 1, 0), jnp.where(other > score, 1, 0))
    return (rank < N_SEL) & (score > -jnp.inf)


def _softmax_step(state, s, pv_fn):
    m, l, acc = state
    m_new = jnp.maximum(m, jnp.max(s, axis=0, keepdims=True))
    alpha = jnp.exp2(m - m_new)
    p = jnp.exp2(s - m_new)
    l = alpha * l + jnp.sum(p, axis=0, keepdims=True)
    acc = alpha * acc + pv_fn(p.astype(BF16))
    return m_new, l, acc


def _softmax_init(nq):
    return (jnp.full((1, nq), NEG, F32), jnp.zeros((1, nq), F32), jnp.zeros((LANES, nq), F32))


def _nsa_prompt_kernel(qpa_ref, qra_ref, nga_ref, qpb_ref, qrb_ref, ngb_ref, kc_ref, vc_ref, ks_ref, vs_ref,
                       kw_ref, vw_ref, ovl_ref, oa_ref, ob_ref, ksa_ref, vst_ref, kwa_ref, vwt_ref,
                       *, seq, n_blocks):
    g = pl.program_id(1)
    i = pl.program_id(2)
    tq = Q_TILE
    nqt = seq // tq
    n_kt = seq // KV_TILE
    nq = NSA_HPG * tq

    @pl.when((g == 0) & (i == 0))
    def _():
        for kt in range(n_kt):
            rows = pl.ds(kt * KV_TILE, KV_TILE)
            key = kt * KV_TILE + lax.broadcasted_iota(jnp.int32, (KV_TILE, LANES), 0)
            lane = lax.broadcasted_iota(jnp.int32, (KV_TILE, LANES), 1)
            onehot = jnp.where(key // SEL_LEN == lane, 1.0, 0.0).astype(BF16)
            ksa_ref[kt] = jnp.concatenate([ks_ref[rows, :].astype(BF16), onehot], axis=1)
            vst_ref[kt] = vs_ref[rows, :].T.astype(BF16)
            kwa_ref[kt] = kw_ref[rows, :].astype(BF16)
            vwt_ref[kt] = vw_ref[rows, :].T.astype(BF16)

    lane_q = lax.broadcasted_iota(jnp.int32, (1, nq), 1) % tq
    lane1 = lax.broadcasted_iota(jnp.int32, (1, tq), 1)
    lane = lax.broadcasted_iota(jnp.int32, (tq, LANES), 1)
    own = (lane // HEAD_DIM) == g
    krow = lax.broadcasted_iota(jnp.int32, (KV_TILE, nq), 0)
    ovl = ovl_ref[...]
    kc = kc_ref[0].astype(BF16)
    vct = vc_ref[0].T.astype(BF16)
    nc = kc.shape[0]
    nbp = -(-n_blocks // 8) * 8

    def own_rows(x):
        return jnp.where(g == 0, x[:HEAD_DIM], x[HEAD_DIM:])

    def prepare(qp_ref, qr_ref, qs):
        def stack_heads(ref, scale):
            parts = []
            for h in range(NSA_HPG):
                qh = ref[:, h * HEAD_DIM:(h + 1) * HEAD_DIM] * scale
                parts.append(jnp.where(own, jnp.concatenate([qh, qh], axis=1), 0.0))
            return parts

        qpos = qs + lane_q
        qp = jnp.concatenate(stack_heads(qp_ref, SCALE), axis=0).astype(BF16)
        s = _dot_nt(kc, qp)
        cblk = lax.broadcasted_iota(jnp.int32, (nc, nq), 0)
        valid = cblk * CMP_STRIDE + (CMP_LEN - 1) <= qpos
        s = jnp.where(valid, s, NEG)
        m = jnp.max(s, axis=0, keepdims=True)
        p = jnp.where(valid, jnp.exp(s - m), 0.0)
        p = p / jnp.maximum(jnp.sum(p, axis=0, keepdims=True), 1e-30)
        o_cmp = own_rows(_dot(vct, p.astype(BF16)))
        psum = p[:, 0:tq]
        for h in range(1, NSA_HPG):
            psum = psum + p[:, h * tq:(h + 1) * tq]
        imp_t = sum(_dot(ovl, piece) for piece in _split3(psum))
        sel = _select_mask(imp_t[:nbp], qs + lane1, n_blocks)
        bias_t = jnp.where(sel, 0.0, NEG)
        bias = jnp.concatenate([bias_t, jnp.zeros((LANES - nbp, tq), F32)], axis=0).T
        qr_parts = stack_heads(qr_ref, SCALE * LOG2E)
        q_sel = jnp.concatenate([jnp.concatenate([q, bias], axis=1) for q in qr_parts], axis=0).astype(BF16)
        q_win = jnp.concatenate(qr_parts, axis=0).astype(BF16)
        return qpos, o_cmp, q_sel, q_win

    qs_a = i * tq
    qs_b = (nqt - 1 - i) * tq
    qpos_a, o_cmp_a, q_sel_a, q_win_a = prepare(qpa_ref, qra_ref, qs_a)
    qpos_b, o_cmp_b, q_sel_b, q_win_b = prepare(qpb_ref, qrb_ref, qs_b)

    kd_a = qs_a // KV_TILE
    kd_b = qs_b // KV_TILE
    max_kd_a = ((nqt // 2 - 1) * tq) // KV_TILE
    n_sel = n_kt + 1
    n_win = WINDOW // KV_TILE + 1

    def sel_task(j):
        if j == 0:
            return 0, q_sel_a, qpos_a, True
        if j > max_kd_a:
            return j - kd_a - 1, q_sel_b, qpos_b, j == n_sel - 1
        in_a = j <= kd_a
        return (jnp.where(in_a, j, j - kd_a - 1), jnp.where(in_a, q_sel_a, q_sel_b),
                jnp.where(in_a, qpos_a, qpos_b), True)

    tasks = []
    for j in range(n_sel):
        kt, q, qpos, masked = sel_task(j)
        valid = (kt * KV_TILE + krow <= qpos) if masked else None
        tasks.append(("sel", j, ksa_ref, vst_ref, kt, q, valid))
    for name, kd, q, qpos in (("wa", kd_a, q_win_a, qpos_a), ("wb", kd_b, q_win_b, qpos_b)):
        for j in range(n_win):
            kt = kd - (n_win - 1) + j
            d = qpos - (kt * KV_TILE + krow)
            valid = (d >= 0) & (d <= WINDOW) & (kt >= 0)
            tasks.append((name, j, kwa_ref, vwt_ref, jnp.maximum(kt, 0), q, valid))

    def scores(task):
        _, _, k_ref, _, kt, q, _ = task
        return _dot_nt(k_ref[kt], q)

    results = {}
    state = None
    s_next = scores(tasks[0])
    for t, task in enumerate(tasks):
        name, j, _, vt_ref, kt, _, valid = task
        s = s_next
        if t + 1 < len(tasks):
            s_next = scores(tasks[t + 1])
        if j == 0:
            state = _softmax_init(nq)
        if name == "sel" and 1 <= j <= max_kd_a + 1:
            switch = j == kd_a + 1
            prev = results.get("sa", state)
            results["sa"] = tuple(jnp.where(switch, x, y) for x, y in zip(state, prev))
            state = tuple(jnp.where(switch, x, y) for x, y in zip(_softmax_init(nq), state))
        if valid is not None:
            s = jnp.where(valid, s, NEG)
        state = _softmax_step(state, s, lambda p: _dot(vt_ref[kt], p))
        last = (name == "sel" and j == n_sel - 1) or (name != "sel" and j == n_win - 1)
        if last:
            results["sb" if name == "sel" else name] = state

    def finish(key):
        _, l, acc = results[key]
        return own_rows(acc / l)

    for ng_ref, o_ref, branches in ((nga_ref, oa_ref, (o_cmp_a, finish("sa"), finish("wa"))),
                                    (ngb_ref, ob_ref, (o_cmp_b, finish("sb"), finish("wb")))):
        ng_t = ng_ref[...].T
        outs = []
        for h in range(NSA_HPG):
            cols = slice(h * tq, (h + 1) * tq)
            tot = jnp.zeros((HEAD_DIM, tq), F32)
            for r, o in enumerate(branches):
                i0 = h * 3 + r
                i1 = (NSA_HPG + h) * 3 + r
                gate = _sigmoid(jnp.where(g == 0, ng_t[i0:i0 + 1], ng_t[i1:i1 + 1]))
                tot = tot + gate * o[:, cols]
            outs.append(tot.T)
        o_ref[...] = jnp.concatenate(outs, axis=1)


def _nsa_prompt(qp, qr, ng, kc, vc, kv, win, ovl, batch, seq):
    n_blocks = -(-seq // SEL_LEN)
    nqt = seq // Q_TILE
    n_kt = seq // KV_TILE
    assert nqt % 2 == 0 and KV_TILE == 2 * Q_TILE and WINDOW % KV_TILE == 0
    gw = NSA_HPG * HEAD_DIM
    half = nqt // 2
    qa = pl.BlockSpec((Q_TILE, gw), lambda b, g, i: (b * nqt + i, g))
    qb = pl.BlockSpec((Q_TILE, gw), lambda b, g, i: (b * nqt + nqt - 1 - i, g))
    na = pl.BlockSpec((Q_TILE, LANES), lambda b, g, i: (b * nqt + i, 0))
    nb = pl.BlockSpec((Q_TILE, LANES), lambda b, g, i: (b * nqt + nqt - 1 - i, 0))
    cspec = pl.BlockSpec((1,) + kc.shape[1:], lambda b, g, i: (b, 0, 0))
    ospec = pl.BlockSpec((Q_TILE, gw), lambda b, g, i: (b * half + i, g))

    def kvspec(col):
        return pl.BlockSpec((seq, LANES), lambda b, g, i: (b, col))

    oshape = jax.ShapeDtypeStruct((batch * half * Q_TILE, qp.shape[1]), F32)
    return pl.pallas_call(
        functools.partial(_nsa_prompt_kernel, seq=seq, n_blocks=n_blocks),
        name="nsa_prompt",
        grid=(batch, NSA_GROUPS, half),
        in_specs=[qa, qa, na, qb, qb, nb, cspec, cspec, kvspec(2), kvspec(3), kvspec(0), kvspec(1),
                  pl.BlockSpec(ovl.shape, lambda b, g, i: (0, 0))],
        out_specs=[ospec, ospec],
        out_shape=[oshape, oshape],
        scratch_shapes=[pltpu.VMEM((n_kt, KV_TILE, 2 * LANES), BF16), pltpu.VMEM((n_kt, LANES, KV_TILE), BF16),
                        pltpu.VMEM((n_kt, KV_TILE, LANES), BF16), pltpu.VMEM((n_kt, LANES, KV_TILE), BF16)],
        compiler_params=_cparams(3),
    )(qp, qr, ng, qp, qr, ng, kc, vc, kv, kv, win, win, ovl)


def _gla_level_matrix(c):
    t = np.arange(c)[:, None]
    u = np.arange(c)[None, :]
    mats = [(u <= t), (u > t)]
    m = c
    while m >= 2:
        split = (t // m) * m + m // 2
        upper = (t % m) >= m // 2
        mats.append(np.where(upper, (u >= split) & (u <= t), (u > t) & (u < split)))
        m //= 2
    return np.concatenate(mats, axis=0).astype(np.float32)


def _gla_prompt_kernel(q_ref, k_ref, v_ref, la_ref, w_ref, o_ref, st_ref, e_ref, s_ref, *, seq):
    c = GLA_CHUNK
    n_levels = int(math.log2(c))
    s_ref[...] = jnp.zeros_like(s_ref)
    row = lax.broadcasted_iota(jnp.int32, (c, c), 0)
    col = lax.broadcasted_iota(jnp.int32, (c, c), 1)
    rowl = lax.broadcasted_iota(jnp.int32, (c, LANES), 0)
    lanel = lax.broadcasted_iota(jnp.int32, (c, LANES), 1)

    def chunk(ci, _):
        rows = pl.ds(pl.multiple_of(ci * c, c), c)
        la = la_ref[rows, :]
        hi = la.astype(BF16)
        lo = (la - hi.astype(F32)).astype(BF16)
        e_ref[...] = jnp.exp(_dot(w_ref[...], hi) + _dot(w_ref[...], lo))
        for pair in range(GLA_HEADS // 2):
            lanes = slice(pair * LANES, (pair + 1) * LANES)
            q = q_ref[rows, lanes]
            k = k_ref[rows, lanes]
            q0 = (q * e_ref[0:c, lanes]).astype(BF16)
            kdec = k * e_ref[c:2 * c, lanes]
            a_last = e_ref[c - 1:c, lanes]
            qls, kls = [], []
            for lv in range(n_levels):
                m = c >> lv
                x = e_ref[(2 + lv) * c:(3 + lv) * c, lanes]
                upper = (rowl % m) >= (m // 2)
                qls.append(jnp.where(upper, q * x, 0.0).astype(BF16))
                kls.append(jnp.where(upper, 0.0, k * x))
            for hh in range(2):
                head = pair * 2 + hh
                mine = (lanel // HEAD_DIM) == hh
                a = jnp.where(row == col, _dot_nt(q.astype(BF16), jnp.where(mine, k, 0.0).astype(BF16)), 0.0)
                for lv in range(n_levels):
                    m = c >> lv
                    same = (row // m) == (col // m)
                    a = a + jnp.where(same, _dot_nt(qls[lv], jnp.where(mine, kls[lv], 0.0).astype(BF16)), 0.0)
                v = v_ref[rows, head * LANES:(head + 1) * LANES]
                st = s_ref[head]
                o = _dot(a.astype(BF16), v.astype(BF16)) + _dot_nt(q0, st.astype(BF16))
                o_ref[rows, head * LANES:(head + 1) * LANES] = o
                kd = jnp.where(mine, kdec, 0.0).astype(BF16)
                s_ref[head] = st * a_last + _dot(v.T.astype(BF16), kd)
        return 0

    lax.fori_loop(0, seq // c, chunk, 0)
    for head in range(GLA_HEADS):
        st = s_ref[head].T
        off = (head % 2) * HEAD_DIM
        st_ref[0, head] = st[off:off + HEAD_DIM]


def _gla_prompt(gq, gk, gv, la, wlev, batch, seq):
    dk2 = gq.shape[1]
    dv4 = gv.shape[1]
    return pl.pallas_call(
        functools.partial(_gla_prompt_kernel, seq=seq),
        name="gla_prompt",
        grid=(batch,),
        in_specs=[pl.BlockSpec((seq, dk2), lambda b: (b, 0)), pl.BlockSpec((seq, dk2), lambda b: (b, 0)),
                  pl.BlockSpec((seq, dv4), lambda b: (b, 0)), pl.BlockSpec((seq, dk2), lambda b: (b, 0)),
                  pl.BlockSpec(wlev.shape, lambda b: (0, 0))],
        out_specs=[pl.BlockSpec((seq, dv4), lambda b: (b, 0)),
                   pl.BlockSpec((1, GLA_HEADS, HEAD_DIM, LANES), lambda b: (b, 0, 0, 0))],
        out_shape=[jax.ShapeDtypeStruct((gq.shape[0], dv4), F32),
                   jax.ShapeDtypeStruct((batch, GLA_HEADS, HEAD_DIM, LANES), F32)],
        scratch_shapes=[pltpu.VMEM((wlev.shape[0], dk2), F32), pltpu.VMEM((GLA_HEADS, LANES, LANES), F32)],
        compiler_params=_cparams(1),
    )(gq, gk, gv, la, wlev)


def _outproj_kernel(h_ref, on_ref, og_ref, wb_ref, gn_ref, wn_ref, wg_ref, wo_ref, g1_ref, b1_ref, o_ref,
                    *, alpha):
    h = h_ref[...]
    hb = h.astype(BF16)
    dm = h.shape[1]
    gw = og_ref.shape[1]
    g_r = _dot(hb, wb_ref[:, :gw])
    parts = []
    for head in range(GLA_HEADS):
        x = og_ref[:, head * LANES:(head + 1) * LANES]
        x = x * lax.rsqrt(jnp.mean(x * x, axis=-1, keepdims=True) + EPS) * gn_ref[...]
        gr = g_r[:, head * LANES:(head + 1) * LANES]
        parts.append(x * (gr * _sigmoid(gr)))
    og = jnp.concatenate(parts, axis=1).astype(BF16)
    a = _dot(on_ref[...].astype(BF16), wn_ref[...])
    c = _dot(og, wg_ref[...])
    m_a = _dot(hb, wb_ref[:, gw:gw + dm])
    m_c = _dot(hb, wb_ref[:, gw + dm:gw + 2 * dm])
    mix = (_sigmoid(m_a) * a + _sigmoid(m_c) * c).astype(BF16)
    y = _dot(mix, wo_ref[...])
    o_ref[...] = _layer_norm(alpha * h + y, g1_ref[...], b1_ref[...])


def _outproj(h, o_nsa, o_gla, w_b, gn, w_nsa, w_gla, w_out, g1, b1, alpha):
    n, d = h.shape
    tm = ROW_TILE
    row = lambda w: pl.BlockSpec((tm, w), lambda i: (i, 0))
    full = lambda a: pl.BlockSpec(a.shape, lambda i: (0,) * a.ndim)
    return pl.pallas_call(
        functools.partial(_outproj_kernel, alpha=alpha),
        name="outproj",
        grid=(n // tm,),
        in_specs=[row(d), row(o_nsa.shape[1]), row(o_gla.shape[1]), full(w_b), full(gn), full(w_nsa),
                  full(w_gla), full(w_out), full(g1), full(b1)],
        out_specs=row(d),
        out_shape=jax.ShapeDtypeStruct((n, d), F32),
        compiler_params=_cparams(1),
    )(h, o_nsa, o_gla, w_b, gn, w_nsa, w_gla, w_out, g1, b1)


def _mlp_kernel(h_ref, w1_ref, w2_ref, g_ref, b_ref, o_ref, *, alpha):
    h = h_ref[...]
    hb = h.astype(BF16)
    dff = w1_ref.shape[1]
    step = 1024
    f = jnp.zeros(h.shape, F32)
    for c0 in range(0, dff, step):
        u = jnp.maximum(_dot(hb, w1_ref[:, c0:c0 + step]), 0.0)
        f = f + _dot((u * u).astype(BF16), w2_ref[c0:c0 + step, :])
    o_ref[...] = _layer_norm(alpha * h + f, g_ref[...], b_ref[...])


def _mlp(h, w1, w2, g, b, alpha):
    n, d = h.shape
    tm = ROW_TILE
    row = pl.BlockSpec((tm, d), lambda i: (i, 0))
    full = lambda a: pl.BlockSpec(a.shape, lambda i: (0,) * a.ndim)
    return pl.pallas_call(
        functools.partial(_mlp_kernel, alpha=alpha),
        name="mlp",
        grid=(n // tm,),
        in_specs=[row, full(w1), full(w2), full(g), full(b)],
        out_specs=row,
        out_shape=jax.ShapeDtypeStruct((n, d), F32),
        compiler_params=_cparams(1),
    )(h, w1, w2, g, b)


def _softmax_rows(parts, extra, valid_extra=None):
    m = extra
    for s in parts:
        m = jnp.maximum(m, jnp.max(s, axis=-1, keepdims=True))
    ps = [jnp.exp(s - m) for s in parts]
    pe = jnp.exp(extra - m)
    tot = pe
    for p in ps:
        tot = tot + jnp.sum(p, axis=-1, keepdims=True)
    inv = 1.0 / tot
    return [p * inv for p in ps], pe * inv


def _nsa_decode_kernel(pt_ref, *refs, n_pages, n_blocks, past, has_prev):
    del pt_ref
    nb = NSA_DEC_TILE
    cmp_all = [refs[bb * n_pages:(bb + 1) * n_pages] for bb in range(nb)]
    sel_all = [refs[(nb + bb) * n_pages:(nb + bb + 1) * n_pages] for bb in range(nb)]
    rest = refs[2 * nb * n_pages:]
    (wd_ref, pe_ref, w2_ref, qp_ref, qr_ref, ng_ref, kvn_ref, winn_ref, wb_ref, ovl_ref, gg_ref, selr_ref,
     exp_ref) = rest[:13]
    o_ref, wo_ref = rest[-2:]
    page = sel_all[0][0].shape[-1]
    wbuf = wb_ref.shape[-1]
    n_rows = qp_ref.shape[1]
    seq_chunks = past // CMP_STRIDE
    n_chunks = nb * seq_chunks
    row = lax.broadcasted_iota(jnp.int32, (n_chunks, LANES), 0)
    outs = []
    for slot in range(2):
        parts = []
        for grp in range(NSA_GROUPS):
            acc = jnp.zeros((n_chunks, 2 * LANES), F32)
            for quad in range(CMP_STRIDE // 4):
                a = jnp.concatenate([pg[slot, grp, quad] for pages in cmp_all for pg in pages], axis=0)
                acc = acc + _dot(a.astype(BF16), wd_ref[slot, quad])
            hid = acc[:, :LANES] + pltpu.roll(acc[:, LANES:], n_chunks - 1, 0) + pe_ref[slot]
            parts.append(_dot(_gelu_tanh(hid).astype(BF16), w2_ref[slot]))
        out = jnp.concatenate(parts, axis=1)
        outs.append(jnp.where(row % seq_chunks < seq_chunks - 1, out, 0.0))
    kc_all, vc_all = outs
    streams = [_nsa_decode_one(bb, kc_all[bb * seq_chunks:(bb + 1) * seq_chunks],
                               vc_all[bb * seq_chunks:(bb + 1) * seq_chunks], sel_all[bb], qp_ref, qr_ref, ng_ref,
                               kvn_ref, winn_ref, wb_ref, ovl_ref, gg_ref, selr_ref, exp_ref, o_ref, wo_ref,
                               n_blocks, past) for bb in range(nb)]
    while streams:
        streams = [st for st in streams if next(st, "done") is None]


def _nsa_decode_one(bb, kc, vc, sel_pages, qp_ref, qr_ref, ng_ref, kvn_ref, winn_ref, wb_ref, ovl_ref, gg_ref,
                    selr_ref, exp_ref, o_ref, wo_ref, n_blocks, past):
    page = sel_pages[0].shape[-1]
    wbuf = wb_ref.shape[-1]
    n_rows = qp_ref.shape[1]
    row8 = lax.broadcasted_iota(jnp.int32, (n_rows, LANES), 0)
    lane8 = lax.broadcasted_iota(jnp.int32, (n_rows, LANES), 1)
    own = (lane8 // HEAD_DIM) == (row8 // NSA_HPG)

    def by_group(x0, x1):
        return jnp.where(lax.broadcasted_iota(jnp.int32, x0.shape, 0) < NSA_HPG, x0, x1)

    qp = qp_ref[bb] * SCALE
    qr = qr_ref[bb] * SCALE
    qrb = qr.astype(BF16)
    kvn = kvn_ref[bb]
    winn = winn_ref[bb]

    def new_key_scores(krow):
        prod = jnp.concatenate([qr, qr], axis=1) * krow
        return jnp.sum(jnp.where(own, prod, 0.0), axis=-1, keepdims=True)

    def new_value(vrow):
        v = jnp.broadcast_to(vrow, (n_rows, LANES))
        return by_group(v[:, :HEAD_DIM], v[:, HEAD_DIM:])

    nc = kc.shape[0]
    qp_pair = jnp.where(own, jnp.concatenate([qp, qp], axis=1), 0.0)
    s_cmp = _dot_nt(qp_pair.astype(BF16), kc.astype(BF16))
    s_sel = [by_group(_dot(qrb, pg[0, 0].astype(BF16)), _dot(qrb, pg[0, 1].astype(BF16))) for pg in sel_pages]
    s_win = by_group(_dot(qrb, wb_ref[bb, 0, 0].astype(BF16)), _dot(qrb, wb_ref[bb, 0, 1].astype(BF16)))
    pieces = _split3(jnp.broadcast_to(_sigmoid(ng_ref[bb]), (LANES, LANES)))
    gates = [sum(_dot_nt(selr_ref[r], piece) for piece in pieces)[:, :HEAD_DIM] for r in range(3)]
    yield

    cblk = lax.broadcasted_iota(jnp.int32, (n_rows, nc), 1)
    valid = cblk * CMP_STRIDE + (CMP_LEN - 1) <= past
    s = jnp.where(valid, s_cmp, NEG)
    p = jnp.where(valid, jnp.exp(s - jnp.max(s, axis=-1, keepdims=True)), 0.0)
    p = p / jnp.maximum(jnp.sum(p, axis=-1, keepdims=True), 1e-30)
    o = _dot(p.astype(BF16), vc.astype(BF16))
    imp_h = sum(_dot_nt(piece, ovl_ref[...]) for piece in _split3(p))
    ps, p_new_w = _softmax_rows([s_win], new_key_scores(winn[:, 0:LANES]))
    pb = ps[0].astype(BF16)
    o_win = by_group(_dot_nt(pb, wb_ref[bb, 1, 0].astype(BF16)), _dot_nt(pb, wb_ref[bb, 1, 1].astype(BF16)))
    yield
    o_cmp = by_group(o[:, :HEAD_DIM], o[:, HEAD_DIM:])
    o_win = o_win + p_new_w * new_value(winn[:, LANES:2 * LANES])
    imp_h = jnp.concatenate([imp_h, jnp.zeros((LANES - n_rows, LANES), F32)], axis=0)
    imp = sum(_dot(gg_ref[...], piece) for piece in _split3(imp_h))
    yield

    nbp = -(-n_blocks // 8) * 8
    sel = _select_mask(imp.T[:nbp], jnp.full((1, LANES), past, jnp.int32), n_blocks)
    bias_t = jnp.where(sel, 0.0, NEG)
    bias_t = jnp.concatenate([bias_t, jnp.zeros((LANES - nbp, LANES), F32)], axis=0)
    bias = bias_t.T[:n_rows]
    bias_keys = _dot(bias.astype(BF16), exp_ref[...])
    yield

    parts = [s + bias_keys[:, pi * page:(pi + 1) * page] for pi, s in enumerate(s_sel)]
    blk_new = past // SEL_LEN
    s_new = new_key_scores(kvn[:, 2 * LANES:3 * LANES]) + bias[:, blk_new:blk_new + 1]
    ps, p_new = _softmax_rows(parts, s_new)
    acc0 = jnp.zeros((n_rows, HEAD_DIM), F32)
    acc1 = jnp.zeros((n_rows, HEAD_DIM), F32)
    for pg, p in zip(sel_pages, ps):
        pb = p.astype(BF16)
        acc0 = acc0 + _dot_nt(pb, pg[1, 0].astype(BF16))
        acc1 = acc1 + _dot_nt(pb, pg[1, 1].astype(BF16))
    yield
    o_sel = by_group(acc0, acc1) + p_new * new_value(kvn[:, 3 * LANES:4 * LANES])
    o_ref[bb] = gates[0] * o_cmp + gates[1] * o_sel + gates[2] * o_win

    lane_w = lax.broadcasted_iota(jnp.int32, (HEAD_DIM, LANES), 1)
    n_col = wbuf // LANES
    for kv in range(2):
        tile = jnp.concatenate([winn[:, kv * LANES:(kv + 1) * LANES], jnp.zeros((LANES - 1, LANES), F32)], axis=0)
        new_t = tile.T
        for grp in range(NSA_GROUPS):
            col = new_t[grp * HEAD_DIM:(grp + 1) * HEAD_DIM, 0:1]
            rolled = [pltpu.roll(wb_ref[bb, kv, grp, :, c * LANES:(c + 1) * LANES], LANES - 1, 1)
                      for c in range(n_col)]
            for c in range(n_col):
                nxt = rolled[c + 1] if c + 1 < n_col else jnp.broadcast_to(col, (HEAD_DIM, LANES))
                wo_ref[bb, kv, grp, :, c * LANES:(c + 1) * LANES] = jnp.where(lane_w == LANES - 1, nxt, rolled[c])


def _nsa_decode(page_table, cache_cmp, cache_t, layer, wd, pe_pair, w2, qp8, qr8, ng3, kvn3, winn3, win_t, ovl, gg,
                selr, expand, win_prev):
    dec_b, n_pages = page_table.shape
    page = cache_t.shape[-1]
    past = n_pages * page
    wbuf = win_t.shape[-1]
    depth = win_t.shape[1]
    assert wbuf <= WINDOW and wbuf % LANES == 0
    n_blocks = -(-(past + 1) // SEL_LEN)
    nb = NSA_DEC_TILE
    assert dec_b % nb == 0
    full = lambda a: pl.BlockSpec(a.shape, lambda b, pt: (0,) * a.ndim)
    per_b = lambda a: pl.BlockSpec((nb,) + a.shape[1:], lambda b, pt: (b,) + (0,) * (a.ndim - 1))
    wshape = (nb, None, 2, NSA_GROUPS, HEAD_DIM, wbuf)

    def cmp_spec(bb, p):
        return pl.BlockSpec((None, None) + cache_cmp.shape[2:],
                            lambda b, pt: (pt[b * nb + bb, p], layer, 0, 0, 0, 0, 0))

    def sel_spec(bb, p):
        return pl.BlockSpec((None, None, 2, NSA_GROUPS, HEAD_DIM, page),
                            lambda b, pt: (pt[b * nb + bb, p], layer, 1, 0, 0, 0))

    in_specs = [cmp_spec(bb, p) for bb in range(nb) for p in range(n_pages)] + [
        sel_spec(bb, p) for bb in range(nb) for p in range(n_pages)] + [
        full(wd), full(pe_pair), full(w2), per_b(qp8), per_b(qr8), per_b(ng3), per_b(kvn3), per_b(winn3),
        pl.BlockSpec(wshape, lambda b, pt: (b, layer, 0, 0, 0, 0)), full(ovl), full(gg), full(selr), full(expand)]
    args = [page_table] + [cache_cmp] * (nb * n_pages) + [cache_t] * (nb * n_pages) + [
        wd, pe_pair, w2, qp8, qr8, ng3, kvn3, winn3, win_t, ovl, gg, selr, expand]
    aliases = {}
    if win_prev is not None:
        in_specs.append(pl.BlockSpec(memory_space=pl.ANY))
        aliases = {len(args): 1}
        args.append(win_prev)
    grid_spec = pltpu.PrefetchScalarGridSpec(
        num_scalar_prefetch=1,
        grid=(dec_b // nb,),
        in_specs=in_specs,
        out_specs=[per_b(qp8), pl.BlockSpec(wshape, lambda b, pt: (b, layer, 0, 0, 0, 0))])
    return pl.pallas_call(
        functools.partial(_nsa_decode_kernel, n_pages=n_pages, n_blocks=n_blocks, past=past,
                          has_prev=win_prev is not None),
        name="nsa_decode",
        grid_spec=grid_spec,
        out_shape=[jax.ShapeDtypeStruct(qp8.shape, F32),
                   jax.ShapeDtypeStruct((dec_b, depth, 2, NSA_GROUPS, HEAD_DIM, wbuf), F32)],
        input_output_aliases=aliases,
        compiler_params=_cparams(1),
    )(*args)


def _gla_decode_kernel(q_ref, k_ref, la_ref, v_ref, s_ref, o_ref, so_ref, qt_ref, kt_ref, at_ref):
    i = pl.program_id(0)
    bt = GLA_DEC_TILE
    n_tiles = qt_ref.shape[0]

    @pl.when(i == 0)
    def _():
        qt = q_ref[...].T
        kt = k_ref[...].T
        at = jnp.exp(la_ref[...]).T
        for j in range(n_tiles):
            qt_ref[j] = qt[:, j * bt:(j + 1) * bt]
            kt_ref[j] = kt[:, j * bt:(j + 1) * bt]
            at_ref[j] = at[:, j * bt:(j + 1) * bt]

    qt = qt_ref[i]
    kt = kt_ref[i]
    at = at_ref[i]
    for bb in range(bt):
        for head in range(GLA_HEADS):
            rows = slice(head * HEAD_DIM, (head + 1) * HEAD_DIM)
            v = v_ref[bb:bb + 1, head * LANES:(head + 1) * LANES]
            st = at[rows, bb:bb + 1] * s_ref[bb, 0, head] + kt[rows, bb:bb + 1] * v
            so_ref[bb, head] = st
            o_ref[bb:bb + 1, head * LANES:(head + 1) * LANES] = jnp.sum(qt[rows, bb:bb + 1] * st, axis=0,
                                                                        keepdims=True)


def _gla_decode(gq_s, gk_s, la_s, gv_s, state, layer):
    dec_b, dk4 = gq_s.shape
    bt = GLA_DEC_TILE
    n_tiles = dec_b // bt
    full = lambda a: pl.BlockSpec(a.shape, lambda i: (0,) * a.ndim)
    sblk = (bt, None, GLA_HEADS, HEAD_DIM, LANES)
    return pl.pallas_call(
        _gla_decode_kernel,
        name="gla_decode",
        grid=(n_tiles,),
        in_specs=[full(gq_s), full(gk_s), full(la_s), pl.BlockSpec((bt, gv_s.shape[1]), lambda i: (i, 0)),
                  pl.BlockSpec((bt, 1, GLA_HEADS, HEAD_DIM, LANES), lambda i: (i, layer, 0, 0, 0))],
        out_specs=[pl.BlockSpec((bt, gv_s.shape[1]), lambda i: (i, 0)),
                   pl.BlockSpec((bt, GLA_HEADS, HEAD_DIM, LANES), lambda i: (i, 0, 0, 0))],
        out_shape=[jax.ShapeDtypeStruct(gv_s.shape, F32),
                   jax.ShapeDtypeStruct((dec_b, GLA_HEADS, HEAD_DIM, LANES), F32)],
        scratch_shapes=[pltpu.VMEM((n_tiles, dk4, bt), F32)] * 3,
        compiler_params=_cparams(1),
    )(gq_s, gk_s, la_s, gv_s, state)


def _overlap_t():
    r, w = SEL_LEN // CMP_STRIDE, CMP_LEN // CMP_STRIDE
    off = (np.arange(r)[:, None] + np.arange(w)[None, :]).reshape(-1)
    j = np.arange(LANES)
    c = np.arange(LANES)
    ov = np.sum(c[None, :, None] == (r * j[:, None, None] + off[None, None, :]), axis=-1)
    return ov.astype(np.float32)


def _rope_tables(seq, past, n_tab_rows):
    half = HEAD_DIM // 2
    inv = ROPE_THETA ** (-jnp.arange(half, dtype=F32) / half)
    pos = jnp.concatenate([jnp.arange(seq, dtype=F32), jnp.full((n_tab_rows - seq,), past, F32)])
    ang = pos[:, None] * inv[None, :]
    cos, sin = jnp.cos(ang), jnp.sin(ang)
    cs = jnp.concatenate([cos, cos, cos, cos], axis=1)
    sn = jnp.concatenate([-sin, sin, -sin, sin], axis=1)
    return cs, sn


def _compress_weights(pe, w1, w2):
    dh = HEAD_DIM
    w1r = w1.reshape(CMP_LEN, dh, -1)
    hid = w1r.shape[-1]
    z = jnp.zeros((dh, hid), w1.dtype)
    groups = []
    for g in range(NSA_GROUPS):
        mats = []
        for lp in range(CMP_STRIDE // 2):
            halves = []
            for base in (0, CMP_STRIDE):
                blocks = []
                for l in (2 * lp, 2 * lp + 1):
                    blocks += [w1r[base + l], z] if g == 0 else [z, w1r[base + l]]
                halves.append(jnp.concatenate(blocks, axis=0))
            mats.append(jnp.concatenate(halves, axis=1))
        groups.append(jnp.stack(mats))
    wc = jnp.stack(groups).astype(BF16)
    pe_flat = jnp.broadcast_to(pe.reshape(1, -1), (8, pe.size))
    return wc, pe_flat, w2.astype(BF16)


def kernel(x_prompt, x_sample, cache_nsa_kv, cache_win_kv, state_gla, page_table, ln_in_g, ln_in_b, w_in, cmp_k_pe, cmp_k_w1, cmp_k_w2, cmp_v_pe, cmp_v_w1, cmp_v_w2, gla_w_a2, gla_b_a, gla_norm_g, w_nsa_up, w_gla_up, w_out, ln1_g, ln1_b, mlp_w1, mlp_w2, ln2_g, ln2_b):
    batch, seq, dm = x_prompt.shape
    dec_b = x_sample.shape[0]
    depth = w_in.shape[0]
    n_phys, _, page = cache_nsa_kv.shape[:3]
    past = page_table.shape[1] * page
    wbuf = cache_win_kv.shape[2]
    alpha = (2.0 * depth) ** 0.25
    n_p = batch * seq
    n_s_pad = -(-dec_b // ROW_TILE) * ROW_TILE
    n = n_p + n_s_pad
    qw = NSA_HEADS * HEAD_DIM

    x = jnp.concatenate([x_prompt.reshape(n_p, dm), x_sample.reshape(dec_b, dm),
                         jnp.zeros((n_s_pad - dec_b, dm), F32)], axis=0)
    h = _entry_norm(x, ln_in_g, ln_in_b)

    cs_tab, sn_tab = _rope_tables(seq, past, seq + ROW_TILE)
    ovl = jnp.asarray(_overlap_t())
    wlev = jnp.asarray(_gla_level_matrix(GLA_CHUNK)).astype(BF16)
    col = np.arange(LANES)
    gg = jnp.asarray(((col[:, None] // NSA_HPG == col[None, :] // NSA_HPG)
                      & (col[:, None] < NSA_HEADS) & (col[None, :] < NSA_HEADS)).astype(np.float32)).astype(BF16)
    selr = jnp.asarray(np.stack([col[None, :] == np.arange(NSA_HEADS)[:, None] * 3 + r
                                 for r in range(3)]).astype(np.float32)).astype(BF16)
    cpp = page // CMP_STRIDE
    cache_cmp = cache_nsa_kv[:, :, :, 0:2].reshape(n_phys, depth, cpp, CMP_STRIDE // 4, 4, 2, NSA_GROUPS, HEAD_DIM)
    cache_cmp = jnp.transpose(cache_cmp, (0, 1, 5, 6, 3, 2, 4, 7)).reshape(n_phys, depth, 2, NSA_GROUPS,
                                                                            CMP_STRIDE // 4, cpp, 4 * HEAD_DIM)
    cache_t = jnp.transpose(cache_nsa_kv, (0, 1, 3, 4, 5, 2))
    win_t = jnp.transpose(cache_win_kv, (0, 1, 3, 4, 5, 2))
    expand = jnp.asarray((np.arange(past)[None, :] // SEL_LEN == col[:, None]).astype(np.float32)).astype(BF16)
    win_buf = None

    sizes = (qw, 6 * NSA_GROUPS * HEAD_DIM, 3 * NSA_HEADS, GLA_HEADS * HEAD_DIM, GLA_HEADS * HEAD_DIM,
             GLA_HEADS * LANES, GLA_RANK, GLA_HEADS * LANES, 2 * dm)
    pts = np.concatenate([[0], np.cumsum(sizes)])
    seg = lambda w, i: w[:, pts[i]:pts[i + 1]]

    kv_p, win_p, gla_p, kv_s, gla_s = [], [], [], [], []
    for l in range(depth):
        wl = w_in[l]
        misc = jnp.concatenate([seg(wl, 2), seg(wl, 6), jnp.zeros((dm, LANES - 3 * NSA_HEADS - GLA_RANK), F32)], 1)
        w_a = jnp.concatenate([seg(wl, 0), seg(wl, 1), seg(wl, 3), seg(wl, 4), seg(wl, 5), misc], 1).astype(BF16)
        w_b = jnp.concatenate([seg(wl, 7), seg(wl, 8)], axis=1).astype(BF16)
        wa_pad = jnp.zeros((LANES, GLA_HEADS * HEAD_DIM), F32).at[3 * NSA_HEADS:3 * NSA_HEADS + GLA_RANK].set(
            gla_w_a2[l]).astype(BF16)
        qp, qr, kv, win, gq, gk, gv, la, ng = _inproj(h, w_a, cs_tab, sn_tab, wa_pad, gla_b_a[l][None, :],
                                                      seq // ROW_TILE, n_p // ROW_TILE)

        wck, pek, w2k = _compress_weights(cmp_k_pe[l], cmp_k_w1[l], cmp_k_w2[l])
        wcv, pev, w2v = _compress_weights(cmp_v_pe[l], cmp_v_w1[l], cmp_v_w2[l])
        wc = jnp.stack([wck, wcv])
        w1s = jnp.stack([cmp_k_w1[l], cmp_v_w1[l]]).reshape(2, 2, CMP_STRIDE // 4, 4 * HEAD_DIM, -1)
        wd = jnp.concatenate([w1s[:, 0], w1s[:, 1]], axis=-1).astype(BF16)
        pe_pair = _pe_bias(jnp.stack([pek, pev]), jnp.stack([cmp_k_w1[l], cmp_v_w1[l]]))
        w2c = jnp.stack([w2k, w2v])

        kc, vc = _compress_prompt(kv, wc, pe_pair, w2c, batch, seq)
        o_lo, o_hi = _nsa_prompt(qp, qr, ng, kc, vc, kv, win, ovl, batch, seq)
        half_rows = seq // 2
        o_hi = jnp.flip(o_hi.reshape(batch, half_rows // Q_TILE, Q_TILE, qw), axis=1)
        o_nsa_p = jnp.concatenate([o_lo.reshape(batch, half_rows, qw), o_hi.reshape(batch, half_rows, qw)],
                                  axis=1).reshape(n_p, qw)
        o_gla_p, st_p = _gla_prompt(gq, gk, gv, la, wlev, batch, seq)
        kv_p.append(kv[:n_p].reshape(batch, seq, 4, NSA_GROUPS, HEAD_DIM))
        wn = min(WINDOW, seq)
        win_p.append(win[:n_p].reshape(batch, seq, 2, NSA_GROUPS, HEAD_DIM)[:, seq - wn:])
        gla_p.append(st_p)

        sl = slice(n_p, n_p + dec_b)
        o8, win_buf = _nsa_decode(page_table, cache_cmp, cache_t, l, wd, pe_pair, w2c,
                                  qp[sl].reshape(dec_b, NSA_HEADS, HEAD_DIM), qr[sl].reshape(dec_b, NSA_HEADS, HEAD_DIM),
                                  ng[sl][:, None, :], kv[sl][:, None, :], win[sl][:, None, :], win_t, ovl, gg, selr,
                                  expand, win_buf)
        o_nsa_s = o8.reshape(dec_b, qw)
        o_gla_s, st_s = _gla_decode(gq[sl], gk[sl], la[sl], gv[sl], state_gla, l)
        kv_s.append(kv[sl].reshape(dec_b, 1, 4, NSA_GROUPS, HEAD_DIM))
        gla_s.append(st_s)

        pad = jnp.zeros((n_s_pad - dec_b, qw), F32)
        o_nsa = jnp.concatenate([o_nsa_p[:n_p], o_nsa_s, pad], axis=0)
        o_gla = jnp.concatenate([o_gla_p[:n_p], o_gla_s, pad], axis=0)
        h = _outproj(h, o_nsa, o_gla, w_b, gla_norm_g[l][None, :], w_nsa_up[l].astype(BF16),
                     w_gla_up[l].astype(BF16), w_out[l].astype(BF16), ln1_g[l][None, :], ln1_b[l][None, :], alpha)
        h = _mlp(h, mlp_w1[l].astype(BF16), mlp_w2[l].astype(BF16), ln2_g[l][None, :], ln2_b[l][None, :], alpha)

    y_prompt = h[:n_p].reshape(batch, seq, dm)
    y_sample = h[n_p:n_p + dec_b].reshape(dec_b, 1, dm)
    return (y_prompt, y_sample, jnp.stack(kv_p, axis=1), jnp.stack(win_p, axis=1), jnp.stack(gla_p, axis=1),
            jnp.stack(kv_s, axis=1), jnp.transpose(win_buf, (0, 1, 5, 2, 3, 4)), jnp.stack(gla_s, axis=1))
```

```python
import functools
import math

import numpy as np
import jax
import jax.numpy as jnp
from jax import lax
from jax.experimental import pallas as pl
from jax.experimental.pallas import tpu as pltpu

F32 = jnp.float32
BF16 = jnp.bfloat16

NSA_HEADS = 8
NSA_GROUPS = 2
NSA_HPG = NSA_HEADS // NSA_GROUPS
HEAD_DIM = 64
SCALE = HEAD_DIM ** -0.5
CMP_LEN = 32
CMP_STRIDE = 16
SEL_LEN = 64
N_SEL = 16
WINDOW = 512
GLA_HEADS = 4
GLA_RANK = 16
GLA_TAU = 16.0
ROPE_THETA = 10000.0
EPS = 1e-5
BIG = 1e6
NEG = -1e30
LOG2E = math.log2(math.e)

LANES = 128
ROW_TILE = 256
Q_TILE = 128
KV_TILE = 256
GLA_CHUNK = 256
GLA_DEC_TILE = 16
NSA_DEC_TILE = 4
VMEM_LIMIT = 56 * 1024 * 1024


def _cparams(n_axes):
    return pltpu.CompilerParams(dimension_semantics=("arbitrary",) * n_axes,
                                vmem_limit_bytes=VMEM_LIMIT)


def _dot(a, b):
    return jnp.dot(a, b, preferred_element_type=F32)


def _dot_nt(a, b):
    return lax.dot_general(a, b, (((1,), (1,)), ((), ())), preferred_element_type=F32)


def _split3(x):
    hi = x.astype(BF16)
    r = x - hi.astype(F32)
    mid = r.astype(BF16)
    lo = (r - mid.astype(F32)).astype(BF16)
    return hi, mid, lo


def _layer_norm(x, g, b):
    mu = jnp.mean(x, axis=-1, keepdims=True)
    xc = x - mu
    var = jnp.mean(xc * xc, axis=-1, keepdims=True)
    return xc * lax.rsqrt(var + EPS) * g + b


def _sigmoid(x):
    return 1.0 / (1.0 + jnp.exp(-x))


def _ln_kernel(x_ref, g_ref, b_ref, o_ref):
    o_ref[...] = _layer_norm(x_ref[...], g_ref[...], b_ref[...])


def _row_tile(n):
    return ROW_TILE if n % ROW_TILE == 0 else n


def _entry_norm(x, g, b):
    n, d = x.shape
    tm = _row_tile(n)
    return pl.pallas_call(
        _ln_kernel,
        name="entry_norm",
        grid=(n // tm,),
        in_specs=[pl.BlockSpec((tm, d), lambda i: (i, 0)),
                  pl.BlockSpec((1, d), lambda i: (0, 0)),
                  pl.BlockSpec((1, d), lambda i: (0, 0))],
        out_specs=pl.BlockSpec((tm, d), lambda i: (i, 0)),
        out_shape=jax.ShapeDtypeStruct((n, d), F32),
        compiler_params=_cparams(1),
    )(x, g.reshape(1, d), b.reshape(1, d))


C_Q, C_KV, C_GQ, C_GK, C_GV, C_MISC, C_END = 0, 512, 1280, 1536, 1792, 2304, 2432


def _rope128(x, cs, sn):
    lane = lax.broadcasted_iota(jnp.int32, x.shape, 1)
    first = (lane % HEAD_DIM) < (HEAD_DIM // 2)
    swapped = jnp.where(first, pltpu.roll(x, LANES - HEAD_DIM // 2, 1), pltpu.roll(x, HEAD_DIM // 2, 1))
    return x * cs + swapped * sn


def _inproj_kernel(h_ref, w_ref, cs_ref, sn_ref, wa_ref, ba_ref, *rest):
    qp_ref, qr_ref, kv_ref, win_ref, gq_ref, gk_ref, gv_ref, la_ref, ng_ref = rest[-9:]
    hb = h_ref[...].astype(BF16)
    cs = cs_ref[...]
    sn = sn_ref[...]

    def seg(lo, hi):
        return _dot(hb, w_ref[:, lo:hi])

    for j in range(4):
        qj = seg(C_Q + j * LANES, C_Q + (j + 1) * LANES)
        qp_ref[:, j * LANES:(j + 1) * LANES] = qj
        qr_ref[:, j * LANES:(j + 1) * LANES] = _rope128(qj, cs, sn)
    for s in range(6):
        x = seg(C_KV + s * LANES, C_KV + (s + 1) * LANES)
        if s in (2, 4):
            x = _rope128(x, cs, sn)
        if s < 4:
            kv_ref[:, s * LANES:(s + 1) * LANES] = x
        else:
            win_ref[:, (s - 4) * LANES:(s - 3) * LANES] = x
    gq_ref[...] = seg(C_GQ, C_GK) * (HEAD_DIM ** -0.5)
    gk_ref[...] = seg(C_GK, C_GV)
    gv_ref[...] = seg(C_GV, C_MISC)
    misc = seg(C_MISC, C_END)
    ng_ref[...] = misc
    x = _dot(misc.astype(BF16), wa_ref[...]) + ba_ref[...]
    la_ref[...] = (jnp.minimum(x, 0.0) - jnp.log(1.0 + jnp.exp(-jnp.abs(x)))) * (1.0 / GLA_TAU)


def _inproj(h, w_a, cs_tab, sn_tab, wa_pad, ba, kv_layout=None, kv_prev=None):
    n, d = h.shape
    tm = _row_tile(n)
    tab_tiles = cs_tab.shape[0] // tm

    def tab_map(i):
        return (i % tab_tiles, 0)

    row = lambda w: pl.BlockSpec((tm, w), lambda i: (i, 0))
    full = lambda a: pl.BlockSpec(a.shape, lambda i: (0,) * a.ndim)
    widths = (512, 512, 512, 256, 256, 256, 512, 256, 128)
    out_specs = [row(w) for w in widths]
    out_shape = [jax.ShapeDtypeStruct((n, w), F32) for w in widths]
    in_specs = [row(d), full(w_a), pl.BlockSpec((tm, LANES), tab_map), pl.BlockSpec((tm, LANES), tab_map),
                full(wa_pad), full(ba)]
    args = [h, w_a, cs_tab, sn_tab, wa_pad, ba]
    aliases = {}
    if kv_layout is not None:
        layer, depth = kv_layout
        out_specs[2] = pl.BlockSpec((tm, widths[2]),
                                    lambda i: ((i // tab_tiles * depth + layer) * tab_tiles + i % tab_tiles, 0))
        out_shape[2] = jax.ShapeDtypeStruct((n * depth, widths[2]), F32)
        if kv_prev is not None:
            in_specs.append(pl.BlockSpec(memory_space=pl.ANY))
            aliases = {len(args): 2}
            args.append(kv_prev)
    return pl.pallas_call(
        _inproj_kernel,
        name="inproj",
        grid=(n // tm,),
        in_specs=in_specs,
        out_specs=out_specs,
        out_shape=out_shape,
        input_output_aliases=aliases,
        compiler_params=_cparams(1),
    )(*args)


def _gelu_tanh(x):
    return 0.5 * x * (1.0 + jnp.tanh(math.sqrt(2.0 / math.pi) * (x + 0.044715 * x * x * x)))


def _compress_rows(chunk_pair, n_chunks, wc_ref, pe_ref, w2_ref, seq_chunks=None):
    seq_chunks = n_chunks if seq_chunks is None else seq_chunks
    outs = []
    for s in range(2):
        acc = [jnp.zeros((n_chunks, 2 * LANES), F32) for _ in range(NSA_GROUPS)]
        for lp in range(CMP_STRIDE // 2):
            a = chunk_pair(s, lp).astype(BF16)
            for g in range(NSA_GROUPS):
                acc[g] = acc[g] + _dot(a, wc_ref[s, g, lp])
        row = lax.broadcasted_iota(jnp.int32, (n_chunks, LANES), 0)
        parts = []
        for g in range(NSA_GROUPS):
            hid = acc[g][:, :LANES] + pltpu.roll(acc[g][:, LANES:], n_chunks - 1, 0) + pe_ref[s]
            parts.append(_dot(_gelu_tanh(hid).astype(BF16), w2_ref[s]))
        out = jnp.concatenate(parts, axis=1)
        outs.append(jnp.where(row % seq_chunks < seq_chunks - 1, out, 0.0))
    return outs


def _pe_bias_kernel(pe_ref, w1_ref, o_ref):
    for s in range(2):
        o_ref[s] = _dot(pe_ref[s].astype(BF16), w1_ref[s].astype(BF16))[0:1]


def _pe_bias(pe_flat, w1):
    hid = w1.shape[2]
    return pl.pallas_call(
        _pe_bias_kernel,
        name="pe_bias",
        out_shape=jax.ShapeDtypeStruct((2, 1, hid), F32),
    )(pe_flat, w1)


def _compress_kernel(xk_ref, xv_ref, wc_ref, pe_ref, w2_ref, kc_ref, vc_ref):
    x_refs = (xk_ref, xv_ref)
    n_chunks = xk_ref.shape[0] // CMP_STRIDE

    def chunk_pair(slot, lp):
        return jnp.concatenate([x_refs[slot][pl.ds(2 * lp, n_chunks, stride=CMP_STRIDE), :],
                                x_refs[slot][pl.ds(2 * lp + 1, n_chunks, stride=CMP_STRIDE), :]], axis=1)

    kc, vc = _compress_rows(chunk_pair, n_chunks, wc_ref, pe_ref, w2_ref)
    kc_ref[0] = kc
    vc_ref[0] = vc


def _compress_prompt(kv, wc, pe_pair, w2, batch, seq, layer, depth):
    nc = seq // CMP_STRIDE
    full = lambda a: pl.BlockSpec(a.shape, lambda b: (0,) * a.ndim)
    return pl.pallas_call(
        _compress_kernel,
        name="compress",
        grid=(batch,),
        in_specs=[pl.BlockSpec((seq, LANES), lambda b: (b * depth + layer, 0)),
                  pl.BlockSpec((seq, LANES), lambda b: (b * depth + layer, 1)),
                  full(wc), full(pe_pair), full(w2)],
        out_specs=[pl.BlockSpec((1, nc, LANES), lambda b: (b, 0, 0))] * 2,
        out_shape=[jax.ShapeDtypeStruct((batch, nc, LANES), F32)] * 2,
        compiler_params=_cparams(1),
    )(kv, kv, wc, pe_pair, w2)


def _select_mask(imp_t, qpos, n_blocks):
    nbp, nq = imp_t.shape
    blk = lax.broadcasted_iota(jnp.int32, (nbp, nq), 0)
    cur = qpos // SEL_LEN
    causal = blk * SEL_LEN <= qpos
    forced = (blk == 0) | (blk == cur) | (blk == cur - 1)
    score = jnp.where(forced, BIG, jnp.where(causal, imp_t, -jnp.inf))
    score = jnp.where(blk < n_blocks, score, -jnp.inf)
    rank = jnp.zeros((nbp, nq), jnp.int32)
    for j in range(n_blocks):
        other = score[j:j + 1, :]
        rank = rank + jnp.where(blk > j, jnp.where(other >= score, 1, 0), jnp.where(other > score, 1, 0))
    return (rank < N_SEL) & (score > -jnp.inf)


def _softmax_step(state, s, pv_fn):
    m, l, acc = state
    m_new = jnp.maximum(m, jnp.max(s, axis=0, keepdims=True))
    alpha = jnp.exp2(m - m_new)
    p = jnp.exp2(s - m_new)
    l = alpha * l + jnp.sum(p, axis=0, keepdims=True)
    acc = alpha * acc + pv_fn(p.astype(BF16))
    return m_new, l, acc


def _softmax_init(nq):
    return (jnp.full((1, nq), NEG, F32), jnp.zeros((1, nq), F32), jnp.zeros((LANES, nq), F32))


def _nsa_prompt_kernel(qpa_ref, qra_ref, nga_ref, qpb_ref, qrb_ref, ngb_ref, kc_ref, vc_ref, ks_ref, vs_ref,
                       kw_ref, vw_ref, ovl_ref, oa_ref, ob_ref, ksa_ref, vst_ref, kwa_ref, vwt_ref,
                       *, seq, n_blocks):
    g = pl.program_id(1)
    i = pl.program_id(2)
    tq = Q_TILE
    nqt = seq // tq
    n_kt = seq // KV_TILE
    nq = NSA_HPG * tq

    @pl.when((g == 0) & (i == 0))
    def _():
        for kt in range(n_kt):
            rows = pl.ds(kt * KV_TILE, KV_TILE)
            key = kt * KV_TILE + lax.broadcasted_iota(jnp.int32, (KV_TILE, LANES), 0)
            lane = lax.broadcasted_iota(jnp.int32, (KV_TILE, LANES), 1)
            onehot = jnp.where(key // SEL_LEN == lane, 1.0, 0.0).astype(BF16)
            ksa_ref[kt] = jnp.concatenate([ks_ref[rows, :].astype(BF16), onehot], axis=1)
            vst_ref[kt] = vs_ref[rows, :].T.astype(BF16)
            kwa_ref[kt] = kw_ref[rows, :].astype(BF16)
            vwt_ref[kt] = vw_ref[rows, :].T.astype(BF16)

    lane_q = lax.broadcasted_iota(jnp.int32, (1, nq), 1) % tq
    lane1 = lax.broadcasted_iota(jnp.int32, (1, tq), 1)
    lane = lax.broadcasted_iota(jnp.int32, (tq, LANES), 1)
    own = (lane // HEAD_DIM) == g
    krow = lax.broadcasted_iota(jnp.int32, (KV_TILE, nq), 0)
    ovl = ovl_ref[...]
    kc = kc_ref[0].astype(BF16)
    vct = vc_ref[0].T.astype(BF16)
    nc = kc.shape[0]
    nbp = -(-n_blocks // 8) * 8

    def own_rows(x):
        return jnp.where(g == 0, x[:HEAD_DIM], x[HEAD_DIM:])

    def prepare(qp_ref, qr_ref, qs):
        def stack_heads(ref, scale):
            parts = []
            for h in range(NSA_HPG):
                qh = ref[:, h * HEAD_DIM:(h + 1) * HEAD_DIM] * scale
                parts.append(jnp.where(own, jnp.concatenate([qh, qh], axis=1), 0.0))
            return parts

        qpos = qs + lane_q
        qp = jnp.concatenate(stack_heads(qp_ref, SCALE), axis=0).astype(BF16)
        s = _dot_nt(kc, qp)
        cblk = lax.broadcasted_iota(jnp.int32, (nc, nq), 0)
        valid = cblk * CMP_STRIDE + (CMP_LEN - 1) <= qpos
        s = jnp.where(valid, s, NEG)
        m = jnp.max(s, axis=0, keepdims=True)
        p = jnp.where(valid, jnp.exp(s - m), 0.0)
        p = p / jnp.maximum(jnp.sum(p, axis=0, keepdims=True), 1e-30)
        o_cmp = own_rows(_dot(vct, p.astype(BF16)))
        psum = p[:, 0:tq]
        for h in range(1, NSA_HPG):
            psum = psum + p[:, h * tq:(h + 1) * tq]
        imp_t = sum(_dot(ovl, piece) for piece in _split3(psum))
        sel = _select_mask(imp_t[:nbp], qs + lane1, n_blocks)
        bias_t = jnp.where(sel, 0.0, NEG)
        bias = jnp.concatenate([bias_t, jnp.zeros((LANES - nbp, tq), F32)], axis=0).T
        qr_parts = stack_heads(qr_ref, SCALE * LOG2E)
        q_sel = jnp.concatenate([jnp.concatenate([q, bias], axis=1) for q in qr_parts], axis=0).astype(BF16)
        q_win = jnp.concatenate(qr_parts, axis=0).astype(BF16)
        return qpos, o_cmp, q_sel, q_win

    qs_a = i * tq
    qs_b = (nqt - 1 - i) * tq
    qpos_a, o_cmp_a, q_sel_a, q_win_a = prepare(qpa_ref, qra_ref, qs_a)
    qpos_b, o_cmp_b, q_sel_b, q_win_b = prepare(qpb_ref, qrb_ref, qs_b)

    kd_a = qs_a // KV_TILE
    kd_b = qs_b // KV_TILE
    max_kd_a = ((nqt // 2 - 1) * tq) // KV_TILE
    n_sel = n_kt + 1
    n_win = WINDOW // KV_TILE + 1

    def sel_task(j):
        if j == 0:
            return 0, q_sel_a, qpos_a, True
        if j > max_kd_a:
            return j - kd_a - 1, q_sel_b, qpos_b, j == n_sel - 1
        in_a = j <= kd_a
        return (jnp.where(in_a, j, j - kd_a - 1), jnp.where(in_a, q_sel_a, q_sel_b),
                jnp.where(in_a, qpos_a, qpos_b), True)

    tasks = []
    for j in range(n_sel):
        kt, q, qpos, masked = sel_task(j)
        valid = (kt * KV_TILE + krow <= qpos) if masked else None
        tasks.append(("sel", j, ksa_ref, vst_ref, kt, q, valid))
    for name, kd, q, qpos in (("wa", kd_a, q_win_a, qpos_a), ("wb", kd_b, q_win_b, qpos_b)):
        for j in range(n_win):
            kt = kd - (n_win - 1) + j
            d = qpos - (kt * KV_TILE + krow)
            valid = (d >= 0) & (d <= WINDOW) & (kt >= 0)
            tasks.append((name, j, kwa_ref, vwt_ref, jnp.maximum(kt, 0), q, valid))

    def scores(task):
        _, _, k_ref, _, kt, q, _ = task
        return _dot_nt(k_ref[kt], q)

    results = {}
    state = None
    s_next = scores(tasks[0])
    for t, task in enumerate(tasks):
        name, j, _, vt_ref, kt, _, valid = task
        s = s_next
        if t + 1 < len(tasks):
            s_next = scores(tasks[t + 1])
        if j == 0:
            state = _softmax_init(nq)
        if name == "sel" and 1 <= j <= max_kd_a + 1:
            switch = j == kd_a + 1
            prev = results.get("sa", state)
            results["sa"] = tuple(jnp.where(switch, x, y) for x, y in zip(state, prev))
            state = tuple(jnp.where(switch, x, y) for x, y in zip(_softmax_init(nq), state))
        if valid is not None:
            s = jnp.where(valid, s, NEG)
        state = _softmax_step(state, s, lambda p: _dot(vt_ref[kt], p))
        last = (name == "sel" and j == n_sel - 1) or (name != "sel" and j == n_win - 1)
        if last:
            results["sb" if name == "sel" else name] = state

    def finish(key):
        _, l, acc = results[key]
        return own_rows(acc / l)

    for ng_ref, o_ref, branches in ((nga_ref, oa_ref, (o_cmp_a, finish("sa"), finish("wa"))),
                                    (ngb_ref, ob_ref, (o_cmp_b, finish("sb"), finish("wb")))):
        ng_t = ng_ref[...].T
        outs = []
        for h in range(NSA_HPG):
            cols = slice(h * tq, (h + 1) * tq)
            tot = jnp.zeros((HEAD_DIM, tq), F32)
            for r, o in enumerate(branches):
                i0 = h * 3 + r
                i1 = (NSA_HPG + h) * 3 + r
                gate = _sigmoid(jnp.where(g == 0, ng_t[i0:i0 + 1], ng_t[i1:i1 + 1]))
                tot = tot + gate * o[:, cols]
            outs.append(tot.T)
        o_ref[...] = jnp.concatenate(outs, axis=1)


def _nsa_prompt(qp, qr, ng, kc, vc, kv, win, ovl, batch, seq, layer, depth):
    n_blocks = -(-seq // SEL_LEN)
    nqt = seq // Q_TILE
    n_kt = seq // KV_TILE
    assert nqt % 2 == 0 and KV_TILE == 2 * Q_TILE and WINDOW % KV_TILE == 0
    gw = NSA_HPG * HEAD_DIM
    half = nqt // 2
    qa = pl.BlockSpec((Q_TILE, gw), lambda b, g, i: (b * nqt + i, g))
    qb = pl.BlockSpec((Q_TILE, gw), lambda b, g, i: (b * nqt + nqt - 1 - i, g))
    na = pl.BlockSpec((Q_TILE, LANES), lambda b, g, i: (b * nqt + i, 0))
    nb = pl.BlockSpec((Q_TILE, LANES), lambda b, g, i: (b * nqt + nqt - 1 - i, 0))
    cspec = pl.BlockSpec((1,) + kc.shape[1:], lambda b, g, i: (b, 0, 0))
    ospec = pl.BlockSpec((Q_TILE, gw), lambda b, g, i: (b * half + i, g))

    def kvspec(col):
        return pl.BlockSpec((seq, LANES), lambda b, g, i: (b * depth + layer, col))

    def winspec(col):
        return pl.BlockSpec((seq, LANES), lambda b, g, i: (b, col))

    oshape = jax.ShapeDtypeStruct((batch * half * Q_TILE, qp.shape[1]), F32)
    return pl.pallas_call(
        functools.partial(_nsa_prompt_kernel, seq=seq, n_blocks=n_blocks),
        name="nsa_prompt",
        grid=(batch, NSA_GROUPS, half),
        in_specs=[qa, qa, na, qb, qb, nb, cspec, cspec, kvspec(2), kvspec(3), winspec(0), winspec(1),
                  pl.BlockSpec(ovl.shape, lambda b, g, i: (0, 0))],
        out_specs=[ospec, ospec],
        out_shape=[oshape, oshape],
        scratch_shapes=[pltpu.VMEM((n_kt, KV_TILE, 2 * LANES), BF16), pltpu.VMEM((n_kt, LANES, KV_TILE), BF16),
                        pltpu.VMEM((n_kt, KV_TILE, LANES), BF16), pltpu.VMEM((n_kt, LANES, KV_TILE), BF16)],
        compiler_params=_cparams(3),
    )(qp, qr, ng, qp, qr, ng, kc, vc, kv, kv, win, win, ovl)


def _gla_level_matrix(c):
    t = np.arange(c)[:, None]
    u = np.arange(c)[None, :]
    mats = [(u <= t), (u > t)]
    m = c
    while m >= 2:
        split = (t // m) * m + m // 2
        upper = (t % m) >= m // 2
        mats.append(np.where(upper, (u >= split) & (u <= t), (u > t) & (u < split)))
        m //= 2
    return np.concatenate(mats, axis=0).astype(np.float32)


def _gla_prompt_kernel(q_ref, k_ref, v_ref, la_ref, w_ref, o_ref, st_ref, e_ref, s_ref, *, seq):
    c = GLA_CHUNK
    n_levels = int(math.log2(c))
    s_ref[...] = jnp.zeros_like(s_ref)
    row = lax.broadcasted_iota(jnp.int32, (c, c), 0)
    col = lax.broadcasted_iota(jnp.int32, (c, c), 1)
    rowl = lax.broadcasted_iota(jnp.int32, (c, LANES), 0)
    lanel = lax.broadcasted_iota(jnp.int32, (c, LANES), 1)

    def chunk(ci, _):
        rows = pl.ds(pl.multiple_of(ci * c, c), c)
        la = la_ref[rows, :]
        hi = la.astype(BF16)
        lo = (la - hi.astype(F32)).astype(BF16)
        e_ref[...] = jnp.exp(_dot(w_ref[...], hi) + _dot(w_ref[...], lo))
        for pair in range(GLA_HEADS // 2):
            lanes = slice(pair * LANES, (pair + 1) * LANES)
            q = q_ref[rows, lanes]
            k = k_ref[rows, lanes]
            q0 = (q * e_ref[0:c, lanes]).astype(BF16)
            kdec = k * e_ref[c:2 * c, lanes]
            a_last = e_ref[c - 1:c, lanes]
            qls, kls = [], []
            for lv in range(n_levels):
                m = c >> lv
                x = e_ref[(2 + lv) * c:(3 + lv) * c, lanes]
                upper = (rowl % m) >= (m // 2)
                qls.append(jnp.where(upper, q * x, 0.0).astype(BF16))
                kls.append(jnp.where(upper, 0.0, k * x))
            for hh in range(2):
                head = pair * 2 + hh
                mine = (lanel // HEAD_DIM) == hh
                a = jnp.where(row == col, _dot_nt(q.astype(BF16), jnp.where(mine, k, 0.0).astype(BF16)), 0.0)
                for lv in range(n_levels):
                    m = c >> lv
                    same = (row // m) == (col // m)
                    a = a + jnp.where(same, _dot_nt(qls[lv], jnp.where(mine, kls[lv], 0.0).astype(BF16)), 0.0)
                v = v_ref[rows, head * LANES:(head + 1) * LANES]
                st = s_ref[head]
                o = _dot(a.astype(BF16), v.astype(BF16)) + _dot_nt(q0, st.astype(BF16))
                o_ref[rows, head * LANES:(head + 1) * LANES] = o
                kd = jnp.where(mine, kdec, 0.0).astype(BF16)
                s_ref[head] = st * a_last + _dot(v.T.astype(BF16), kd)
        return 0

    lax.fori_loop(0, seq // c, chunk, 0)
    for head in range(GLA_HEADS):
        st = s_ref[head].T
        off = (head % 2) * HEAD_DIM
        st_ref[0, head] = st[off:off + HEAD_DIM]


def _gla_prompt(gq, gk, gv, la, wlev, batch, seq):
    dk2 = gq.shape[1]
    dv4 = gv.shape[1]
    return pl.pallas_call(
        functools.partial(_gla_prompt_kernel, seq=seq),
        name="gla_prompt",
        grid=(batch,),
        in_specs=[pl.BlockSpec((seq, dk2), lambda b: (b, 0)), pl.BlockSpec((seq, dk2), lambda b: (b, 0)),
                  pl.BlockSpec((seq, dv4), lambda b: (b, 0)), pl.BlockSpec((seq, dk2), lambda b: (b, 0)),
                  pl.BlockSpec(wlev.shape, lambda b: (0, 0))],
        out_specs=[pl.BlockSpec((seq, dv4), lambda b: (b, 0)),
                   pl.BlockSpec((1, GLA_HEADS, HEAD_DIM, LANES), lambda b: (b, 0, 0, 0))],
        out_shape=[jax.ShapeDtypeStruct((gq.shape[0], dv4), F32),
                   jax.ShapeDtypeStruct((batch, GLA_HEADS, HEAD_DIM, LANES), F32)],
        scratch_shapes=[pltpu.VMEM((wlev.shape[0], dk2), F32), pltpu.VMEM((GLA_HEADS, LANES, LANES), F32)],
        compiler_params=_cparams(1),
    )(gq, gk, gv, la, wlev)


def _outproj_kernel(h_ref, on_ref, og_ref, wb_ref, gn_ref, wn_ref, wg_ref, wo_ref, g1_ref, b1_ref, o_ref,
                    *, alpha):
    h = h_ref[...]
    hb = h.astype(BF16)
    dm = h.shape[1]
    gw = og_ref.shape[1]
    g_r = _dot(hb, wb_ref[:, :gw])
    parts = []
    for head in range(GLA_HEADS):
        x = og_ref[:, head * LANES:(head + 1) * LANES]
        x = x * lax.rsqrt(jnp.mean(x * x, axis=-1, keepdims=True) + EPS) * gn_ref[...]
        gr = g_r[:, head * LANES:(head + 1) * LANES]
        parts.append(x * (gr * _sigmoid(gr)))
    og = jnp.concatenate(parts, axis=1).astype(BF16)
    a = _dot(on_ref[...].astype(BF16), wn_ref[...])
    c = _dot(og, wg_ref[...])
    m_a = _dot(hb, wb_ref[:, gw:gw + dm])
    m_c = _dot(hb, wb_ref[:, gw + dm:gw + 2 * dm])
    mix = (_sigmoid(m_a) * a + _sigmoid(m_c) * c).astype(BF16)
    y = _dot(mix, wo_ref[...])
    o_ref[...] = _layer_norm(alpha * h + y, g1_ref[...], b1_ref[...])


def _outproj(h, o_nsa, o_gla, w_b, gn, w_nsa, w_gla, w_out, g1, b1, alpha):
    n, d = h.shape
    tm = _row_tile(n)
    row = lambda w: pl.BlockSpec((tm, w), lambda i: (i, 0))
    full = lambda a: pl.BlockSpec(a.shape, lambda i: (0,) * a.ndim)
    return pl.pallas_call(
        functools.partial(_outproj_kernel, alpha=alpha),
        name="outproj",
        grid=(n // tm,),
        in_specs=[row(d), row(o_nsa.shape[1]), row(o_gla.shape[1]), full(w_b), full(gn), full(w_nsa),
                  full(w_gla), full(w_out), full(g1), full(b1)],
        out_specs=row(d),
        out_shape=jax.ShapeDtypeStruct((n, d), F32),
        compiler_params=_cparams(1),
    )(h, o_nsa, o_gla, w_b, gn, w_nsa, w_gla, w_out, g1, b1)


def _mlp_kernel(h_ref, w1_ref, w2_ref, g_ref, b_ref, o_ref, *, alpha):
    h = h_ref[...]
    hb = h.astype(BF16)
    dff = w1_ref.shape[1]
    step = 1024
    f = jnp.zeros(h.shape, F32)
    for c0 in range(0, dff, step):
        u = jnp.maximum(_dot(hb, w1_ref[:, c0:c0 + step]), 0.0)
        f = f + _dot((u * u).astype(BF16), w2_ref[c0:c0 + step, :])
    o_ref[...] = _layer_norm(alpha * h + f, g_ref[...], b_ref[...])


def _mlp(h, w1, w2, g, b, alpha):
    n, d = h.shape
    tm = _row_tile(n)
    row = pl.BlockSpec((tm, d), lambda i: (i, 0))
    full = lambda a: pl.BlockSpec(a.shape, lambda i: (0,) * a.ndim)
    return pl.pallas_call(
        functools.partial(_mlp_kernel, alpha=alpha),
        name="mlp",
        grid=(n // tm,),
        in_specs=[row, full(w1), full(w2), full(g), full(b)],
        out_specs=row,
        out_shape=jax.ShapeDtypeStruct((n, d), F32),
        compiler_params=_cparams(1),
    )(h, w1, w2, g, b)


def _softmax_rows(parts, extra, valid_extra=None):
    m = extra
    for s in parts:
        m = jnp.maximum(m, jnp.max(s, axis=-1, keepdims=True))
    ps = [jnp.exp(s - m) for s in parts]
    pe = jnp.exp(extra - m)
    tot = pe
    for p in ps:
        tot = tot + jnp.sum(p, axis=-1, keepdims=True)
    inv = 1.0 / tot
    return [p * inv for p in ps], pe * inv


def _nsa_decode_kernel(pt_ref, *refs, n_pages, n_blocks, past, has_prev):
    del pt_ref
    nb = NSA_DEC_TILE
    pages_all = [refs[bb * n_pages:(bb + 1) * n_pages] for bb in range(nb)]
    rest = refs[nb * n_pages:]
    (wc_ref, pe_ref, w2_ref, qp_ref, qr_ref, ng_ref, kvn_ref, winn_ref, wb_ref, ovl_ref, gg_ref, selr_ref,
     exp_ref, perm_ref) = rest[:14]
    o_ref, wo_ref = rest[-2:]
    page = pages_all[0][0].shape[-1]
    wbuf = wb_ref.shape[-1]
    n_rows = qp_ref.shape[1]
    seq_chunks = past // CMP_STRIDE
    n_chunks = nb * seq_chunks
    cpp = page // CMP_STRIDE
    n_lp = CMP_STRIDE // 2

    def chunk_pairs(slot):
        gathered = []
        for pages in pages_all:
            for pg in pages:
                xt = pg[slot].reshape(NSA_GROUPS * HEAD_DIM, page).astype(BF16)
                gathered.append([_dot_nt(perm_ref[par], xt) for par in range(2)])
        return [jnp.concatenate(
            [jnp.concatenate([even[lp * cpp:(lp + 1) * cpp], odd[lp * cpp:(lp + 1) * cpp]], axis=1)
             for even, odd in gathered], axis=0) for lp in range(n_lp)]

    pairs = [chunk_pairs(slot) for slot in range(2)]
    kc_all, vc_all = _compress_rows(lambda slot, lp: pairs[slot][lp], n_chunks, wc_ref, pe_ref, w2_ref, seq_chunks)
    streams = [_nsa_decode_one(bb, kc_all[bb * seq_chunks:(bb + 1) * seq_chunks],
                               vc_all[bb * seq_chunks:(bb + 1) * seq_chunks], pages_all[bb], qp_ref, qr_ref, ng_ref,
                               kvn_ref, winn_ref, wb_ref, ovl_ref, gg_ref, selr_ref, exp_ref, o_ref, wo_ref,
                               n_blocks, past) for bb in range(nb)]
    while streams:
        streams = [st for st in streams if next(st, "done") is None]


def _nsa_decode_one(bb, kc, vc, sel_pages, qp_ref, qr_ref, ng_ref, kvn_ref, winn_ref, wb_ref, ovl_ref, gg_ref,
                    selr_ref, exp_ref, o_ref, wo_ref, n_blocks, past):
    page = sel_pages[0].shape[-1]
    wbuf = wb_ref.shape[-1]
    n_rows = qp_ref.shape[1]
    row8 = lax.broadcasted_iota(jnp.int32, (n_rows, LANES), 0)
    lane8 = lax.broadcasted_iota(jnp.int32, (n_rows, LANES), 1)
    own = (lane8 // HEAD_DIM) == (row8 // NSA_HPG)

    def by_group(x0, x1):
        return jnp.where(lax.broadcasted_iota(jnp.int32, x0.shape, 0) < NSA_HPG, x0, x1)

    qp = qp_ref[bb] * SCALE
    qr = qr_ref[bb] * SCALE
    qrb = qr.astype(BF16)
    kvn = kvn_ref[bb]
    winn = winn_ref[bb]

    def new_key_scores(krow):
        prod = jnp.concatenate([qr, qr], axis=1) * krow
        return jnp.sum(jnp.where(own, prod, 0.0), axis=-1, keepdims=True)

    def new_value(vrow):
        v = jnp.broadcast_to(vrow, (n_rows, LANES))
        return by_group(v[:, :HEAD_DIM], v[:, HEAD_DIM:])

    nc = kc.shape[0]
    qp_pair = jnp.where(own, jnp.concatenate([qp, qp], axis=1), 0.0)
    s_cmp = _dot_nt(qp_pair.astype(BF16), kc.astype(BF16))
    s_sel = [by_group(_dot(qrb, pg[2, 0].astype(BF16)), _dot(qrb, pg[2, 1].astype(BF16))) for pg in sel_pages]
    s_win = by_group(_dot(qrb, wb_ref[bb, 0, 0].astype(BF16)), _dot(qrb, wb_ref[bb, 0, 1].astype(BF16)))
    pieces = _split3(jnp.broadcast_to(_sigmoid(ng_ref[bb]), (LANES, LANES)))
    gates = [sum(_dot_nt(selr_ref[r], piece) for piece in pieces)[:, :HEAD_DIM] for r in range(3)]
    yield

    cblk = lax.broadcasted_iota(jnp.int32, (n_rows, nc), 1)
    valid = cblk * CMP_STRIDE + (CMP_LEN - 1) <= past
    s = jnp.where(valid, s_cmp, NEG)
    p = jnp.where(valid, jnp.exp(s - jnp.max(s, axis=-1, keepdims=True)), 0.0)
    p = p / jnp.maximum(jnp.sum(p, axis=-1, keepdims=True), 1e-30)
    o = _dot(p.astype(BF16), vc.astype(BF16))
    imp_h = sum(_dot_nt(piece, ovl_ref[...]) for piece in _split3(p))
    ps, p_new_w = _softmax_rows([s_win], new_key_scores(winn[:, 0:LANES]))
    pb = ps[0].astype(BF16)
    o_win = by_group(_dot_nt(pb, wb_ref[bb, 1, 0].astype(BF16)), _dot_nt(pb, wb_ref[bb, 1, 1].astype(BF16)))
    yield
    o_cmp = by_group(o[:, :HEAD_DIM], o[:, HEAD_DIM:])
    o_win = o_win + p_new_w * new_value(winn[:, LANES:2 * LANES])
    imp_h = jnp.concatenate([imp_h, jnp.zeros((LANES - n_rows, LANES), F32)], axis=0)
    imp = sum(_dot(gg_ref[...], piece) for piece in _split3(imp_h))
    yield

    nbp = -(-n_blocks // 8) * 8
    sel = _select_mask(imp.T[:nbp], jnp.full((1, LANES), past, jnp.int32), n_blocks)
    bias_t = jnp.where(sel, 0.0, NEG)
    bias_t = jnp.concatenate([bias_t, jnp.zeros((LANES - nbp, LANES), F32)], axis=0)
    bias = bias_t.T[:n_rows]
    bias_keys = _dot(bias.astype(BF16), exp_ref[...])
    yield

    parts = [s + bias_keys[:, pi * page:(pi + 1) * page] for pi, s in enumerate(s_sel)]
    blk_new = past // SEL_LEN
    s_new = new_key_scores(kvn[:, 2 * LANES:3 * LANES]) + bias[:, blk_new:blk_new + 1]
    ps, p_new = _softmax_rows(parts, s_new)
    acc0 = jnp.zeros((n_rows, HEAD_DIM), F32)
    acc1 = jnp.zeros((n_rows, HEAD_DIM), F32)
    for pg, p in zip(sel_pages, ps):
        pb = p.astype(BF16)
        acc0 = acc0 + _dot_nt(pb, pg[3, 0].astype(BF16))
        acc1 = acc1 + _dot_nt(pb, pg[3, 1].astype(BF16))
    yield
    o_sel = by_group(acc0, acc1) + p_new * new_value(kvn[:, 3 * LANES:4 * LANES])
    o_ref[bb] = gates[0] * o_cmp + gates[1] * o_sel + gates[2] * o_win

    lane_w = lax.broadcasted_iota(jnp.int32, (HEAD_DIM, LANES), 1)
    n_col = wbuf // LANES
    for kv in range(2):
        tile = jnp.concatenate([winn[:, kv * LANES:(kv + 1) * LANES], jnp.zeros((LANES - 1, LANES), F32)], axis=0)
        new_t = tile.T
        for grp in range(NSA_GROUPS):
            col = new_t[grp * HEAD_DIM:(grp + 1) * HEAD_DIM, 0:1]
            rolled = [pltpu.roll(wb_ref[bb, kv, grp, :, c * LANES:(c + 1) * LANES], LANES - 1, 1)
                      for c in range(n_col)]
            for c in range(n_col):
                nxt = rolled[c + 1] if c + 1 < n_col else jnp.broadcast_to(col, (HEAD_DIM, LANES))
                wo_ref[bb, kv, grp, :, c * LANES:(c + 1) * LANES] = jnp.where(lane_w == LANES - 1, nxt, rolled[c])


def _nsa_decode(page_table, cache_t, layer, wc, pe_pair, w2, qp8, qr8, ng3, kvn3, winn3, win_t, ovl, gg,
                selr, expand, perm, win_prev):
    dec_b, n_pages = page_table.shape
    page = cache_t.shape[-1]
    past = n_pages * page
    wbuf = win_t.shape[-1]
    depth = win_t.shape[1]
    assert wbuf <= WINDOW and wbuf % LANES == 0
    n_blocks = -(-(past + 1) // SEL_LEN)
    nb = NSA_DEC_TILE
    assert dec_b % nb == 0
    full = lambda a: pl.BlockSpec(a.shape, lambda b, pt: (0,) * a.ndim)
    per_b = lambda a: pl.BlockSpec((nb,) + a.shape[1:], lambda b, pt: (b,) + (0,) * (a.ndim - 1))
    wshape = (nb, None, 2, NSA_GROUPS, HEAD_DIM, wbuf)

    def page_spec(bb, p):
        return pl.BlockSpec((None, None) + cache_t.shape[2:], lambda b, pt: (pt[b * nb + bb, p], layer, 0, 0, 0, 0))

    in_specs = [page_spec(bb, p) for bb in range(nb) for p in range(n_pages)] + [
        full(wc), full(pe_pair), full(w2), per_b(qp8), per_b(qr8), per_b(ng3), per_b(kvn3), per_b(winn3),
        pl.BlockSpec(wshape, lambda b, pt: (b, layer, 0, 0, 0, 0)), full(ovl), full(gg), full(selr), full(expand),
        full(perm)]
    args = [page_table] + [cache_t] * (nb * n_pages) + [
        wc, pe_pair, w2, qp8, qr8, ng3, kvn3, winn3, win_t, ovl, gg, selr, expand, perm]
    aliases = {}
    if win_prev is not None:
        in_specs.append(pl.BlockSpec(memory_space=pl.ANY))
        aliases = {len(args): 1}
        args.append(win_prev)
    grid_spec = pltpu.PrefetchScalarGridSpec(
        num_scalar_prefetch=1,
        grid=(dec_b // nb,),
        in_specs=in_specs,
        out_specs=[per_b(qp8), pl.BlockSpec(wshape, lambda b, pt: (b, layer, 0, 0, 0, 0))])
    return pl.pallas_call(
        functools.partial(_nsa_decode_kernel, n_pages=n_pages, n_blocks=n_blocks, past=past,
                          has_prev=win_prev is not None),
        name="nsa_decode",
        grid_spec=grid_spec,
        out_shape=[jax.ShapeDtypeStruct(qp8.shape, F32),
                   jax.ShapeDtypeStruct((dec_b, depth, 2, NSA_GROUPS, HEAD_DIM, wbuf), F32)],
        input_output_aliases=aliases,
        compiler_params=_cparams(1),
    )(*args)


def _gla_decode_kernel(q_ref, k_ref, la_ref, v_ref, s_ref, o_ref, so_ref, qt_ref, kt_ref, at_ref):
    i = pl.program_id(0)
    bt = GLA_DEC_TILE
    n_tiles = qt_ref.shape[0]

    @pl.when(i == 0)
    def _():
        qt = q_ref[...].T
        kt = k_ref[...].T
        at = jnp.exp(la_ref[...]).T
        for j in range(n_tiles):
            qt_ref[j] = qt[:, j * bt:(j + 1) * bt]
            kt_ref[j] = kt[:, j * bt:(j + 1) * bt]
            at_ref[j] = at[:, j * bt:(j + 1) * bt]

    qt = qt_ref[i]
    kt = kt_ref[i]
    at = at_ref[i]
    for bb in range(bt):
        for head in range(GLA_HEADS):
            rows = slice(head * HEAD_DIM, (head + 1) * HEAD_DIM)
            v = v_ref[bb:bb + 1, head * LANES:(head + 1) * LANES]
            st = at[rows, bb:bb + 1] * s_ref[bb, 0, head] + kt[rows, bb:bb + 1] * v
            so_ref[bb, head] = st
            o_ref[bb:bb + 1, head * LANES:(head + 1) * LANES] = jnp.sum(qt[rows, bb:bb + 1] * st, axis=0,
                                                                        keepdims=True)


def _gla_decode(gq_s, gk_s, la_s, gv_s, state, layer):
    dec_b, dk4 = gq_s.shape
    bt = GLA_DEC_TILE
    n_tiles = dec_b // bt
    full = lambda a: pl.BlockSpec(a.shape, lambda i: (0,) * a.ndim)
    sblk = (bt, None, GLA_HEADS, HEAD_DIM, LANES)
    return pl.pallas_call(
        _gla_decode_kernel,
        name="gla_decode",
        grid=(n_tiles,),
        in_specs=[full(gq_s), full(gk_s), full(la_s), pl.BlockSpec((bt, gv_s.shape[1]), lambda i: (i, 0)),
                  pl.BlockSpec((bt, 1, GLA_HEADS, HEAD_DIM, LANES), lambda i: (i, layer, 0, 0, 0))],
        out_specs=[pl.BlockSpec((bt, gv_s.shape[1]), lambda i: (i, 0)),
                   pl.BlockSpec((bt, GLA_HEADS, HEAD_DIM, LANES), lambda i: (i, 0, 0, 0))],
        out_shape=[jax.ShapeDtypeStruct(gv_s.shape, F32),
                   jax.ShapeDtypeStruct((dec_b, GLA_HEADS, HEAD_DIM, LANES), F32)],
        scratch_shapes=[pltpu.VMEM((n_tiles, dk4, bt), F32)] * 3,
        compiler_params=_cparams(1),
    )(gq_s, gk_s, la_s, gv_s, state)


def _overlap_t():
    r, w = SEL_LEN // CMP_STRIDE, CMP_LEN // CMP_STRIDE
    off = (np.arange(r)[:, None] + np.arange(w)[None, :]).reshape(-1)
    j = np.arange(LANES)
    c = np.arange(LANES)
    ov = np.sum(c[None, :, None] == (r * j[:, None, None] + off[None, None, :]), axis=-1)
    return ov.astype(np.float32)


def _rope_tables(pos):
    half = HEAD_DIM // 2
    inv = ROPE_THETA ** (-jnp.arange(half, dtype=F32) / half)
    ang = pos[:, None] * inv[None, :]
    cos, sin = jnp.cos(ang), jnp.sin(ang)
    cs = jnp.concatenate([cos, cos, cos, cos], axis=1)
    sn = jnp.concatenate([-sin, sin, -sin, sin], axis=1)
    return cs, sn


def _compress_weights(pe, w1, w2):
    dh = HEAD_DIM
    w1r = w1.reshape(CMP_LEN, dh, -1)
    hid = w1r.shape[-1]
    z = jnp.zeros((dh, hid), w1.dtype)
    groups = []
    for g in range(NSA_GROUPS):
        mats = []
        for lp in range(CMP_STRIDE // 2):
            halves = []
            for base in (0, CMP_STRIDE):
                blocks = []
                for l in (2 * lp, 2 * lp + 1):
                    blocks += [w1r[base + l], z] if g == 0 else [z, w1r[base + l]]
                halves.append(jnp.concatenate(blocks, axis=0))
            mats.append(jnp.concatenate(halves, axis=1))
        groups.append(jnp.stack(mats))
    wc = jnp.stack(groups).astype(BF16)
    pe_flat = jnp.broadcast_to(pe.reshape(1, -1), (8, pe.size))
    return wc, pe_flat, w2.astype(BF16)


def kernel(x_prompt, x_sample, cache_nsa_kv, cache_win_kv, state_gla, page_table, ln_in_g, ln_in_b, w_in, cmp_k_pe, cmp_k_w1, cmp_k_w2, cmp_v_pe, cmp_v_w1, cmp_v_w2, gla_w_a2, gla_b_a, gla_norm_g, w_nsa_up, w_gla_up, w_out, ln1_g, ln1_b, mlp_w1, mlp_w2, ln2_g, ln2_b):
    batch, seq, dm = x_prompt.shape
    dec_b = x_sample.shape[0]
    depth = w_in.shape[0]
    n_phys, _, page = cache_nsa_kv.shape[:3]
    past = page_table.shape[1] * page
    wbuf = cache_win_kv.shape[2]
    alpha = (2.0 * depth) ** 0.25
    n_p = batch * seq
    qw = NSA_HEADS * HEAD_DIM

    h_p = _entry_norm(x_prompt.reshape(n_p, dm), ln_in_g, ln_in_b)
    h_s = _entry_norm(x_sample.reshape(dec_b, dm), ln_in_g, ln_in_b)

    cs_p, sn_p = _rope_tables(jnp.arange(seq, dtype=F32))
    cs_s, sn_s = _rope_tables(jnp.full((dec_b,), past, F32))
    ovl = jnp.asarray(_overlap_t())
    wlev = jnp.asarray(_gla_level_matrix(GLA_CHUNK)).astype(BF16)
    col = np.arange(LANES)
    gg = jnp.asarray(((col[:, None] // NSA_HPG == col[None, :] // NSA_HPG)
                      & (col[:, None] < NSA_HEADS) & (col[None, :] < NSA_HEADS)).astype(np.float32)).astype(BF16)
    selr = jnp.asarray(np.stack([col[None, :] == np.arange(NSA_HEADS)[:, None] * 3 + r
                                 for r in range(3)]).astype(np.float32)).astype(BF16)
    cache_t = jnp.transpose(cache_nsa_kv, (0, 1, 3, 4, 5, 2))
    win_t = jnp.transpose(cache_win_kv, (0, 1, 3, 4, 5, 2))
    expand = jnp.asarray((np.arange(past)[None, :] // SEL_LEN == col[:, None]).astype(np.float32)).astype(BF16)
    r_idx = np.arange(page)[None, :]
    m_idx = np.arange(page // 2)[:, None]
    cpp = page // CMP_STRIDE
    perm = jnp.asarray(np.stack([r_idx == (m_idx % cpp) * CMP_STRIDE + 2 * (m_idx // cpp) + par
                                 for par in range(2)]).astype(np.float32)).astype(BF16)
    win_buf = None

    sizes = (qw, 6 * NSA_GROUPS * HEAD_DIM, 3 * NSA_HEADS, GLA_HEADS * HEAD_DIM, GLA_HEADS * HEAD_DIM,
             GLA_HEADS * LANES, GLA_RANK, GLA_HEADS * LANES, 2 * dm)
    pts = np.concatenate([[0], np.cumsum(sizes)])
    seg = lambda w, i: w[:, pts[i]:pts[i + 1]]

    kv_big = None
    win_p, gla_p, kv_s, gla_s = [], [], [], []
    for l in range(depth):
        wl = w_in[l]
        misc = jnp.concatenate([seg(wl, 2), seg(wl, 6), jnp.zeros((dm, LANES - 3 * NSA_HEADS - GLA_RANK), F32)], 1)
        w_a = jnp.concatenate([seg(wl, 0), seg(wl, 1), seg(wl, 3), seg(wl, 4), seg(wl, 5), misc], 1).astype(BF16)
        w_b = jnp.concatenate([seg(wl, 7), seg(wl, 8)], axis=1).astype(BF16)
        wa_pad = jnp.zeros((LANES, GLA_HEADS * HEAD_DIM), F32).at[3 * NSA_HEADS:3 * NSA_HEADS + GLA_RANK].set(
            gla_w_a2[l]).astype(BF16)
        ba = gla_b_a[l][None, :]
        w_nu, w_gu, w_o = w_nsa_up[l].astype(BF16), w_gla_up[l].astype(BF16), w_out[l].astype(BF16)
        w1, w2 = mlp_w1[l].astype(BF16), mlp_w2[l].astype(BF16)
        gn, g1, b1, g2, b2 = (gla_norm_g[l][None, :], ln1_g[l][None, :], ln1_b[l][None, :], ln2_g[l][None, :],
                              ln2_b[l][None, :])

        wck, pek, w2k = _compress_weights(cmp_k_pe[l], cmp_k_w1[l], cmp_k_w2[l])
        wcv, pev, w2v = _compress_weights(cmp_v_pe[l], cmp_v_w1[l], cmp_v_w2[l])
        wc = jnp.stack([wck, wcv])
        pe_pair = _pe_bias(jnp.stack([pek, pev]), jnp.stack([cmp_k_w1[l], cmp_v_w1[l]]))
        w2c = jnp.stack([w2k, w2v])

        qp, qr, kv_big, win, gq, gk, gv, la, ng = _inproj(h_p, w_a, cs_p, sn_p, wa_pad, ba, (l, depth), kv_big)
        kc, vc = _compress_prompt(kv_big, wc, pe_pair, w2c, batch, seq, l, depth)
        o_lo, o_hi = _nsa_prompt(qp, qr, ng, kc, vc, kv_big, win, ovl, batch, seq, l, depth)
        half_rows = seq // 2
        o_hi = jnp.flip(o_hi.reshape(batch, half_rows // Q_TILE, Q_TILE, qw), axis=1)
        o_nsa = jnp.concatenate([o_lo.reshape(batch, half_rows, qw), o_hi.reshape(batch, half_rows, qw)],
                                axis=1).reshape(n_p, qw)
        o_gla, st_p = _gla_prompt(gq, gk, gv, la, wlev, batch, seq)
        wn = min(WINDOW, seq)
        win_p.append(win.reshape(batch, seq, 2, NSA_GROUPS, HEAD_DIM)[:, seq - wn:])
        gla_p.append(st_p)
        h_p = _outproj(h_p, o_nsa, o_gla, w_b, gn, w_nu, w_gu, w_o, g1, b1, alpha)
        h_p = _mlp(h_p, w1, w2, g2, b2, alpha)

        qp, qr, kv, win, gq, gk, gv, la, ng = _inproj(h_s, w_a, cs_s, sn_s, wa_pad, ba)
        o8, win_buf = _nsa_decode(page_table, cache_t, l, wc, pe_pair, w2c,
                                  qp.reshape(dec_b, NSA_HEADS, HEAD_DIM), qr.reshape(dec_b, NSA_HEADS, HEAD_DIM),
                                  ng[:, None, :], kv[:, None, :], win[:, None, :], win_t, ovl, gg, selr,
                                  expand, perm, win_buf)
        o_gla, st_s = _gla_decode(gq, gk, la, gv, state_gla, l)
        kv_s.append(kv.reshape(dec_b, 1, 4, NSA_GROUPS, HEAD_DIM))
        gla_s.append(st_s)
        h_s = _outproj(h_s, o8.reshape(dec_b, qw), o_gla, w_b, gn, w_nu, w_gu, w_o, g1, b1, alpha)
        h_s = _mlp(h_s, w1, w2, g2, b2, alpha)

    return (h_p.reshape(batch, seq, dm), h_s.reshape(dec_b, 1, dm),
            kv_big.reshape(batch, depth, seq, 4, NSA_GROUPS, HEAD_DIM), jnp.stack(win_p, axis=1),
            jnp.stack(gla_p, axis=1), jnp.stack(kv_s, axis=1), jnp.transpose(win_buf, (0, 1, 5, 2, 3, 4)),
            jnp.stack(gla_s, axis=1))
```

```python
import functools
import math

import numpy as np
import jax
import jax.numpy as jnp
from jax import lax
from jax.experimental import pallas as pl
from jax.experimental.pallas import tpu as pltpu

F32 = jnp.float32
BF16 = jnp.bfloat16

NSA_HEADS = 8
NSA_GROUPS = 2
NSA_HPG = NSA_HEADS // NSA_GROUPS
HEAD_DIM = 64
SCALE = HEAD_DIM ** -0.5
CMP_LEN = 32
CMP_STRIDE = 16
SEL_LEN = 64
N_SEL = 16
WINDOW = 512
GLA_HEADS = 4
GLA_RANK = 16
GLA_TAU = 16.0
ROPE_THETA = 10000.0
EPS = 1e-5
BIG = 1e6
NEG = -1e30
LOG2E = math.log2(math.e)

LANES = 128
ROW_TILE = 256
Q_TILE = 128
KV_TILE = 256
GLA_CHUNK = 256
GLA_DEC_TILE = 16
NSA_DEC_TILE = 4
VMEM_LIMIT = 56 * 1024 * 1024


def _cparams(n_axes):
    return pltpu.CompilerParams(dimension_semantics=("arbitrary",) * n_axes,
                                vmem_limit_bytes=VMEM_LIMIT)


def _dot(a, b):
    return jnp.dot(a, b, preferred_element_type=F32)


def _dot_nt(a, b):
    return lax.dot_general(a, b, (((1,), (1,)), ((), ())), preferred_element_type=F32)


def _split3(x):
    hi = x.astype(BF16)
    r = x - hi.astype(F32)
    mid = r.astype(BF16)
    lo = (r - mid.astype(F32)).astype(BF16)
    return hi, mid, lo


def _layer_norm(x, g, b):
    mu = jnp.mean(x, axis=-1, keepdims=True)
    xc = x - mu
    var = jnp.mean(xc * xc, axis=-1, keepdims=True)
    return xc * lax.rsqrt(var + EPS) * g + b


def _sigmoid(x):
    return 1.0 / (1.0 + jnp.exp(-x))


def _ln_kernel(x_ref, g_ref, b_ref, o_ref):
    o_ref[...] = _layer_norm(x_ref[...], g_ref[...], b_ref[...])


def _row_tile(n):
    return ROW_TILE if n % ROW_TILE == 0 else n


def _entry_norm(x, g, b):
    n, d = x.shape
    tm = _row_tile(n)
    return pl.pallas_call(
        _ln_kernel,
        name="entry_norm",
        grid=(n // tm,),
        in_specs=[pl.BlockSpec((tm, d), lambda i: (i, 0)),
                  pl.BlockSpec((1, d), lambda i: (0, 0)),
                  pl.BlockSpec((1, d), lambda i: (0, 0))],
        out_specs=pl.BlockSpec((tm, d), lambda i: (i, 0)),
        out_shape=jax.ShapeDtypeStruct((n, d), F32),
        compiler_params=_cparams(1),
    )(x, g.reshape(1, d), b.reshape(1, d))


C_Q, C_KV, C_GQ, C_GK, C_GV, C_MISC, C_END = 0, 512, 1280, 1536, 1792, 2304, 2432


def _rope128(x, cs, sn):
    lane = lax.broadcasted_iota(jnp.int32, x.shape, 1)
    first = (lane % HEAD_DIM) < (HEAD_DIM // 2)
    swapped = jnp.where(first, pltpu.roll(x, LANES - HEAD_DIM // 2, 1), pltpu.roll(x, HEAD_DIM // 2, 1))
    return x * cs + swapped * sn


def _inproj_kernel(h_ref, w_ref, cs_ref, sn_ref, wa_ref, ba_ref, *rest, with_t):
    n_out = 11 if with_t else 9
    qp_ref, qr_ref, kv_ref, win_ref, gq_ref, gk_ref, gv_ref, la_ref, ng_ref = rest[-n_out:][:9]
    kvt_ref, wint_ref = rest[-2:] if with_t else (None, None)
    hb = h_ref[...].astype(BF16)
    cs = cs_ref[...]
    sn = sn_ref[...]

    def seg(lo, hi):
        return _dot(hb, w_ref[:, lo:hi])

    for j in range(4):
        qj = seg(C_Q + j * LANES, C_Q + (j + 1) * LANES)
        qp_ref[:, j * LANES:(j + 1) * LANES] = qj
        qr_ref[:, j * LANES:(j + 1) * LANES] = _rope128(qj, cs, sn)
    for s in range(6):
        x = seg(C_KV + s * LANES, C_KV + (s + 1) * LANES)
        if s in (2, 4):
            x = _rope128(x, cs, sn)
        if s < 4:
            kv_ref[:, s * LANES:(s + 1) * LANES] = x
        else:
            win_ref[:, (s - 4) * LANES:(s - 3) * LANES] = x
        if with_t:
            xt = x.T.reshape(NSA_GROUPS, HEAD_DIM, x.shape[0])
            if s < 4:
                kvt_ref[s] = xt
            else:
                wint_ref[s - 4] = xt
    gq_ref[...] = seg(C_GQ, C_GK) * (HEAD_DIM ** -0.5)
    gk_ref[...] = seg(C_GK, C_GV)
    gv_ref[...] = seg(C_GV, C_MISC)
    misc = seg(C_MISC, C_END)
    ng_ref[...] = misc
    x = _dot(misc.astype(BF16), wa_ref[...]) + ba_ref[...]
    la_ref[...] = (jnp.minimum(x, 0.0) - jnp.log(1.0 + jnp.exp(-jnp.abs(x)))) * (1.0 / GLA_TAU)


def _inproj(h, w_a, cs_tab, sn_tab, wa_pad, ba, t_layout=None, t_prev=None):
    n, d = h.shape
    tm = _row_tile(n)
    tab_tiles = cs_tab.shape[0] // tm

    def tab_map(i):
        return (i % tab_tiles, 0)

    row = lambda w: pl.BlockSpec((tm, w), lambda i: (i, 0))
    full = lambda a: pl.BlockSpec(a.shape, lambda i: (0,) * a.ndim)
    widths = (512, 512, 512, 256, 256, 256, 512, 256, 128)
    out_specs = [row(w) for w in widths]
    out_shape = [jax.ShapeDtypeStruct((n, w), F32) for w in widths]
    in_specs = [row(d), full(w_a), pl.BlockSpec((tm, LANES), tab_map), pl.BlockSpec((tm, LANES), tab_map),
                full(wa_pad), full(ba)]
    args = [h, w_a, cs_tab, sn_tab, wa_pad, ba]
    aliases = {}
    if t_layout is not None:
        layer, depth = t_layout
        batch = n // cs_tab.shape[0]
        for n_slots in (4, 2):
            out_specs.append(pl.BlockSpec((None, None, n_slots, NSA_GROUPS, HEAD_DIM, tm),
                                          lambda i: (i // tab_tiles, layer, 0, 0, 0, i % tab_tiles)))
            out_shape.append(jax.ShapeDtypeStruct((batch, depth, n_slots, NSA_GROUPS, HEAD_DIM, cs_tab.shape[0]), F32))
        if t_prev is not None:
            for k, buf in enumerate(t_prev):
                in_specs.append(pl.BlockSpec(memory_space=pl.ANY))
                aliases[len(args)] = len(widths) + k
                args.append(buf)
    return pl.pallas_call(
        functools.partial(_inproj_kernel, with_t=t_layout is not None),
        name="inproj",
        grid=(n // tm,),
        in_specs=in_specs,
        out_specs=out_specs,
        out_shape=out_shape,
        input_output_aliases=aliases,
        compiler_params=_cparams(1),
    )(*args)


def _gelu_tanh(x):
    return 0.5 * x * (1.0 + jnp.tanh(math.sqrt(2.0 / math.pi) * (x + 0.044715 * x * x * x)))


def _compress_rows(chunk_pair, n_chunks, wc_ref, pe_ref, w2_ref, seq_chunks=None):
    seq_chunks = n_chunks if seq_chunks is None else seq_chunks
    outs = []
    for s in range(2):
        acc = [jnp.zeros((n_chunks, 2 * LANES), F32) for _ in range(NSA_GROUPS)]
        for lp in range(CMP_STRIDE // 2):
            a = chunk_pair(s, lp).astype(BF16)
            for g in range(NSA_GROUPS):
                acc[g] = acc[g] + _dot(a, wc_ref[s, g, lp])
        row = lax.broadcasted_iota(jnp.int32, (n_chunks, LANES), 0)
        parts = []
        for g in range(NSA_GROUPS):
            hid = acc[g][:, :LANES] + pltpu.roll(acc[g][:, LANES:], n_chunks - 1, 0) + pe_ref[s]
            parts.append(_dot(_gelu_tanh(hid).astype(BF16), w2_ref[s]))
        out = jnp.concatenate(parts, axis=1)
        outs.append(jnp.where(row % seq_chunks < seq_chunks - 1, out, 0.0))
    return outs


def _pe_bias_kernel(pe_ref, w1_ref, o_ref):
    for s in range(2):
        o_ref[s] = _dot(pe_ref[s].astype(BF16), w1_ref[s].astype(BF16))[0:1]


def _pe_bias(pe_flat, w1):
    hid = w1.shape[2]
    return pl.pallas_call(
        _pe_bias_kernel,
        name="pe_bias",
        out_shape=jax.ShapeDtypeStruct((2, 1, hid), F32),
    )(pe_flat, w1)


def _compress_kernel(xk_ref, xv_ref, wc_ref, pe_ref, w2_ref, kc_ref, vc_ref):
    x_refs = (xk_ref, xv_ref)
    n_chunks = xk_ref.shape[0] // CMP_STRIDE

    def chunk_pair(slot, lp):
        return jnp.concatenate([x_refs[slot][pl.ds(2 * lp, n_chunks, stride=CMP_STRIDE), :],
                                x_refs[slot][pl.ds(2 * lp + 1, n_chunks, stride=CMP_STRIDE), :]], axis=1)

    kc, vc = _compress_rows(chunk_pair, n_chunks, wc_ref, pe_ref, w2_ref)
    kc_ref[0] = kc
    vc_ref[0] = vc


def _compress_prompt(kv, wc, pe_pair, w2, batch, seq):
    nc = seq // CMP_STRIDE
    full = lambda a: pl.BlockSpec(a.shape, lambda b: (0,) * a.ndim)
    return pl.pallas_call(
        _compress_kernel,
        name="compress",
        grid=(batch,),
        in_specs=[pl.BlockSpec((seq, LANES), lambda b: (b, 0)), pl.BlockSpec((seq, LANES), lambda b: (b, 1)),
                  full(wc), full(pe_pair), full(w2)],
        out_specs=[pl.BlockSpec((1, nc, LANES), lambda b: (b, 0, 0))] * 2,
        out_shape=[jax.ShapeDtypeStruct((batch, nc, LANES), F32)] * 2,
        compiler_params=_cparams(1),
    )(kv, kv, wc, pe_pair, w2)


def _select_mask(imp_t, qpos, n_blocks):
    nbp, nq = imp_t.shape
    blk = lax.broadcasted_iota(jnp.int32, (nbp, nq), 0)
    cur = qpos // SEL_LEN
    causal = blk * SEL_LEN <= qpos
    forced = (blk == 0) | (blk == cur) | (blk == cur - 1)
    score = jnp.where(forced, BIG, jnp.where(causal, imp_t, -jnp.inf))
    score = jnp.where(blk < n_blocks, score, -jnp.inf)
    rank = jnp.zeros((nbp, nq), jnp.int32)
    for j in range(n_blocks):
        other = score[j:j + 1, :]
        rank = rank + jnp.where(blk > j, jnp.where(other >= score, 1, 0), jnp.where(other > score, 1, 0))
    return (rank < N_SEL) & (score > -jnp.inf)


V_ROWS = LANES + 16


def _softmax_step(state, s, pv_fn):
    m, acc = state
    m_new = jnp.maximum(m, jnp.max(s, axis=0, keepdims=True))
    alpha = jnp.exp2(m - m_new)
    p = jnp.exp2(s - m_new)
    return m_new, alpha * acc + pv_fn(p.astype(BF16))


def _softmax_init(nq):
    return (jnp.full((1, nq), NEG, F32), jnp.zeros((V_ROWS, nq), F32))


def _nsa_prompt_kernel(qpa_ref, qra_ref, nga_ref, qpb_ref, qrb_ref, ngb_ref, kc_ref, vc_ref, ks_ref, vs_ref,
                       kw_ref, vw_ref, ovl_ref, oa_ref, ob_ref, ksa_ref, vst_ref, kwa_ref, vwt_ref,
                       *, seq, n_blocks):
    g = pl.program_id(1)
    i = pl.program_id(2)
    tq = Q_TILE
    nqt = seq // tq
    n_kt = seq // KV_TILE
    nq = NSA_HPG * tq

    @pl.when((g == 0) & (i == 0))
    def _():
        for kt in range(n_kt):
            rows = pl.ds(kt * KV_TILE, KV_TILE)
            key = kt * KV_TILE + lax.broadcasted_iota(jnp.int32, (KV_TILE, LANES), 0)
            lane = lax.broadcasted_iota(jnp.int32, (KV_TILE, LANES), 1)
            onehot = jnp.where(key // SEL_LEN == lane, 1.0, 0.0).astype(BF16)
            ksa_ref[kt] = jnp.concatenate([ks_ref[rows, :].astype(BF16), onehot], axis=1)
            kwa_ref[kt] = kw_ref[rows, :].astype(BF16)
            cols = slice(kt * KV_TILE, (kt + 1) * KV_TILE)
            ones = jnp.ones((V_ROWS - LANES, KV_TILE), BF16)
            vst_ref[kt] = jnp.concatenate([vs_ref[:, :, cols].reshape(LANES, KV_TILE).astype(BF16), ones], axis=0)
            vwt_ref[kt] = jnp.concatenate([vw_ref[:, :, cols].reshape(LANES, KV_TILE).astype(BF16), ones], axis=0)

    lane_q = lax.broadcasted_iota(jnp.int32, (1, nq), 1) % tq
    lane1 = lax.broadcasted_iota(jnp.int32, (1, tq), 1)
    lane = lax.broadcasted_iota(jnp.int32, (tq, LANES), 1)
    own = (lane // HEAD_DIM) == g
    ovl = ovl_ref[...]
    kc = kc_ref[0].astype(BF16)
    vct = vc_ref[0].T.astype(BF16)
    nc = kc.shape[0]
    nbp = -(-n_blocks // 8) * 8

    def own_rows(x):
        return jnp.where(g == 0, x[:HEAD_DIM], x[HEAD_DIM:])

    def prepare(qp_ref, qr_ref, qs):
        def stack_heads(ref, scale):
            parts = []
            for h in range(NSA_HPG):
                qh = ref[:, h * HEAD_DIM:(h + 1) * HEAD_DIM] * scale
                parts.append(jnp.where(own, jnp.concatenate([qh, qh], axis=1), 0.0))
            return parts

        qpos = qs + lane_q
        qp = jnp.concatenate(stack_heads(qp_ref, SCALE), axis=0).astype(BF16)
        s = _dot_nt(kc, qp)
        cblk = lax.broadcasted_iota(jnp.int32, (nc, nq), 0)
        valid = cblk * CMP_STRIDE + (CMP_LEN - 1) <= qpos
        s = jnp.where(valid, s, NEG)
        m = jnp.max(s, axis=0, keepdims=True)
        p = jnp.where(valid, jnp.exp(s - m), 0.0)
        p = p / jnp.maximum(jnp.sum(p, axis=0, keepdims=True), 1e-30)
        o_cmp = own_rows(_dot(vct, p.astype(BF16)))
        psum = p[:, 0:tq]
        for h in range(1, NSA_HPG):
            psum = psum + p[:, h * tq:(h + 1) * tq]
        imp_t = sum(_dot(ovl, piece) for piece in _split3(psum))
        sel = _select_mask(imp_t[:nbp], qs + lane1, n_blocks)
        bias_t = jnp.where(sel, 0.0, NEG)
        bias = jnp.concatenate([bias_t, jnp.zeros((LANES - nbp, tq), F32)], axis=0).T
        qr_parts = stack_heads(qr_ref, SCALE * LOG2E)
        q_sel = jnp.concatenate([jnp.concatenate([q, bias], axis=1) for q in qr_parts], axis=0).astype(BF16)
        q_win = jnp.concatenate(qr_parts, axis=0).astype(BF16)
        return qpos, o_cmp, q_sel, q_win

    qs_a = i * tq
    qs_b = (nqt - 1 - i) * tq
    qpos_a, o_cmp_a, q_sel_a, q_win_a = prepare(qpa_ref, qra_ref, qs_a)
    qpos_b, o_cmp_b, q_sel_b, q_win_b = prepare(qpb_ref, qrb_ref, qs_b)

    kd_a = qs_a // KV_TILE
    kd_b = qs_b // KV_TILE
    max_kd_a = ((nqt // 2 - 1) * tq) // KV_TILE
    n_sel = n_kt + 1
    n_win = WINDOW // KV_TILE + 1

    krow = lax.broadcasted_iota(jnp.int32, (KV_TILE, nq), 0)

    def sel_task(j):
        if j == 0:
            return 0, q_sel_a, qpos_a, True
        if j > max_kd_a:
            return j - kd_a - 1, q_sel_b, qpos_b, j == n_sel - 1
        in_a = j <= kd_a
        return (jnp.where(in_a, j, j - kd_a - 1), jnp.where(in_a, q_sel_a, q_sel_b),
                jnp.where(in_a, qpos_a, qpos_b), True)

    tasks = []
    for j in range(n_sel):
        kt, q, qpos, masked = sel_task(j)
        valid = (kt * KV_TILE + krow <= qpos) if masked else None
        tasks.append(("sel", j, ksa_ref, vst_ref, kt, q, valid))
    for name, kd, q, qpos in (("wa", kd_a, q_win_a, qpos_a), ("wb", kd_b, q_win_b, qpos_b)):
        for j in range(n_win):
            kt = kd - (n_win - 1) + j
            d = qpos - (kt * KV_TILE + krow)
            valid = (d >= 0) & (d <= WINDOW) & (kt >= 0)
            tasks.append((name, j, kwa_ref, vwt_ref, jnp.maximum(kt, 0), q, valid))

    def scores(task):
        _, _, k_ref, _, kt, q, _ = task
        return _dot_nt(k_ref[kt], q)

    sel_tasks = tasks[:n_sel]
    win_tasks = tasks[n_sel:]
    tasks = []
    while sel_tasks or win_tasks:
        if sel_tasks:
            tasks.append(sel_tasks.pop(0))
        if win_tasks:
            tasks.append(win_tasks.pop(0))

    results = {}
    states = {}
    s_next = scores(tasks[0])
    for t, task in enumerate(tasks):
        name, j, _, vt_ref, kt, _, mask = task
        s = s_next
        if t + 1 < len(tasks):
            s_next = scores(tasks[t + 1])
        if j == 0:
            states[name] = _softmax_init(nq)
        state = states[name]
        if name == "sel" and 1 <= j <= max_kd_a + 1:
            switch = j == kd_a + 1
            prev = results.get("sa", state)
            results["sa"] = tuple(jnp.where(switch, x, y) for x, y in zip(state, prev))
            state = tuple(jnp.where(switch, x, y) for x, y in zip(_softmax_init(nq), state))
        if mask is not None:
            s = jnp.where(mask, s, NEG)
        state = _softmax_step(state, s, lambda p: _dot(vt_ref[kt], p))
        states[name] = state
        last = (name == "sel" and j == n_sel - 1) or (name != "sel" and j == n_win - 1)
        if last:
            results["sb" if name == "sel" else name] = state

    def finish(key):
        _, acc = results[key]
        return own_rows(acc[:LANES] / acc[LANES:LANES + 1])

    for ng_ref, o_ref, branches in ((nga_ref, oa_ref, (o_cmp_a, finish("sa"), finish("wa"))),
                                    (ngb_ref, ob_ref, (o_cmp_b, finish("sb"), finish("wb")))):
        ng_t = ng_ref[...].T
        outs = []
        for h in range(NSA_HPG):
            cols = slice(h * tq, (h + 1) * tq)
            tot = jnp.zeros((HEAD_DIM, tq), F32)
            for r, o in enumerate(branches):
                i0 = h * 3 + r
                i1 = (NSA_HPG + h) * 3 + r
                gate = _sigmoid(jnp.where(g == 0, ng_t[i0:i0 + 1], ng_t[i1:i1 + 1]))
                tot = tot + gate * o[:, cols]
            outs.append(tot.T)
        o_ref[...] = jnp.concatenate(outs, axis=1)


def _nsa_prompt(qp, qr, ng, kc, vc, kv, win, kvt, wint, ovl, batch, seq, layer):
    n_blocks = -(-seq // SEL_LEN)
    nqt = seq // Q_TILE
    n_kt = seq // KV_TILE
    assert nqt % 2 == 0 and KV_TILE == 2 * Q_TILE and WINDOW % KV_TILE == 0
    gw = NSA_HPG * HEAD_DIM
    half = nqt // 2
    qa = pl.BlockSpec((Q_TILE, gw), lambda b, g, i: (b * nqt + i, g))
    qb = pl.BlockSpec((Q_TILE, gw), lambda b, g, i: (b * nqt + nqt - 1 - i, g))
    na = pl.BlockSpec((Q_TILE, LANES), lambda b, g, i: (b * nqt + i, 0))
    nb = pl.BlockSpec((Q_TILE, LANES), lambda b, g, i: (b * nqt + nqt - 1 - i, 0))
    cspec = pl.BlockSpec((1,) + kc.shape[1:], lambda b, g, i: (b, 0, 0))
    ospec = pl.BlockSpec((Q_TILE, gw), lambda b, g, i: (b * half + i, g))

    def rowspec(col):
        return pl.BlockSpec((seq, LANES), lambda b, g, i: (b, col))

    def tspec(slot):
        return pl.BlockSpec((None, None, None, NSA_GROUPS, HEAD_DIM, seq), lambda b, g, i: (b, layer, slot, 0, 0, 0))

    oshape = jax.ShapeDtypeStruct((batch * half * Q_TILE, qp.shape[1]), F32)
    return pl.pallas_call(
        functools.partial(_nsa_prompt_kernel, seq=seq, n_blocks=n_blocks),
        name="nsa_prompt",
        grid=(batch, NSA_GROUPS, half),
        in_specs=[qa, qa, na, qb, qb, nb, cspec, cspec, rowspec(2), tspec(3), rowspec(0), tspec(1),
                  pl.BlockSpec(ovl.shape, lambda b, g, i: (0, 0))],
        out_specs=[ospec, ospec],
        out_shape=[oshape, oshape],
        scratch_shapes=[pltpu.VMEM((n_kt, KV_TILE, 2 * LANES), BF16), pltpu.VMEM((n_kt, V_ROWS, KV_TILE), BF16),
                        pltpu.VMEM((n_kt, KV_TILE, LANES), BF16), pltpu.VMEM((n_kt, V_ROWS, KV_TILE), BF16)],
        compiler_params=_cparams(3),
    )(qp, qr, ng, qp, qr, ng, kc, vc, kv, kvt, win, wint, ovl)


def _gla_level_matrix(c):
    t = np.arange(c)[:, None]
    u = np.arange(c)[None, :]
    mats = [(u <= t), (u > t)]
    m = c
    while m >= 2:
        split = (t // m) * m + m // 2
        upper = (t % m) >= m // 2
        mats.append(np.where(upper, (u >= split) & (u <= t), (u > t) & (u < split)))
        m //= 2
    return np.concatenate(mats, axis=0).astype(np.float32)


def _gla_prompt_kernel(q_ref, k_ref, v_ref, la_ref, w_ref, o_ref, st_ref, e_ref, s_ref, *, seq):
    c = GLA_CHUNK
    n_levels = int(math.log2(c))
    s_ref[...] = jnp.zeros_like(s_ref)
    row = lax.broadcasted_iota(jnp.int32, (c, c), 0)
    col = lax.broadcasted_iota(jnp.int32, (c, c), 1)
    rowl = lax.broadcasted_iota(jnp.int32, (c, LANES), 0)
    lanel = lax.broadcasted_iota(jnp.int32, (c, LANES), 1)

    def chunk(ci, _):
        rows = pl.ds(pl.multiple_of(ci * c, c), c)
        la = la_ref[rows, :]
        hi = la.astype(BF16)
        lo = (la - hi.astype(F32)).astype(BF16)
        e_ref[...] = jnp.exp(_dot(w_ref[...], hi) + _dot(w_ref[...], lo))
        for pair in range(GLA_HEADS // 2):
            lanes = slice(pair * LANES, (pair + 1) * LANES)
            q = q_ref[rows, lanes]
            k = k_ref[rows, lanes]
            q0 = (q * e_ref[0:c, lanes]).astype(BF16)
            kdec = k * e_ref[c:2 * c, lanes]
            a_last = e_ref[c - 1:c, lanes]
            qls, kls = [], []
            for lv in range(n_levels):
                m = c >> lv
                x = e_ref[(2 + lv) * c:(3 + lv) * c, lanes]
                upper = (rowl % m) >= (m // 2)
                qls.append(jnp.where(upper, q * x, 0.0).astype(BF16))
                kls.append(jnp.where(upper, 0.0, k * x))
            for hh in range(2):
                head = pair * 2 + hh
                mine = (lanel // HEAD_DIM) == hh
                a = jnp.where(row == col, _dot_nt(q.astype(BF16), jnp.where(mine, k, 0.0).astype(BF16)), 0.0)
                for lv in range(n_levels):
                    m = c >> lv
                    same = (row // m) == (col // m)
                    a = a + jnp.where(same, _dot_nt(qls[lv], jnp.where(mine, kls[lv], 0.0).astype(BF16)), 0.0)
                v = v_ref[rows, head * LANES:(head + 1) * LANES]
                st = s_ref[head]
                o = _dot(a.astype(BF16), v.astype(BF16)) + _dot_nt(q0, st.astype(BF16))
                o_ref[rows, head * LANES:(head + 1) * LANES] = o
                kd = jnp.where(mine, kdec, 0.0).astype(BF16)
                s_ref[head] = st * a_last + _dot(v.T.astype(BF16), kd)
        return 0

    lax.fori_loop(0, seq // c, chunk, 0)
    for head in range(GLA_HEADS):
        st = s_ref[head].T
        off = (head % 2) * HEAD_DIM
        st_ref[0, head] = st[off:off + HEAD_DIM]


def _gla_prompt(gq, gk, gv, la, wlev, batch, seq):
    dk2 = gq.shape[1]
    dv4 = gv.shape[1]
    return pl.pallas_call(
        functools.partial(_gla_prompt_kernel, seq=seq),
        name="gla_prompt",
        grid=(batch,),
        in_specs=[pl.BlockSpec((seq, dk2), lambda b: (b, 0)), pl.BlockSpec((seq, dk2), lambda b: (b, 0)),
                  pl.BlockSpec((seq, dv4), lambda b: (b, 0)), pl.BlockSpec((seq, dk2), lambda b: (b, 0)),
                  pl.BlockSpec(wlev.shape, lambda b: (0, 0))],
        out_specs=[pl.BlockSpec((seq, dv4), lambda b: (b, 0)),
                   pl.BlockSpec((1, GLA_HEADS, HEAD_DIM, LANES), lambda b: (b, 0, 0, 0))],
        out_shape=[jax.ShapeDtypeStruct((gq.shape[0], dv4), F32),
                   jax.ShapeDtypeStruct((batch, GLA_HEADS, HEAD_DIM, LANES), F32)],
        scratch_shapes=[pltpu.VMEM((wlev.shape[0], dk2), F32), pltpu.VMEM((GLA_HEADS, LANES, LANES), F32)],
        compiler_params=_cparams(1),
    )(gq, gk, gv, la, wlev)


def _outproj_kernel(h_ref, on_ref, og_ref, wb_ref, gn_ref, wn_ref, wg_ref, wo_ref, g1_ref, b1_ref, o_ref,
                    *, alpha):
    h = h_ref[...]
    hb = h.astype(BF16)
    dm = h.shape[1]
    gw = og_ref.shape[1]
    g_r = _dot(hb, wb_ref[:, :gw])
    parts = []
    for head in range(GLA_HEADS):
        x = og_ref[:, head * LANES:(head + 1) * LANES]
        x = x * lax.rsqrt(jnp.mean(x * x, axis=-1, keepdims=True) + EPS) * gn_ref[...]
        gr = g_r[:, head * LANES:(head + 1) * LANES]
        parts.append(x * (gr * _sigmoid(gr)))
    og = jnp.concatenate(parts, axis=1).astype(BF16)
    a = _dot(on_ref[...].astype(BF16), wn_ref[...])
    c = _dot(og, wg_ref[...])
    m_a = _dot(hb, wb_ref[:, gw:gw + dm])
    m_c = _dot(hb, wb_ref[:, gw + dm:gw + 2 * dm])
    mix = (_sigmoid(m_a) * a + _sigmoid(m_c) * c).astype(BF16)
    y = _dot(mix, wo_ref[...])
    o_ref[...] = _layer_norm(alpha * h + y, g1_ref[...], b1_ref[...])


def _outproj(h, o_nsa, o_gla, w_b, gn, w_nsa, w_gla, w_out, g1, b1, alpha):
    n, d = h.shape
    tm = _row_tile(n)
    row = lambda w: pl.BlockSpec((tm, w), lambda i: (i, 0))
    full = lambda a: pl.BlockSpec(a.shape, lambda i: (0,) * a.ndim)
    return pl.pallas_call(
        functools.partial(_outproj_kernel, alpha=alpha),
        name="outproj",
        grid=(n // tm,),
        in_specs=[row(d), row(o_nsa.shape[1]), row(o_gla.shape[1]), full(w_b), full(gn), full(w_nsa),
                  full(w_gla), full(w_out), full(g1), full(b1)],
        out_specs=row(d),
        out_shape=jax.ShapeDtypeStruct((n, d), F32),
        compiler_params=_cparams(1),
    )(h, o_nsa, o_gla, w_b, gn, w_nsa, w_gla, w_out, g1, b1)


def _mlp_kernel(h_ref, w1_ref, w2_ref, g_ref, b_ref, o_ref, *, alpha):
    h = h_ref[...]
    hb = h.astype(BF16)
    dff = w1_ref.shape[1]
    step = 1024
    f = jnp.zeros(h.shape, F32)
    for c0 in range(0, dff, step):
        u = jnp.maximum(_dot(hb, w1_ref[:, c0:c0 + step]), 0.0)
        f = f + _dot((u * u).astype(BF16), w2_ref[c0:c0 + step, :])
    o_ref[...] = _layer_norm(alpha * h + f, g_ref[...], b_ref[...])


def _mlp(h, w1, w2, g, b, alpha):
    n, d = h.shape
    tm = _row_tile(n)
    row = pl.BlockSpec((tm, d), lambda i: (i, 0))
    full = lambda a: pl.BlockSpec(a.shape, lambda i: (0,) * a.ndim)
    return pl.pallas_call(
        functools.partial(_mlp_kernel, alpha=alpha),
        name="mlp",
        grid=(n // tm,),
        in_specs=[row, full(w1), full(w2), full(g), full(b)],
        out_specs=row,
        out_shape=jax.ShapeDtypeStruct((n, d), F32),
        compiler_params=_cparams(1),
    )(h, w1, w2, g, b)


def _softmax_rows(parts, extra, valid_extra=None):
    m = extra
    for s in parts:
        m = jnp.maximum(m, jnp.max(s, axis=-1, keepdims=True))
    ps = [jnp.exp(s - m) for s in parts]
    pe = jnp.exp(extra - m)
    tot = pe
    for p in ps:
        tot = tot + jnp.sum(p, axis=-1, keepdims=True)
    inv = 1.0 / tot
    return [p * inv for p in ps], pe * inv


def _nsa_decode_kernel(pt_ref, *refs, n_pages, n_blocks, past, has_prev):
    del pt_ref
    nb = NSA_DEC_TILE
    pages_all = [refs[bb * n_pages:(bb + 1) * n_pages] for bb in range(nb)]
    rest = refs[nb * n_pages:]
    (wc_ref, pe_ref, w2_ref, qp_ref, qr_ref, ng_ref, kvn_ref, winn_ref, wb_ref, ovl_ref, gg_ref, selr_ref,
     exp_ref, perm_ref) = rest[:14]
    o_ref, wo_ref = rest[-2:]
    page = pages_all[0][0].shape[-1]
    wbuf = wb_ref.shape[-1]
    n_rows = qp_ref.shape[1]
    seq_chunks = past // CMP_STRIDE
    n_chunks = nb * seq_chunks
    cpp = page // CMP_STRIDE
    n_lp = CMP_STRIDE // 2

    def chunk_pairs(slot):
        gathered = []
        for pages in pages_all:
            for pg in pages:
                xt = pg[slot].reshape(NSA_GROUPS * HEAD_DIM, page).astype(BF16)
                gathered.append([_dot_nt(perm_ref[par], xt) for par in range(2)])
        return [jnp.concatenate(
            [jnp.concatenate([even[lp * cpp:(lp + 1) * cpp], odd[lp * cpp:(lp + 1) * cpp]], axis=1)
             for even, odd in gathered], axis=0) for lp in range(n_lp)]

    pairs = [chunk_pairs(slot) for slot in range(2)]
    kc_all, vc_all = _compress_rows(lambda slot, lp: pairs[slot][lp], n_chunks, wc_ref, pe_ref, w2_ref, seq_chunks)
    streams = [_nsa_decode_one(bb, kc_all[bb * seq_chunks:(bb + 1) * seq_chunks],
                               vc_all[bb * seq_chunks:(bb + 1) * seq_chunks], pages_all[bb], qp_ref, qr_ref, ng_ref,
                               kvn_ref, winn_ref, wb_ref, ovl_ref, gg_ref, selr_ref, exp_ref, o_ref, wo_ref,
                               n_blocks, past) for bb in range(nb)]
    while streams:
        streams = [st for st in streams if next(st, "done") is None]


def _nsa_decode_one(bb, kc, vc, sel_pages, qp_ref, qr_ref, ng_ref, kvn_ref, winn_ref, wb_ref, ovl_ref, gg_ref,
                    selr_ref, exp_ref, o_ref, wo_ref, n_blocks, past):
    page = sel_pages[0].shape[-1]
    wbuf = wb_ref.shape[-1]
    n_rows = qp_ref.shape[1]
    row8 = lax.broadcasted_iota(jnp.int32, (n_rows, LANES), 0)
    lane8 = lax.broadcasted_iota(jnp.int32, (n_rows, LANES), 1)
    own = (lane8 // HEAD_DIM) == (row8 // NSA_HPG)

    def by_group(x0, x1):
        return jnp.where(lax.broadcasted_iota(jnp.int32, x0.shape, 0) < NSA_HPG, x0, x1)

    qp = qp_ref[bb] * SCALE
    qr = qr_ref[bb] * SCALE
    qrb = qr.astype(BF16)
    kvn = kvn_ref[bb]
    winn = winn_ref[bb]

    def new_key_scores(krow):
        prod = jnp.concatenate([qr, qr], axis=1) * krow
        return jnp.sum(jnp.where(own, prod, 0.0), axis=-1, keepdims=True)

    def new_value(vrow):
        v = jnp.broadcast_to(vrow, (n_rows, LANES))
        return by_group(v[:, :HEAD_DIM], v[:, HEAD_DIM:])

    nc = kc.shape[0]
    qp_pair = jnp.where(own, jnp.concatenate([qp, qp], axis=1), 0.0)
    s_cmp = _dot_nt(qp_pair.astype(BF16), kc.astype(BF16))
    s_sel = [by_group(_dot(qrb, pg[2, 0].astype(BF16)), _dot(qrb, pg[2, 1].astype(BF16))) for pg in sel_pages]
    s_win = by_group(_dot(qrb, wb_ref[bb, 0, 0].astype(BF16)), _dot(qrb, wb_ref[bb, 0, 1].astype(BF16)))
    pieces = _split3(jnp.broadcast_to(_sigmoid(ng_ref[bb]), (LANES, LANES)))
    gates = [sum(_dot_nt(selr_ref[r], piece) for piece in pieces)[:, :HEAD_DIM] for r in range(3)]
    yield

    cblk = lax.broadcasted_iota(jnp.int32, (n_rows, nc), 1)
    valid = cblk * CMP_STRIDE + (CMP_LEN - 1) <= past
    s = jnp.where(valid, s_cmp, NEG)
    p = jnp.where(valid, jnp.exp(s - jnp.max(s, axis=-1, keepdims=True)), 0.0)
    p = p / jnp.maximum(jnp.sum(p, axis=-1, keepdims=True), 1e-30)
    o = _dot(p.astype(BF16), vc.astype(BF16))
    imp_h = sum(_dot_nt(piece, ovl_ref[...]) for piece in _split3(p))
    ps, p_new_w = _softmax_rows([s_win], new_key_scores(winn[:, 0:LANES]))
    pb = ps[0].astype(BF16)
    o_win = by_group(_dot_nt(pb, wb_ref[bb, 1, 0].astype(BF16)), _dot_nt(pb, wb_ref[bb, 1, 1].astype(BF16)))
    yield
    o_cmp = by_group(o[:, :HEAD_DIM], o[:, HEAD_DIM:])
    o_win = o_win + p_new_w * new_value(winn[:, LANES:2 * LANES])
    imp_h = jnp.concatenate([imp_h, jnp.zeros((LANES - n_rows, LANES), F32)], axis=0)
    imp = sum(_dot(gg_ref[...], piece) for piece in _split3(imp_h))
    yield

    nbp = -(-n_blocks // 8) * 8
    sel = _select_mask(imp.T[:nbp], jnp.full((1, LANES), past, jnp.int32), n_blocks)
    bias_t = jnp.where(sel, 0.0, NEG)
    bias_t = jnp.concatenate([bias_t, jnp.zeros((LANES - nbp, LANES), F32)], axis=0)
    bias = bias_t.T[:n_rows]
    bias_keys = _dot(bias.astype(BF16), exp_ref[...])
    yield

    parts = [s + bias_keys[:, pi * page:(pi + 1) * page] for pi, s in enumerate(s_sel)]
    blk_new = past // SEL_LEN
    s_new = new_key_scores(kvn[:, 2 * LANES:3 * LANES]) + bias[:, blk_new:blk_new + 1]
    ps, p_new = _softmax_rows(parts, s_new)
    acc0 = jnp.zeros((n_rows, HEAD_DIM), F32)
    acc1 = jnp.zeros((n_rows, HEAD_DIM), F32)
    for pg, p in zip(sel_pages, ps):
        pb = p.astype(BF16)
        acc0 = acc0 + _dot_nt(pb, pg[3, 0].astype(BF16))
        acc1 = acc1 + _dot_nt(pb, pg[3, 1].astype(BF16))
    yield
    o_sel = by_group(acc0, acc1) + p_new * new_value(kvn[:, 3 * LANES:4 * LANES])
    o_ref[bb] = gates[0] * o_cmp + gates[1] * o_sel + gates[2] * o_win

    lane_w = lax.broadcasted_iota(jnp.int32, (HEAD_DIM, LANES), 1)
    n_col = wbuf // LANES
    for kv in range(2):
        tile = jnp.concatenate([winn[:, kv * LANES:(kv + 1) * LANES], jnp.zeros((LANES - 1, LANES), F32)], axis=0)
        new_t = tile.T
        for grp in range(NSA_GROUPS):
            col = new_t[grp * HEAD_DIM:(grp + 1) * HEAD_DIM, 0:1]
            rolled = [pltpu.roll(wb_ref[bb, kv, grp, :, c * LANES:(c + 1) * LANES], LANES - 1, 1)
                      for c in range(n_col)]
            for c in range(n_col):
                nxt = rolled[c + 1] if c + 1 < n_col else jnp.broadcast_to(col, (HEAD_DIM, LANES))
                wo_ref[bb, kv, grp, :, c * LANES:(c + 1) * LANES] = jnp.where(lane_w == LANES - 1, nxt, rolled[c])


def _nsa_decode(page_table, cache_t, layer, wc, pe_pair, w2, qp8, qr8, ng3, kvn3, winn3, win_t, ovl, gg,
                selr, expand, perm, win_prev):
    dec_b, n_pages = page_table.shape
    page = cache_t.shape[-1]
    past = n_pages * page
    wbuf = win_t.shape[-1]
    depth = win_t.shape[1]
    assert wbuf <= WINDOW and wbuf % LANES == 0
    n_blocks = -(-(past + 1) // SEL_LEN)
    nb = NSA_DEC_TILE
    assert dec_b % nb == 0
    full = lambda a: pl.BlockSpec(a.shape, lambda b, pt: (0,) * a.ndim)
    per_b = lambda a: pl.BlockSpec((nb,) + a.shape[1:], lambda b, pt: (b,) + (0,) * (a.ndim - 1))
    wshape = (nb, None, 2, NSA_GROUPS, HEAD_DIM, wbuf)

    def page_spec(bb, p):
        return pl.BlockSpec((None, None) + cache_t.shape[2:], lambda b, pt: (pt[b * nb + bb, p], layer, 0, 0, 0, 0))

    in_specs = [page_spec(bb, p) for bb in range(nb) for p in range(n_pages)] + [
        full(wc), full(pe_pair), full(w2), per_b(qp8), per_b(qr8), per_b(ng3), per_b(kvn3), per_b(winn3),
        pl.BlockSpec(wshape, lambda b, pt: (b, layer, 0, 0, 0, 0)), full(ovl), full(gg), full(selr), full(expand),
        full(perm)]
    args = [page_table] + [cache_t] * (nb * n_pages) + [
        wc, pe_pair, w2, qp8, qr8, ng3, kvn3, winn3, win_t, ovl, gg, selr, expand, perm]
    aliases = {}
    if win_prev is not None:
        in_specs.append(pl.BlockSpec(memory_space=pl.ANY))
        aliases = {len(args): 1}
        args.append(win_prev)
    grid_spec = pltpu.PrefetchScalarGridSpec(
        num_scalar_prefetch=1,
        grid=(dec_b // nb,),
        in_specs=in_specs,
        out_specs=[per_b(qp8), pl.BlockSpec(wshape, lambda b, pt: (b, layer, 0, 0, 0, 0))])
    return pl.pallas_call(
        functools.partial(_nsa_decode_kernel, n_pages=n_pages, n_blocks=n_blocks, past=past,
                          has_prev=win_prev is not None),
        name="nsa_decode",
        grid_spec=grid_spec,
        out_shape=[jax.ShapeDtypeStruct(qp8.shape, F32),
                   jax.ShapeDtypeStruct((dec_b, depth, 2, NSA_GROUPS, HEAD_DIM, wbuf), F32)],
        input_output_aliases=aliases,
        compiler_params=_cparams(1),
    )(*args)


def _gla_decode_kernel(q_ref, k_ref, la_ref, v_ref, s_ref, o_ref, so_ref, qt_ref, kt_ref, at_ref):
    i = pl.program_id(0)
    bt = GLA_DEC_TILE
    n_tiles = qt_ref.shape[0]

    @pl.when(i == 0)
    def _():
        qt = q_ref[...].T
        kt = k_ref[...].T
        at = jnp.exp(la_ref[...]).T
        for j in range(n_tiles):
            qt_ref[j] = qt[:, j * bt:(j + 1) * bt]
            kt_ref[j] = kt[:, j * bt:(j + 1) * bt]
            at_ref[j] = at[:, j * bt:(j + 1) * bt]

    qt = qt_ref[i]
    kt = kt_ref[i]
    at = at_ref[i]
    for bb in range(bt):
        for head in range(GLA_HEADS):
            rows = slice(head * HEAD_DIM, (head + 1) * HEAD_DIM)
            v = v_ref[bb:bb + 1, head * LANES:(head + 1) * LANES]
            st = at[rows, bb:bb + 1] * s_ref[bb, 0, head] + kt[rows, bb:bb + 1] * v
            so_ref[bb, head] = st
            o_ref[bb:bb + 1, head * LANES:(head + 1) * LANES] = jnp.sum(qt[rows, bb:bb + 1] * st, axis=0,
                                                                        keepdims=True)


def _gla_decode(gq_s, gk_s, la_s, gv_s, state, layer):
    dec_b, dk4 = gq_s.shape
    bt = GLA_DEC_TILE
    n_tiles = dec_b // bt
    full = lambda a: pl.BlockSpec(a.shape, lambda i: (0,) * a.ndim)
    sblk = (bt, None, GLA_HEADS, HEAD_DIM, LANES)
    return pl.pallas_call(
        _gla_decode_kernel,
        name="gla_decode",
        grid=(n_tiles,),
        in_specs=[full(gq_s), full(gk_s), full(la_s), pl.BlockSpec((bt, gv_s.shape[1]), lambda i: (i, 0)),
                  pl.BlockSpec((bt, 1, GLA_HEADS, HEAD_DIM, LANES), lambda i: (i, layer, 0, 0, 0))],
        out_specs=[pl.BlockSpec((bt, gv_s.shape[1]), lambda i: (i, 0)),
                   pl.BlockSpec((bt, GLA_HEADS, HEAD_DIM, LANES), lambda i: (i, 0, 0, 0))],
        out_shape=[jax.ShapeDtypeStruct(gv_s.shape, F32),
                   jax.ShapeDtypeStruct((dec_b, GLA_HEADS, HEAD_DIM, LANES), F32)],
        scratch_shapes=[pltpu.VMEM((n_tiles, dk4, bt), F32)] * 3,
        compiler_params=_cparams(1),
    )(gq_s, gk_s, la_s, gv_s, state)


def _overlap_t():
    r, w = SEL_LEN // CMP_STRIDE, CMP_LEN // CMP_STRIDE
    off = (np.arange(r)[:, None] + np.arange(w)[None, :]).reshape(-1)
    j = np.arange(LANES)
    c = np.arange(LANES)
    ov = np.sum(c[None, :, None] == (r * j[:, None, None] + off[None, None, :]), axis=-1)
    return ov.astype(np.float32)


def _rope_tables(pos):
    half = HEAD_DIM // 2
    inv = ROPE_THETA ** (-jnp.arange(half, dtype=F32) / half)
    ang = pos[:, None] * inv[None, :]
    cos, sin = jnp.cos(ang), jnp.sin(ang)
    cs = jnp.concatenate([cos, cos, cos, cos], axis=1)
    sn = jnp.concatenate([-sin, sin, -sin, sin], axis=1)
    return cs, sn


def _compress_weights(pe, w1, w2):
    dh = HEAD_DIM
    w1r = w1.reshape(CMP_LEN, dh, -1)
    hid = w1r.shape[-1]
    z = jnp.zeros((dh, hid), w1.dtype)
    groups = []
    for g in range(NSA_GROUPS):
        mats = []
        for lp in range(CMP_STRIDE // 2):
            halves = []
            for base in (0, CMP_STRIDE):
                blocks = []
                for l in (2 * lp, 2 * lp + 1):
                    blocks += [w1r[base + l], z] if g == 0 else [z, w1r[base + l]]
                halves.append(jnp.concatenate(blocks, axis=0))
            mats.append(jnp.concatenate(halves, axis=1))
        groups.append(jnp.stack(mats))
    wc = jnp.stack(groups).astype(BF16)
    pe_flat = jnp.broadcast_to(pe.reshape(1, -1), (8, pe.size))
    return wc, pe_flat, w2.astype(BF16)


def kernel(x_prompt, x_sample, cache_nsa_kv, cache_win_kv, state_gla, page_table, ln_in_g, ln_in_b, w_in, cmp_k_pe, cmp_k_w1, cmp_k_w2, cmp_v_pe, cmp_v_w1, cmp_v_w2, gla_w_a2, gla_b_a, gla_norm_g, w_nsa_up, w_gla_up, w_out, ln1_g, ln1_b, mlp_w1, mlp_w2, ln2_g, ln2_b):
    batch, seq, dm = x_prompt.shape
    dec_b = x_sample.shape[0]
    depth = w_in.shape[0]
    n_phys, _, page = cache_nsa_kv.shape[:3]
    past = page_table.shape[1] * page
    wbuf = cache_win_kv.shape[2]
    alpha = (2.0 * depth) ** 0.25
    n_p = batch * seq
    qw = NSA_HEADS * HEAD_DIM

    h_p = _entry_norm(x_prompt.reshape(n_p, dm), ln_in_g, ln_in_b)
    h_s = _entry_norm(x_sample.reshape(dec_b, dm), ln_in_g, ln_in_b)

    cs_p, sn_p = _rope_tables(jnp.arange(seq, dtype=F32))
    cs_s, sn_s = _rope_tables(jnp.full((dec_b,), past, F32))
    ovl = jnp.asarray(_overlap_t())
    wlev = jnp.asarray(_gla_level_matrix(GLA_CHUNK)).astype(BF16)
    col = np.arange(LANES)
    gg = jnp.asarray(((col[:, None] // NSA_HPG == col[None, :] // NSA_HPG)
                      & (col[:, None] < NSA_HEADS) & (col[None, :] < NSA_HEADS)).astype(np.float32)).astype(BF16)
    selr = jnp.asarray(np.stack([col[None, :] == np.arange(NSA_HEADS)[:, None] * 3 + r
                                 for r in range(3)]).astype(np.float32)).astype(BF16)
    cache_t = jnp.transpose(cache_nsa_kv, (0, 1, 3, 4, 5, 2))
    win_t = jnp.transpose(cache_win_kv, (0, 1, 3, 4, 5, 2))
    expand = jnp.asarray((np.arange(past)[None, :] // SEL_LEN == col[:, None]).astype(np.float32)).astype(BF16)
    r_idx = np.arange(page)[None, :]
    m_idx = np.arange(page // 2)[:, None]
    cpp = page // CMP_STRIDE
    perm = jnp.asarray(np.stack([r_idx == (m_idx % cpp) * CMP_STRIDE + 2 * (m_idx // cpp) + par
                                 for par in range(2)]).astype(np.float32)).astype(BF16)
    win_buf = None

    sizes = (qw, 6 * NSA_GROUPS * HEAD_DIM, 3 * NSA_HEADS, GLA_HEADS * HEAD_DIM, GLA_HEADS * HEAD_DIM,
             GLA_HEADS * LANES, GLA_RANK, GLA_HEADS * LANES, 2 * dm)
    pts = np.concatenate([[0], np.cumsum(sizes)])
    seg = lambda w, i: w[:, pts[i]:pts[i + 1]]

    t_bufs = None
    gla_p, kv_s, gla_s = [], [], []
    for l in range(depth):
        wl = w_in[l]
        misc = jnp.concatenate([seg(wl, 2), seg(wl, 6), jnp.zeros((dm, LANES - 3 * NSA_HEADS - GLA_RANK), F32)], 1)
        w_a = jnp.concatenate([seg(wl, 0), seg(wl, 1), seg(wl, 3), seg(wl, 4), seg(wl, 5), misc], 1).astype(BF16)
        w_b = jnp.concatenate([seg(wl, 7), seg(wl, 8)], axis=1).astype(BF16)
        wa_pad = jnp.zeros((LANES, GLA_HEADS * HEAD_DIM), F32).at[3 * NSA_HEADS:3 * NSA_HEADS + GLA_RANK].set(
            gla_w_a2[l]).astype(BF16)
        ba = gla_b_a[l][None, :]
        w_nu, w_gu, w_o = w_nsa_up[l].astype(BF16), w_gla_up[l].astype(BF16), w_out[l].astype(BF16)
        w1, w2 = mlp_w1[l].astype(BF16), mlp_w2[l].astype(BF16)
        gn, g1, b1, g2, b2 = (gla_norm_g[l][None, :], ln1_g[l][None, :], ln1_b[l][None, :], ln2_g[l][None, :],
                              ln2_b[l][None, :])

        wck, pek, w2k = _compress_weights(cmp_k_pe[l], cmp_k_w1[l], cmp_k_w2[l])
        wcv, pev, w2v = _compress_weights(cmp_v_pe[l], cmp_v_w1[l], cmp_v_w2[l])
        wc = jnp.stack([wck, wcv])
        pe_pair = _pe_bias(jnp.stack([pek, pev]), jnp.stack([cmp_k_w1[l], cmp_v_w1[l]]))
        w2c = jnp.stack([w2k, w2v])

        qp, qr, kv, win, gq, gk, gv, la, ng, kvt, wint = _inproj(h_p, w_a, cs_p, sn_p, wa_pad, ba, (l, depth), t_bufs)
        t_bufs = (kvt, wint)
        kc, vc = _compress_prompt(kv, wc, pe_pair, w2c, batch, seq)
        o_lo, o_hi = _nsa_prompt(qp, qr, ng, kc, vc, kv, win, kvt, wint, ovl, batch, seq, l)
        half_rows = seq // 2
        o_hi = jnp.flip(o_hi.reshape(batch, half_rows // Q_TILE, Q_TILE, qw), axis=1)
        o_nsa = jnp.concatenate([o_lo.reshape(batch, half_rows, qw), o_hi.reshape(batch, half_rows, qw)],
                                axis=1).reshape(n_p, qw)
        o_gla, st_p = _gla_prompt(gq, gk, gv, la, wlev, batch, seq)
        gla_p.append(st_p)
        h_p = _outproj(h_p, o_nsa, o_gla, w_b, gn, w_nu, w_gu, w_o, g1, b1, alpha)
        h_p = _mlp(h_p, w1, w2, g2, b2, alpha)

        qp, qr, kv, win, gq, gk, gv, la, ng = _inproj(h_s, w_a, cs_s, sn_s, wa_pad, ba)
        o8, win_buf = _nsa_decode(page_table, cache_t, l, wc, pe_pair, w2c,
                                  qp.reshape(dec_b, NSA_HEADS, HEAD_DIM), qr.reshape(dec_b, NSA_HEADS, HEAD_DIM),
                                  ng[:, None, :], kv[:, None, :], win[:, None, :], win_t, ovl, gg, selr,
                                  expand, perm, win_buf)
        o_gla, st_s = _gla_decode(gq, gk, la, gv, state_gla, l)
        kv_s.append(kv.reshape(dec_b, 1, 4, NSA_GROUPS, HEAD_DIM))
        gla_s.append(st_s)
        h_s = _outproj(h_s, o8.reshape(dec_b, qw), o_gla, w_b, gn, w_nu, w_gu, w_o, g1, b1, alpha)
        h_s = _mlp(h_s, w1, w2, g2, b2, alpha)

    kvt, wint = t_bufs
    wn = min(WINDOW, seq)
    return (h_p.reshape(batch, seq, dm), h_s.reshape(dec_b, 1, dm),
            jnp.transpose(kvt, (0, 1, 5, 2, 3, 4)), jnp.transpose(wint[..., seq - wn:], (0, 1, 5, 2, 3, 4)),
            jnp.stack(gla_p, axis=1), jnp.stack(kv_s, axis=1), jnp.transpose(win_buf, (0, 1, 5, 2, 3, 4)),
            jnp.stack(gla_s, axis=1))
```

```python
import functools
import math

import numpy as np
import jax
import jax.numpy as jnp
from jax import lax
from jax.experimental import pallas as pl
from jax.experimental.pallas import tpu as pltpu

F32 = jnp.float32
BF16 = jnp.bfloat16

NSA_HEADS = 8
NSA_GROUPS = 2
NSA_HPG = NSA_HEADS // NSA_GROUPS
HEAD_DIM = 64
SCALE = HEAD_DIM ** -0.5
CMP_LEN = 32
CMP_STRIDE = 16
SEL_LEN = 64
N_SEL = 16
WINDOW = 512
GLA_HEADS = 4
GLA_RANK = 16
GLA_TAU = 16.0
ROPE_THETA = 10000.0
EPS = 1e-5
BIG = 1e6
NEG = -1e30
LOG2E = math.log2(math.e)

LANES = 128
ROW_TILE = 256
Q_TILE = 128
KV_TILE = 256
GLA_CHUNK = 256
GLA_DEC_TILE = 16
NSA_DEC_TILE = 4
VMEM_LIMIT = 56 * 1024 * 1024


def _cparams(n_axes):
    return pltpu.CompilerParams(dimension_semantics=("arbitrary",) * n_axes,
                                vmem_limit_bytes=VMEM_LIMIT)


def _dot(a, b):
    return jnp.dot(a, b, preferred_element_type=F32)


def _dot_nt(a, b):
    return lax.dot_general(a, b, (((1,), (1,)), ((), ())), preferred_element_type=F32)


def _split3(x):
    hi = x.astype(BF16)
    r = x - hi.astype(F32)
    mid = r.astype(BF16)
    lo = (r - mid.astype(F32)).astype(BF16)
    return hi, mid, lo


def _layer_norm(x, g, b):
    mu = jnp.mean(x, axis=-1, keepdims=True)
    xc = x - mu
    var = jnp.mean(xc * xc, axis=-1, keepdims=True)
    return xc * lax.rsqrt(var + EPS) * g + b


def _sigmoid(x):
    return 1.0 / (1.0 + jnp.exp(-x))


def _row_tile(n):
    return ROW_TILE if n % ROW_TILE == 0 else n


C_Q, C_KV, C_GQ, C_GK, C_GV, C_MISC, C_END = 0, 512, 1280, 1536, 1792, 2304, 2432


def _rope128(x, cs, sn):
    lane = lax.broadcasted_iota(jnp.int32, x.shape, 1)
    first = (lane % HEAD_DIM) < (HEAD_DIM // 2)
    swapped = jnp.where(first, pltpu.roll(x, LANES - HEAD_DIM // 2, 1), pltpu.roll(x, HEAD_DIM // 2, 1))
    return x * cs + swapped * sn


def _inproj_kernel(h_ref, w_ref, cs_ref, sn_ref, wa_ref, ba_ref, *rest, with_t, pre_ln):
    n_out = 11 if with_t else 9
    qp_ref, qr_ref, kv_ref, win_ref, gq_ref, gk_ref, gv_ref, la_ref, ng_ref = rest[-n_out:][:9]
    kvt_ref, wint_ref = rest[-2:] if with_t else (None, None)
    h = h_ref[...]
    if pre_ln:
        h = _layer_norm(h, rest[0][...], rest[1][...])
    hb = h.astype(BF16)
    cs = cs_ref[...]
    sn = sn_ref[...]

    def seg(lo, hi):
        return _dot(hb, w_ref[:, lo:hi])

    for j2 in range(2):
        q2 = seg(C_Q + j2 * 2 * LANES, C_Q + (j2 + 1) * 2 * LANES)
        for jj in range(2):
            qj = q2[:, jj * LANES:(jj + 1) * LANES]
            j = 2 * j2 + jj
            qp_ref[:, j * LANES:(j + 1) * LANES] = qj
            qr_ref[:, j * LANES:(j + 1) * LANES] = _rope128(qj, cs, sn)
    for s2 in range(3):
        x2 = seg(C_KV + s2 * 2 * LANES, C_KV + (s2 + 1) * 2 * LANES)
        for ss in range(2):
            s = 2 * s2 + ss
            x = x2[:, ss * LANES:(ss + 1) * LANES]
            if s in (2, 4):
                x = _rope128(x, cs, sn)
            if s < 4:
                kv_ref[:, s * LANES:(s + 1) * LANES] = x
            else:
                win_ref[:, (s - 4) * LANES:(s - 3) * LANES] = x
            if with_t:
                xt = x.T.reshape(NSA_GROUPS, HEAD_DIM, x.shape[0])
                if s < 4:
                    kvt_ref[s] = xt
                else:
                    wint_ref[s - 4] = xt
    gq_ref[...] = seg(C_GQ, C_GK) * (HEAD_DIM ** -0.5)
    gk_ref[...] = seg(C_GK, C_GV)
    gv_ref[...] = seg(C_GV, C_MISC)
    misc = seg(C_MISC, C_END)
    ng_ref[...] = misc
    x = _dot(misc.astype(BF16), wa_ref[...]) + ba_ref[...]
    la_ref[...] = (jnp.minimum(x, 0.0) - jnp.log(1.0 + jnp.exp(-jnp.abs(x)))) * (1.0 / GLA_TAU)


def _inproj(h, w_a, cs_tab, sn_tab, wa_pad, ba, t_layout=None, t_prev=None, ln=None):
    n, d = h.shape
    tm = _row_tile(n)
    tab_tiles = cs_tab.shape[0] // tm

    def tab_map(i):
        return (i % tab_tiles, 0)

    row = lambda w: pl.BlockSpec((tm, w), lambda i: (i, 0))
    full = lambda a: pl.BlockSpec(a.shape, lambda i: (0,) * a.ndim)
    widths = (512, 512, 512, 256, 256, 256, 512, 256, 128)
    out_specs = [row(w) for w in widths]
    out_shape = [jax.ShapeDtypeStruct((n, w), F32) for w in widths]
    in_specs = [row(d), full(w_a), pl.BlockSpec((tm, LANES), tab_map), pl.BlockSpec((tm, LANES), tab_map),
                full(wa_pad), full(ba)]
    args = [h, w_a, cs_tab, sn_tab, wa_pad, ba]
    if ln is not None:
        in_specs += [full(ln[0]), full(ln[1])]
        args += list(ln)
    aliases = {}
    if t_layout is not None:
        layer, depth = t_layout
        batch = n // cs_tab.shape[0]
        for n_slots in (4, 2):
            out_specs.append(pl.BlockSpec((None, None, n_slots, NSA_GROUPS, HEAD_DIM, tm),
                                          lambda i: (i // tab_tiles, layer, 0, 0, 0, i % tab_tiles)))
            out_shape.append(jax.ShapeDtypeStruct((batch, depth, n_slots, NSA_GROUPS, HEAD_DIM, cs_tab.shape[0]), F32))
        if t_prev is not None:
            for k, buf in enumerate(t_prev):
                in_specs.append(pl.BlockSpec(memory_space=pl.ANY))
                aliases[len(args)] = len(widths) + k
                args.append(buf)
    return pl.pallas_call(
        functools.partial(_inproj_kernel, with_t=t_layout is not None, pre_ln=ln is not None),
        name="inproj",
        grid=(n // tm,),
        in_specs=in_specs,
        out_specs=out_specs,
        out_shape=out_shape,
        input_output_aliases=aliases,
        compiler_params=_cparams(1),
    )(*args)


def _gelu_tanh(x):
    return 0.5 * x * (1.0 + jnp.tanh(math.sqrt(2.0 / math.pi) * (x + 0.044715 * x * x * x)))


def _compress_rows(chunk_pair, n_chunks, wc_ref, pe_ref, w2_ref, seq_chunks=None):
    seq_chunks = n_chunks if seq_chunks is None else seq_chunks
    outs = []
    for s in range(2):
        acc = [jnp.zeros((n_chunks, 2 * LANES), F32) for _ in range(NSA_GROUPS)]
        for lp in range(CMP_STRIDE // 2):
            a = chunk_pair(s, lp).astype(BF16)
            for g in range(NSA_GROUPS):
                acc[g] = acc[g] + _dot(a, wc_ref[s, g, lp])
        row = lax.broadcasted_iota(jnp.int32, (n_chunks, LANES), 0)
        parts = []
        for g in range(NSA_GROUPS):
            hid = acc[g][:, :LANES] + pltpu.roll(acc[g][:, LANES:], n_chunks - 1, 0) + pe_ref[s]
            parts.append(_dot(_gelu_tanh(hid).astype(BF16), w2_ref[s]))
        out = jnp.concatenate(parts, axis=1)
        outs.append(jnp.where(row % seq_chunks < seq_chunks - 1, out, 0.0))
    return outs


def _pe_bias_kernel(pe_ref, w1_ref, o_ref):
    for s in range(2):
        o_ref[s] = _dot(pe_ref[s].astype(BF16), w1_ref[s].astype(BF16))[0:1]


def _pe_bias(pe_flat, w1):
    hid = w1.shape[2]
    return pl.pallas_call(
        _pe_bias_kernel,
        name="pe_bias",
        out_shape=jax.ShapeDtypeStruct((2, 1, hid), F32),
    )(pe_flat, w1)


def _compress_kernel(xk_ref, xv_ref, wc_ref, pe_ref, w2_ref, kc_ref, vc_ref):
    x_refs = (xk_ref, xv_ref)
    n_chunks = xk_ref.shape[0] // CMP_STRIDE

    def chunk_pair(slot, lp):
        return jnp.concatenate([x_refs[slot][pl.ds(2 * lp, n_chunks, stride=CMP_STRIDE), :],
                                x_refs[slot][pl.ds(2 * lp + 1, n_chunks, stride=CMP_STRIDE), :]], axis=1)

    kc, vc = _compress_rows(chunk_pair, n_chunks, wc_ref, pe_ref, w2_ref)
    kc_ref[0] = kc
    vc_ref[0] = vc


def _compress_prompt(kv, wc, pe_pair, w2, batch, seq):
    nc = seq // CMP_STRIDE
    full = lambda a: pl.BlockSpec(a.shape, lambda b: (0,) * a.ndim)
    return pl.pallas_call(
        _compress_kernel,
        name="compress",
        grid=(batch,),
        in_specs=[pl.BlockSpec((seq, LANES), lambda b: (b, 0)), pl.BlockSpec((seq, LANES), lambda b: (b, 1)),
                  full(wc), full(pe_pair), full(w2)],
        out_specs=[pl.BlockSpec((1, nc, LANES), lambda b: (b, 0, 0))] * 2,
        out_shape=[jax.ShapeDtypeStruct((batch, nc, LANES), F32)] * 2,
        compiler_params=_cparams(1),
    )(kv, kv, wc, pe_pair, w2)


def _select_mask(imp_t, qpos, n_blocks):
    nbp, nq = imp_t.shape
    blk = lax.broadcasted_iota(jnp.int32, (nbp, nq), 0)
    cur = qpos // SEL_LEN
    causal = blk * SEL_LEN <= qpos
    forced = (blk == 0) | (blk == cur) | (blk == cur - 1)
    score = jnp.where(forced, BIG, jnp.where(causal, imp_t, -jnp.inf))
    score = jnp.where(blk < n_blocks, score, -jnp.inf)
    rank = jnp.zeros((nbp, nq), jnp.int32)
    for j in range(n_blocks):
        other = score[j:j + 1, :]
        rank = rank + jnp.where(blk > j, jnp.where(other >= score, 1, 0), jnp.where(other > score, 1, 0))
    return (rank < N_SEL) & (score > -jnp.inf)


V_ROWS = LANES + 16


def _softmax_step(state, s, pv_fn):
    m, acc = state
    m_new = jnp.maximum(m, jnp.max(s, axis=0, keepdims=True))
    alpha = jnp.exp2(m - m_new)
    p = jnp.exp2(s - m_new)
    return m_new, alpha * acc + pv_fn(p.astype(BF16))


def _softmax_init(nq):
    return (jnp.full((1, nq), NEG, F32), jnp.zeros((V_ROWS, nq), F32))


def _nsa_prompt_kernel(qpa_ref, qra_ref, nga_ref, qpb_ref, qrb_ref, ngb_ref, kc_ref, vc_ref, ks_ref, vs_ref,
                       kw_ref, vw_ref, ovl_ref, oa_ref, ob_ref, ksa_ref, vst_ref, kwa_ref, vwt_ref,
                       *, seq, n_blocks):
    g = pl.program_id(1)
    i = pl.program_id(2)
    tq = Q_TILE
    nqt = seq // tq
    n_kt = seq // KV_TILE
    nq = NSA_HPG * tq

    @pl.when((g == 0) & (i == 0))
    def _():
        for kt in range(n_kt):
            rows = pl.ds(kt * KV_TILE, KV_TILE)
            key = kt * KV_TILE + lax.broadcasted_iota(jnp.int32, (KV_TILE, LANES), 0)
            lane = lax.broadcasted_iota(jnp.int32, (KV_TILE, LANES), 1)
            onehot = jnp.where(key // SEL_LEN == lane, 1.0, 0.0).astype(BF16)
            ksa_ref[kt] = jnp.concatenate([ks_ref[rows, :].astype(BF16), onehot], axis=1)
            kwa_ref[kt] = kw_ref[rows, :].astype(BF16)
            cols = slice(kt * KV_TILE, (kt + 1) * KV_TILE)
            ones = jnp.ones((V_ROWS - LANES, KV_TILE), BF16)
            vst_ref[kt] = jnp.concatenate([vs_ref[:, :, cols].reshape(LANES, KV_TILE).astype(BF16), ones], axis=0)
            vwt_ref[kt] = jnp.concatenate([vw_ref[:, :, cols].reshape(LANES, KV_TILE).astype(BF16), ones], axis=0)

    lane_q = lax.broadcasted_iota(jnp.int32, (1, nq), 1) % tq
    lane1 = lax.broadcasted_iota(jnp.int32, (1, tq), 1)
    lane = lax.broadcasted_iota(jnp.int32, (tq, LANES), 1)
    own = (lane // HEAD_DIM) == g
    ovl = ovl_ref[...]
    kc = kc_ref[0].astype(BF16)
    vct = vc_ref[0].T.astype(BF16)
    nc = kc.shape[0]
    nbp = -(-n_blocks // 8) * 8

    def own_rows(x):
        return jnp.where(g == 0, x[:HEAD_DIM], x[HEAD_DIM:])

    def prepare(qp_ref, qr_ref, qs):
        def stack_heads(ref, scale):
            parts = []
            for h in range(NSA_HPG):
                qh = ref[:, h * HEAD_DIM:(h + 1) * HEAD_DIM] * scale
                parts.append(jnp.where(own, jnp.concatenate([qh, qh], axis=1), 0.0))
            return parts

        qpos = qs + lane_q
        qp = jnp.concatenate(stack_heads(qp_ref, SCALE), axis=0).astype(BF16)
        s = _dot_nt(kc, qp)
        cblk = lax.broadcasted_iota(jnp.int32, (nc, nq), 0)
        valid = cblk * CMP_STRIDE + (CMP_LEN - 1) <= qpos
        s = jnp.where(valid, s, NEG)
        m = jnp.max(s, axis=0, keepdims=True)
        p = jnp.where(valid, jnp.exp(s - m), 0.0)
        p = p / jnp.maximum(jnp.sum(p, axis=0, keepdims=True), 1e-30)
        o_cmp = own_rows(_dot(vct, p.astype(BF16)))
        psum = p[:, 0:tq]
        for h in range(1, NSA_HPG):
            psum = psum + p[:, h * tq:(h + 1) * tq]
        imp_t = sum(_dot(ovl, piece) for piece in _split3(psum))
        sel = _select_mask(imp_t[:nbp], qs + lane1, n_blocks)
        bias_t = jnp.where(sel, 0.0, NEG)
        bias = jnp.concatenate([bias_t, jnp.zeros((LANES - nbp, tq), F32)], axis=0).T
        qr_parts = stack_heads(qr_ref, SCALE * LOG2E)
        q_sel = jnp.concatenate([jnp.concatenate([q, bias], axis=1) for q in qr_parts], axis=0).astype(BF16)
        q_win = jnp.concatenate(qr_parts, axis=0).astype(BF16)
        return qpos, o_cmp, q_sel, q_win

    qs_a = i * tq
    qs_b = (nqt - 1 - i) * tq
    qpos_a, o_cmp_a, q_sel_a, q_win_a = prepare(qpa_ref, qra_ref, qs_a)
    qpos_b, o_cmp_b, q_sel_b, q_win_b = prepare(qpb_ref, qrb_ref, qs_b)

    kd_a = qs_a // KV_TILE
    kd_b = qs_b // KV_TILE
    max_kd_a = ((nqt // 2 - 1) * tq) // KV_TILE
    n_sel = n_kt + 1
    n_win = WINDOW // KV_TILE + 1

    krow = lax.broadcasted_iota(jnp.int32, (KV_TILE, nq), 0)

    def sel_task(j):
        if j == 0:
            return 0, q_sel_a, qpos_a, True
        if j > max_kd_a:
            return j - kd_a - 1, q_sel_b, qpos_b, j == n_sel - 1
        in_a = j <= kd_a
        return (jnp.where(in_a, j, j - kd_a - 1), jnp.where(in_a, q_sel_a, q_sel_b),
                jnp.where(in_a, qpos_a, qpos_b), True)

    tasks = []
    for j in range(n_sel):
        kt, q, qpos, masked = sel_task(j)
        valid = (kt * KV_TILE + krow <= qpos) if masked else None
        tasks.append(("sel", j, ksa_ref, vst_ref, kt, q, valid))
    for name, kd, q, qpos in (("wa", kd_a, q_win_a, qpos_a), ("wb", kd_b, q_win_b, qpos_b)):
        for j in range(n_win):
            kt = kd - (n_win - 1) + j
            d = qpos - (kt * KV_TILE + krow)
            valid = (d >= 0) & (d <= WINDOW) & (kt >= 0)
            tasks.append((name, j, kwa_ref, vwt_ref, jnp.maximum(kt, 0), q, valid))

    def scores(task):
        _, _, k_ref, _, kt, q, _ = task
        return _dot_nt(k_ref[kt], q)

    sel_tasks = tasks[:n_sel]
    win_tasks = tasks[n_sel:]
    tasks = []
    while sel_tasks or win_tasks:
        if sel_tasks:
            tasks.append(sel_tasks.pop(0))
        if win_tasks:
            tasks.append(win_tasks.pop(0))

    results = {}
    states = {}
    s_next = scores(tasks[0])
    for t, task in enumerate(tasks):
        name, j, _, vt_ref, kt, _, mask = task
        s = s_next
        if t + 1 < len(tasks):
            s_next = scores(tasks[t + 1])
        if j == 0:
            states[name] = _softmax_init(nq)
        state = states[name]
        if name == "sel" and 1 <= j <= max_kd_a + 1:
            switch = j == kd_a + 1
            prev = results.get("sa", state)
            results["sa"] = tuple(jnp.where(switch, x, y) for x, y in zip(state, prev))
            state = tuple(jnp.where(switch, x, y) for x, y in zip(_softmax_init(nq), state))
        if mask is not None:
            s = jnp.where(mask, s, NEG)
        state = _softmax_step(state, s, lambda p: _dot(vt_ref[kt], p))
        states[name] = state
        last = (name == "sel" and j == n_sel - 1) or (name != "sel" and j == n_win - 1)
        if last:
            results["sb" if name == "sel" else name] = state

    def finish(key):
        _, acc = results[key]
        return own_rows(acc[:LANES] / acc[LANES:LANES + 1])

    for ng_ref, o_ref, branches in ((nga_ref, oa_ref, (o_cmp_a, finish("sa"), finish("wa"))),
                                    (ngb_ref, ob_ref, (o_cmp_b, finish("sb"), finish("wb")))):
        ng_t = ng_ref[...].T
        outs = []
        for h in range(NSA_HPG):
            cols = slice(h * tq, (h + 1) * tq)
            tot = jnp.zeros((HEAD_DIM, tq), F32)
            for r, o in enumerate(branches):
                i0 = h * 3 + r
                i1 = (NSA_HPG + h) * 3 + r
                gate = _sigmoid(jnp.where(g == 0, ng_t[i0:i0 + 1], ng_t[i1:i1 + 1]))
                tot = tot + gate * o[:, cols]
            outs.append(tot.T)
        o_ref[...] = jnp.concatenate(outs, axis=1)


def _nsa_prompt(qp, qr, ng, kc, vc, kv, win, kvt, wint, ovl, batch, seq, layer):
    n_blocks = -(-seq // SEL_LEN)
    nqt = seq // Q_TILE
    n_kt = seq // KV_TILE
    assert nqt % 2 == 0 and KV_TILE == 2 * Q_TILE and WINDOW % KV_TILE == 0
    gw = NSA_HPG * HEAD_DIM
    half = nqt // 2
    qa = pl.BlockSpec((Q_TILE, gw), lambda b, g, i: (b * nqt + i, g))
    qb = pl.BlockSpec((Q_TILE, gw), lambda b, g, i: (b * nqt + nqt - 1 - i, g))
    na = pl.BlockSpec((Q_TILE, LANES), lambda b, g, i: (b * nqt + i, 0))
    nb = pl.BlockSpec((Q_TILE, LANES), lambda b, g, i: (b * nqt + nqt - 1 - i, 0))
    cspec = pl.BlockSpec((1,) + kc.shape[1:], lambda b, g, i: (b, 0, 0))
    ospec = pl.BlockSpec((Q_TILE, gw), lambda b, g, i: (b * half + i, g))

    def rowspec(col):
        return pl.BlockSpec((seq, LANES), lambda b, g, i: (b, col))

    def tspec(slot):
        return pl.BlockSpec((None, None, None, NSA_GROUPS, HEAD_DIM, seq), lambda b, g, i: (b, layer, slot, 0, 0, 0))

    oshape = jax.ShapeDtypeStruct((batch * half * Q_TILE, qp.shape[1]), F32)
    return pl.pallas_call(
        functools.partial(_nsa_prompt_kernel, seq=seq, n_blocks=n_blocks),
        name="nsa_prompt",
        grid=(batch, NSA_GROUPS, half),
        in_specs=[qa, qa, na, qb, qb, nb, cspec, cspec, rowspec(2), tspec(3), rowspec(0), tspec(1),
                  pl.BlockSpec(ovl.shape, lambda b, g, i: (0, 0))],
        out_specs=[ospec, ospec],
        out_shape=[oshape, oshape],
        scratch_shapes=[pltpu.VMEM((n_kt, KV_TILE, 2 * LANES), BF16), pltpu.VMEM((n_kt, V_ROWS, KV_TILE), BF16),
                        pltpu.VMEM((n_kt, KV_TILE, LANES), BF16), pltpu.VMEM((n_kt, V_ROWS, KV_TILE), BF16)],
        compiler_params=_cparams(3),
    )(qp, qr, ng, qp, qr, ng, kc, vc, kv, kvt, win, wint, ovl)


def _gla_level_matrix(c):
    t = np.arange(c)[:, None]
    u = np.arange(c)[None, :]
    mats = [(u <= t), (u > t)]
    m = c
    while m >= 2:
        split = (t // m) * m + m // 2
        upper = (t % m) >= m // 2
        mats.append(np.where(upper, (u >= split) & (u <= t), (u > t) & (u < split)))
        m //= 2
    return np.concatenate(mats, axis=0).astype(np.float32)


def _gla_prompt_kernel(q_ref, k_ref, v_ref, la_ref, w_ref, o_ref, st_ref, e_ref, s_ref, *, seq):
    c = GLA_CHUNK
    n_levels = int(math.log2(c))
    s_ref[...] = jnp.zeros_like(s_ref)
    row = lax.broadcasted_iota(jnp.int32, (c, c), 0)
    col = lax.broadcasted_iota(jnp.int32, (c, c), 1)
    rowl = lax.broadcasted_iota(jnp.int32, (c, LANES), 0)
    lanel = lax.broadcasted_iota(jnp.int32, (c, LANES), 1)

    def chunk(ci, _):
        rows = pl.ds(pl.multiple_of(ci * c, c), c)
        la = la_ref[rows, :]
        hi = la.astype(BF16)
        lo = (la - hi.astype(F32)).astype(BF16)
        e_ref[...] = jnp.exp(_dot(w_ref[...], hi) + _dot(w_ref[...], lo))
        for pair in range(GLA_HEADS // 2):
            lanes = slice(pair * LANES, (pair + 1) * LANES)
            q = q_ref[rows, lanes]
            k = k_ref[rows, lanes]
            q0 = (q * e_ref[0:c, lanes]).astype(BF16)
            kdec = k * e_ref[c:2 * c, lanes]
            a_last = e_ref[c - 1:c, lanes]
            qls, kls = [], []
            for lv in range(n_levels):
                m = c >> lv
                x = e_ref[(2 + lv) * c:(3 + lv) * c, lanes]
                upper = (rowl % m) >= (m // 2)
                qls.append(jnp.where(upper, q * x, 0.0).astype(BF16))
                kls.append(jnp.where(upper, 0.0, k * x))
            for hh in range(2):
                head = pair * 2 + hh
                mine = (lanel // HEAD_DIM) == hh
                a = jnp.where(row == col, _dot_nt(q.astype(BF16), jnp.where(mine, k, 0.0).astype(BF16)), 0.0)
                for lv in range(n_levels):
                    m = c >> lv
                    same = (row // m) == (col // m)
                    a = a + jnp.where(same, _dot_nt(qls[lv], jnp.where(mine, kls[lv], 0.0).astype(BF16)), 0.0)
                v = v_ref[rows, head * LANES:(head + 1) * LANES]
                st = s_ref[head]
                o = _dot(a.astype(BF16), v.astype(BF16)) + _dot_nt(q0, st.astype(BF16))
                o_ref[rows, head * LANES:(head + 1) * LANES] = o
                kd = jnp.where(mine, kdec, 0.0).astype(BF16)
                s_ref[head] = st * a_last + _dot(v.T.astype(BF16), kd)
        return 0

    lax.fori_loop(0, seq // c, chunk, 0)
    for head in range(GLA_HEADS):
        st = s_ref[head].T
        off = (head % 2) * HEAD_DIM
        st_ref[0, head] = st[off:off + HEAD_DIM]


def _gla_prompt(gq, gk, gv, la, wlev, batch, seq):
    dk2 = gq.shape[1]
    dv4 = gv.shape[1]
    return pl.pallas_call(
        functools.partial(_gla_prompt_kernel, seq=seq),
        name="gla_prompt",
        grid=(batch,),
        in_specs=[pl.BlockSpec((seq, dk2), lambda b: (b, 0)), pl.BlockSpec((seq, dk2), lambda b: (b, 0)),
                  pl.BlockSpec((seq, dv4), lambda b: (b, 0)), pl.BlockSpec((seq, dk2), lambda b: (b, 0)),
                  pl.BlockSpec(wlev.shape, lambda b: (0, 0))],
        out_specs=[pl.BlockSpec((seq, dv4), lambda b: (b, 0)),
                   pl.BlockSpec((1, GLA_HEADS, HEAD_DIM, LANES), lambda b: (b, 0, 0, 0))],
        out_shape=[jax.ShapeDtypeStruct((gq.shape[0], dv4), F32),
                   jax.ShapeDtypeStruct((batch, GLA_HEADS, HEAD_DIM, LANES), F32)],
        scratch_shapes=[pltpu.VMEM((wlev.shape[0], dk2), F32), pltpu.VMEM((GLA_HEADS, LANES, LANES), F32)],
        compiler_params=_cparams(1),
    )(gq, gk, gv, la, wlev)


def _outproj_kernel(h_ref, *refs, alpha, tiles_per_seq, pre_ln):
    if tiles_per_seq:
        lo_ref, hi_ref = refs[:2]
        refs = refs[2:]
        t = pl.program_id(0) % tiles_per_seq
        hi = hi_ref[...]
        half = hi.shape[0] // 2
        on = jnp.where(t < tiles_per_seq // 2, lo_ref[...], jnp.concatenate([hi[half:], hi[:half]], axis=0))
    else:
        on = refs[0][...]
        refs = refs[1:]
    og_ref, wb_ref, gn_ref, wn_ref, wg_ref, wo_ref, g1_ref, b1_ref = refs[:8]
    o_ref = refs[-1]
    h = h_ref[...]
    if pre_ln:
        h = _layer_norm(h, refs[8][...], refs[9][...])
    hb = h.astype(BF16)
    dm = h.shape[1]
    gw = og_ref.shape[1]
    g_r = _dot(hb, wb_ref[:, :gw])
    parts = []
    for head in range(GLA_HEADS):
        x = og_ref[:, head * LANES:(head + 1) * LANES]
        x = x * lax.rsqrt(jnp.mean(x * x, axis=-1, keepdims=True) + EPS) * gn_ref[...]
        gr = g_r[:, head * LANES:(head + 1) * LANES]
        parts.append(x * (gr * _sigmoid(gr)))
    og = jnp.concatenate(parts, axis=1).astype(BF16)
    a = _dot(on.astype(BF16), wn_ref[...])
    c = _dot(og, wg_ref[...])
    m_a = _dot(hb, wb_ref[:, gw:gw + dm])
    m_c = _dot(hb, wb_ref[:, gw + dm:gw + 2 * dm])
    mix = (_sigmoid(m_a) * a + _sigmoid(m_c) * c).astype(BF16)
    y = _dot(mix, wo_ref[...])
    o_ref[...] = _layer_norm(alpha * h + y, g1_ref[...], b1_ref[...])


def _outproj(h, o_nsa, o_gla, w_b, gn, w_nsa, w_gla, w_out, g1, b1, alpha, seq=None, ln=None):
    n, d = h.shape
    tm = _row_tile(n)
    row = lambda w: pl.BlockSpec((tm, w), lambda i: (i, 0))
    full = lambda a: pl.BlockSpec(a.shape, lambda i: (0,) * a.ndim)
    if isinstance(o_nsa, tuple):
        assert tm == 2 * Q_TILE
        tps = seq // tm
        hs = tps // 2
        qw = o_nsa[0].shape[1]
        lo_spec = pl.BlockSpec((tm, qw), lambda i: (i // tps * hs + jnp.minimum(i % tps, hs - 1), 0))
        hi_spec = pl.BlockSpec((tm, qw), lambda i: (i // tps * hs + jnp.clip(tps - 1 - i % tps, 0, hs - 1), 0))
        nsa_specs, nsa_args = [lo_spec, hi_spec], list(o_nsa)
    else:
        tps = 0
        nsa_specs, nsa_args = [row(o_nsa.shape[1])], [o_nsa]
    ln_args = [] if ln is None else list(ln)
    return pl.pallas_call(
        functools.partial(_outproj_kernel, alpha=alpha, tiles_per_seq=tps, pre_ln=ln is not None),
        name="outproj",
        grid=(n // tm,),
        in_specs=[row(d)] + nsa_specs + [row(o_gla.shape[1]), full(w_b), full(gn), full(w_nsa),
                                         full(w_gla), full(w_out), full(g1), full(b1)] + [full(a) for a in ln_args],
        out_specs=row(d),
        out_shape=jax.ShapeDtypeStruct((n, d), F32),
        compiler_params=_cparams(1),
    )(h, *nsa_args, o_gla, w_b, gn, w_nsa, w_gla, w_out, g1, b1, *ln_args)


def _mlp_kernel(h_ref, w1_ref, w2_ref, g_ref, b_ref, o_ref, *, alpha):
    h = h_ref[...]
    hb = h.astype(BF16)
    dff = w1_ref.shape[1]
    step = 1024
    f = jnp.zeros(h.shape, F32)
    for c0 in range(0, dff, step):
        u = jnp.maximum(_dot(hb, w1_ref[:, c0:c0 + step]), 0.0)
        f = f + _dot((u * u).astype(BF16), w2_ref[c0:c0 + step, :])
    o_ref[...] = _layer_norm(alpha * h + f, g_ref[...], b_ref[...])


def _mlp(h, w1, w2, g, b, alpha):
    n, d = h.shape
    tm = _row_tile(n)
    row = pl.BlockSpec((tm, d), lambda i: (i, 0))
    full = lambda a: pl.BlockSpec(a.shape, lambda i: (0,) * a.ndim)
    return pl.pallas_call(
        functools.partial(_mlp_kernel, alpha=alpha),
        name="mlp",
        grid=(n // tm,),
        in_specs=[row, full(w1), full(w2), full(g), full(b)],
        out_specs=row,
        out_shape=jax.ShapeDtypeStruct((n, d), F32),
        compiler_params=_cparams(1),
    )(h, w1, w2, g, b)


def _softmax_rows(parts, extra, valid_extra=None):
    m = extra
    for s in parts:
        m = jnp.maximum(m, jnp.max(s, axis=-1, keepdims=True))
    ps = [jnp.exp(s - m) for s in parts]
    pe = jnp.exp(extra - m)
    tot = pe
    for p in ps:
        tot = tot + jnp.sum(p, axis=-1, keepdims=True)
    inv = 1.0 / tot
    return [p * inv for p in ps], pe * inv


def _nsa_decode_kernel(pt_ref, *refs, n_pages, n_blocks, past, has_prev):
    del pt_ref
    nb = NSA_DEC_TILE
    pages_all = [refs[bb * n_pages:(bb + 1) * n_pages] for bb in range(nb)]
    rest = refs[nb * n_pages:]
    (wc_ref, pe_ref, w2_ref, qp_ref, qr_ref, ng_ref, kvn_ref, winn_ref, wb_ref, ovl_ref, gg_ref, selr_ref,
     exp_ref, perm_ref) = rest[:14]
    o_ref, wo_ref = rest[-2:]
    page = pages_all[0][0].shape[-1]
    wbuf = wb_ref.shape[-1]
    n_rows = qp_ref.shape[1]
    seq_chunks = past // CMP_STRIDE
    n_chunks = nb * seq_chunks
    cpp = page // CMP_STRIDE
    n_lp = CMP_STRIDE // 2

    def chunk_pairs(slot):
        gathered = []
        for pages in pages_all:
            for pg in pages:
                xt = pg[slot].reshape(NSA_GROUPS * HEAD_DIM, page).astype(BF16)
                gathered.append(_dot_nt(perm_ref[...], xt))
        half = n_lp * cpp
        return [jnp.concatenate(
            [jnp.concatenate([x[lp * cpp:(lp + 1) * cpp], x[half + lp * cpp:half + (lp + 1) * cpp]], axis=1)
             for x in gathered], axis=0) for lp in range(n_lp)]

    pairs = [chunk_pairs(slot) for slot in range(2)]
    kc_all, vc_all = _compress_rows(lambda slot, lp: pairs[slot][lp], n_chunks, wc_ref, pe_ref, w2_ref, seq_chunks)
    streams = [_nsa_decode_one(bb, kc_all[bb * seq_chunks:(bb + 1) * seq_chunks],
                               vc_all[bb * seq_chunks:(bb + 1) * seq_chunks], pages_all[bb], qp_ref, qr_ref, ng_ref,
                               kvn_ref, winn_ref, wb_ref, ovl_ref, gg_ref, selr_ref, exp_ref, o_ref, wo_ref,
                               n_blocks, past) for bb in range(nb)]
    while streams:
        streams = [st for st in streams if next(st, "done") is None]


def _nsa_decode_one(bb, kc, vc, sel_pages, qp_ref, qr_ref, ng_ref, kvn_ref, winn_ref, wb_ref, ovl_ref, gg_ref,
                    selr_ref, exp_ref, o_ref, wo_ref, n_blocks, past):
    page = sel_pages[0].shape[-1]
    wbuf = wb_ref.shape[-1]
    n_rows = qp_ref.shape[1]
    row8 = lax.broadcasted_iota(jnp.int32, (n_rows, LANES), 0)
    lane8 = lax.broadcasted_iota(jnp.int32, (n_rows, LANES), 1)
    own = (lane8 // HEAD_DIM) == (row8 // NSA_HPG)

    def by_group(x0, x1):
        return jnp.where(lax.broadcasted_iota(jnp.int32, x0.shape, 0) < NSA_HPG, x0, x1)

    qp = qp_ref[bb] * SCALE
    qr = qr_ref[bb] * SCALE
    qrb = qr.astype(BF16)
    kvn = kvn_ref[bb]
    winn = winn_ref[bb]

    def new_key_scores(krow):
        prod = jnp.concatenate([qr, qr], axis=1) * krow
        return jnp.sum(jnp.where(own, prod, 0.0), axis=-1, keepdims=True)

    def new_value(vrow):
        v = jnp.broadcast_to(vrow, (n_rows, LANES))
        return by_group(v[:, :HEAD_DIM], v[:, HEAD_DIM:])

    nc = kc.shape[0]
    qp_pair = jnp.where(own, jnp.concatenate([qp, qp], axis=1), 0.0)
    s_cmp = _dot_nt(qp_pair.astype(BF16), kc.astype(BF16))
    s_sel = [by_group(_dot(qrb, pg[2, 0].astype(BF16)), _dot(qrb, pg[2, 1].astype(BF16))) for pg in sel_pages]
    s_win = by_group(_dot(qrb, wb_ref[bb, 0, 0].astype(BF16)), _dot(qrb, wb_ref[bb, 0, 1].astype(BF16)))
    pieces = _split3(jnp.broadcast_to(_sigmoid(ng_ref[bb]), (LANES, LANES)))
    gates = [sum(_dot_nt(selr_ref[r], piece) for piece in pieces)[:, :HEAD_DIM] for r in range(3)]
    yield

    cblk = lax.broadcasted_iota(jnp.int32, (n_rows, nc), 1)
    valid = cblk * CMP_STRIDE + (CMP_LEN - 1) <= past
    s = jnp.where(valid, s_cmp, NEG)
    p = jnp.where(valid, jnp.exp(s - jnp.max(s, axis=-1, keepdims=True)), 0.0)
    p = p / jnp.maximum(jnp.sum(p, axis=-1, keepdims=True), 1e-30)
    o = _dot(p.astype(BF16), vc.astype(BF16))
    imp_h = sum(_dot_nt(piece, ovl_ref[...]) for piece in _split3(p))
    ps, p_new_w = _softmax_rows([s_win], new_key_scores(winn[:, 0:LANES]))
    pb = ps[0].astype(BF16)
    o_win = by_group(_dot_nt(pb, wb_ref[bb, 1, 0].astype(BF16)), _dot_nt(pb, wb_ref[bb, 1, 1].astype(BF16)))
    yield
    o_cmp = by_group(o[:, :HEAD_DIM], o[:, HEAD_DIM:])
    o_win = o_win + p_new_w * new_value(winn[:, LANES:2 * LANES])
    imp_h = jnp.concatenate([imp_h, jnp.zeros((LANES - n_rows, LANES), F32)], axis=0)
    imp = sum(_dot(gg_ref[...], piece) for piece in _split3(imp_h))
    yield

    nbp = -(-n_blocks // 8) * 8
    sel = _select_mask(imp.T[:nbp], jnp.full((1, LANES), past, jnp.int32), n_blocks)
    bias_t = jnp.where(sel, 0.0, NEG)
    bias_t = jnp.concatenate([bias_t, jnp.zeros((LANES - nbp, LANES), F32)], axis=0)
    bias = bias_t.T[:n_rows]
    bias_keys = _dot(bias.astype(BF16), exp_ref[...])
    yield

    parts = [s + bias_keys[:, pi * page:(pi + 1) * page] for pi, s in enumerate(s_sel)]
    blk_new = past // SEL_LEN
    s_new = new_key_scores(kvn[:, 2 * LANES:3 * LANES]) + bias[:, blk_new:blk_new + 1]
    ps, p_new = _softmax_rows(parts, s_new)
    acc0 = jnp.zeros((n_rows, HEAD_DIM), F32)
    acc1 = jnp.zeros((n_rows, HEAD_DIM), F32)
    for pg, p in zip(sel_pages, ps):
        pb = p.astype(BF16)
        acc0 = acc0 + _dot_nt(pb, pg[3, 0].astype(BF16))
        acc1 = acc1 + _dot_nt(pb, pg[3, 1].astype(BF16))
    yield
    o_sel = by_group(acc0, acc1) + p_new * new_value(kvn[:, 3 * LANES:4 * LANES])
    o_ref[bb] = gates[0] * o_cmp + gates[1] * o_sel + gates[2] * o_win

    lane_w = lax.broadcasted_iota(jnp.int32, (HEAD_DIM, LANES), 1)
    n_col = wbuf // LANES
    for kv in range(2):
        tile = jnp.concatenate([winn[:, kv * LANES:(kv + 1) * LANES], jnp.zeros((LANES - 1, LANES), F32)], axis=0)
        new_t = tile.T
        for grp in range(NSA_GROUPS):
            col = new_t[grp * HEAD_DIM:(grp + 1) * HEAD_DIM, 0:1]
            rolled = [pltpu.roll(wb_ref[bb, kv, grp, :, c * LANES:(c + 1) * LANES], LANES - 1, 1)
                      for c in range(n_col)]
            for c in range(n_col):
                nxt = rolled[c + 1] if c + 1 < n_col else jnp.broadcast_to(col, (HEAD_DIM, LANES))
                wo_ref[bb, kv, grp, :, c * LANES:(c + 1) * LANES] = jnp.where(lane_w == LANES - 1, nxt, rolled[c])


def _nsa_decode(page_table, cache_t, layer, wc, pe_pair, w2, qp8, qr8, ng3, kvn3, winn3, win_t, ovl, gg,
                selr, expand, perm, win_prev):
    dec_b, n_pages = page_table.shape
    page = cache_t.shape[-1]
    past = n_pages * page
    wbuf = win_t.shape[-1]
    depth = win_t.shape[1]
    assert wbuf <= WINDOW and wbuf % LANES == 0
    n_blocks = -(-(past + 1) // SEL_LEN)
    nb = NSA_DEC_TILE
    assert dec_b % nb == 0
    full = lambda a: pl.BlockSpec(a.shape, lambda b, pt: (0,) * a.ndim)
    per_b = lambda a: pl.BlockSpec((nb,) + a.shape[1:], lambda b, pt: (b,) + (0,) * (a.ndim - 1))
    wshape = (nb, None, 2, NSA_GROUPS, HEAD_DIM, wbuf)

    def page_spec(bb, p):
        return pl.BlockSpec((None, None) + cache_t.shape[2:], lambda b, pt: (pt[b * nb + bb, p], layer, 0, 0, 0, 0))

    in_specs = [page_spec(bb, p) for bb in range(nb) for p in range(n_pages)] + [
        full(wc), full(pe_pair), full(w2), per_b(qp8), per_b(qr8), per_b(ng3), per_b(kvn3), per_b(winn3),
        pl.BlockSpec(wshape, lambda b, pt: (b, layer, 0, 0, 0, 0)), full(ovl), full(gg), full(selr), full(expand),
        full(perm)]
    args = [page_table] + [cache_t] * (nb * n_pages) + [
        wc, pe_pair, w2, qp8, qr8, ng3, kvn3, winn3, win_t, ovl, gg, selr, expand, perm]
    aliases = {}
    if win_prev is not None:
        in_specs.append(pl.BlockSpec(memory_space=pl.ANY))
        aliases = {len(args): 1}
        args.append(win_prev)
    grid_spec = pltpu.PrefetchScalarGridSpec(
        num_scalar_prefetch=1,
        grid=(dec_b // nb,),
        in_specs=in_specs,
        out_specs=[per_b(qp8), pl.BlockSpec(wshape, lambda b, pt: (b, layer, 0, 0, 0, 0))])
    return pl.pallas_call(
        functools.partial(_nsa_decode_kernel, n_pages=n_pages, n_blocks=n_blocks, past=past,
                          has_prev=win_prev is not None),
        name="nsa_decode",
        grid_spec=grid_spec,
        out_shape=[jax.ShapeDtypeStruct(qp8.shape, F32),
                   jax.ShapeDtypeStruct((dec_b, depth, 2, NSA_GROUPS, HEAD_DIM, wbuf), F32)],
        input_output_aliases=aliases,
        compiler_params=_cparams(1),
    )(*args)


def _gla_decode_kernel(q_ref, k_ref, la_ref, v_ref, s_ref, o_ref, so_ref, qt_ref, kt_ref, at_ref):
    i = pl.program_id(0)
    bt = GLA_DEC_TILE
    n_tiles = qt_ref.shape[0]

    @pl.when(i == 0)
    def _():
        qt = q_ref[...].T
        kt = k_ref[...].T
        at = jnp.exp(la_ref[...]).T
        for j in range(n_tiles):
            qt_ref[j] = qt[:, j * bt:(j + 1) * bt]
            kt_ref[j] = kt[:, j * bt:(j + 1) * bt]
            at_ref[j] = at[:, j * bt:(j + 1) * bt]

    qt = qt_ref[i]
    kt = kt_ref[i]
    at = at_ref[i]
    for bb in range(bt):
        for head in range(GLA_HEADS):
            rows = slice(head * HEAD_DIM, (head + 1) * HEAD_DIM)
            v = v_ref[bb:bb + 1, head * LANES:(head + 1) * LANES]
            st = at[rows, bb:bb + 1] * s_ref[bb, 0, head] + kt[rows, bb:bb + 1] * v
            so_ref[bb, head] = st
            o_ref[bb:bb + 1, head * LANES:(head + 1) * LANES] = jnp.sum(qt[rows, bb:bb + 1] * st, axis=0,
                                                                        keepdims=True)


def _gla_decode(gq_s, gk_s, la_s, gv_s, state, layer):
    dec_b, dk4 = gq_s.shape
    bt = GLA_DEC_TILE
    n_tiles = dec_b // bt
    full = lambda a: pl.BlockSpec(a.shape, lambda i: (0,) * a.ndim)
    sblk = (bt, None, GLA_HEADS, HEAD_DIM, LANES)
    return pl.pallas_call(
        _gla_decode_kernel,
        name="gla_decode",
        grid=(n_tiles,),
        in_specs=[full(gq_s), full(gk_s), full(la_s), pl.BlockSpec((bt, gv_s.shape[1]), lambda i: (i, 0)),
                  pl.BlockSpec((bt, 1, GLA_HEADS, HEAD_DIM, LANES), lambda i: (i, layer, 0, 0, 0))],
        out_specs=[pl.BlockSpec((bt, gv_s.shape[1]), lambda i: (i, 0)),
                   pl.BlockSpec((bt, GLA_HEADS, HEAD_DIM, LANES), lambda i: (i, 0, 0, 0))],
        out_shape=[jax.ShapeDtypeStruct(gv_s.shape, F32),
                   jax.ShapeDtypeStruct((dec_b, GLA_HEADS, HEAD_DIM, LANES), F32)],
        scratch_shapes=[pltpu.VMEM((n_tiles, dk4, bt), F32)] * 3,
        compiler_params=_cparams(1),
    )(gq_s, gk_s, la_s, gv_s, state)


def _overlap_t():
    r, w = SEL_LEN // CMP_STRIDE, CMP_LEN // CMP_STRIDE
    off = (np.arange(r)[:, None] + np.arange(w)[None, :]).reshape(-1)
    j = np.arange(LANES)
    c = np.arange(LANES)
    ov = np.sum(c[None, :, None] == (r * j[:, None, None] + off[None, None, :]), axis=-1)
    return ov.astype(np.float32)


def _rope_tables(pos):
    half = HEAD_DIM // 2
    inv = ROPE_THETA ** (-jnp.arange(half, dtype=F32) / half)
    ang = pos[:, None] * inv[None, :]
    cos, sin = jnp.cos(ang), jnp.sin(ang)
    cs = jnp.concatenate([cos, cos, cos, cos], axis=1)
    sn = jnp.concatenate([-sin, sin, -sin, sin], axis=1)
    return cs, sn


def _compress_weights(pe, w1, w2):
    dh = HEAD_DIM
    w1r = w1.reshape(CMP_LEN, dh, -1)
    hid = w1r.shape[-1]
    z = jnp.zeros((dh, hid), w1.dtype)
    groups = []
    for g in range(NSA_GROUPS):
        mats = []
        for lp in range(CMP_STRIDE // 2):
            halves = []
            for base in (0, CMP_STRIDE):
                blocks = []
                for l in (2 * lp, 2 * lp + 1):
                    blocks += [w1r[base + l], z] if g == 0 else [z, w1r[base + l]]
                halves.append(jnp.concatenate(blocks, axis=0))
            mats.append(jnp.concatenate(halves, axis=1))
        groups.append(jnp.stack(mats))
    wc = jnp.stack(groups).astype(BF16)
    pe_flat = jnp.broadcast_to(pe.reshape(1, -1), (8, pe.size))
    return wc, pe_flat, w2.astype(BF16)


def kernel(x_prompt, x_sample, cache_nsa_kv, cache_win_kv, state_gla, page_table, ln_in_g, ln_in_b, w_in, cmp_k_pe, cmp_k_w1, cmp_k_w2, cmp_v_pe, cmp_v_w1, cmp_v_w2, gla_w_a2, gla_b_a, gla_norm_g, w_nsa_up, w_gla_up, w_out, ln1_g, ln1_b, mlp_w1, mlp_w2, ln2_g, ln2_b):
    batch, seq, dm = x_prompt.shape
    dec_b = x_sample.shape[0]
    depth = w_in.shape[0]
    n_phys, _, page = cache_nsa_kv.shape[:3]
    past = page_table.shape[1] * page
    wbuf = cache_win_kv.shape[2]
    alpha = (2.0 * depth) ** 0.25
    n_p = batch * seq
    qw = NSA_HEADS * HEAD_DIM

    h_p = x_prompt.reshape(n_p, dm)
    h_s = x_sample.reshape(dec_b, dm)
    ln_in = (ln_in_g[None, :], ln_in_b[None, :])

    cs_p, sn_p = _rope_tables(jnp.arange(seq, dtype=F32))
    cs_s, sn_s = _rope_tables(jnp.full((dec_b,), past, F32))
    ovl = jnp.asarray(_overlap_t())
    wlev = jnp.asarray(_gla_level_matrix(GLA_CHUNK)).astype(BF16)
    col = np.arange(LANES)
    gg = jnp.asarray(((col[:, None] // NSA_HPG == col[None, :] // NSA_HPG)
                      & (col[:, None] < NSA_HEADS) & (col[None, :] < NSA_HEADS)).astype(np.float32)).astype(BF16)
    selr = jnp.asarray(np.stack([col[None, :] == np.arange(NSA_HEADS)[:, None] * 3 + r
                                 for r in range(3)]).astype(np.float32)).astype(BF16)
    cache_t = jnp.transpose(cache_nsa_kv, (0, 1, 3, 4, 5, 2))
    win_t = jnp.transpose(cache_win_kv, (0, 1, 3, 4, 5, 2))
    expand = jnp.asarray((np.arange(past)[None, :] // SEL_LEN == col[:, None]).astype(np.float32)).astype(BF16)
    r_idx = np.arange(page)[None, :]
    m_idx = np.arange(page // 2)[:, None]
    cpp = page // CMP_STRIDE
    perm = jnp.asarray(np.concatenate([r_idx == (m_idx % cpp) * CMP_STRIDE + 2 * (m_idx // cpp) + par
                                       for par in range(2)]).astype(np.float32)).astype(BF16)
    win_buf = None

    sizes = (qw, 6 * NSA_GROUPS * HEAD_DIM, 3 * NSA_HEADS, GLA_HEADS * HEAD_DIM, GLA_HEADS * HEAD_DIM,
             GLA_HEADS * LANES, GLA_RANK, GLA_HEADS * LANES, 2 * dm)
    pts = np.concatenate([[0], np.cumsum(sizes)])
    seg = lambda w, i: w[:, pts[i]:pts[i + 1]]

    t_bufs = None
    gla_p, kv_s, gla_s = [], [], []
    for l in range(depth):
        wl = w_in[l]
        misc = jnp.concatenate([seg(wl, 2), seg(wl, 6), jnp.zeros((dm, LANES - 3 * NSA_HEADS - GLA_RANK), F32)], 1)
        w_a = jnp.concatenate([seg(wl, 0), seg(wl, 1), seg(wl, 3), seg(wl, 4), seg(wl, 5), misc], 1).astype(BF16)
        w_b = jnp.concatenate([seg(wl, 7), seg(wl, 8)], axis=1).astype(BF16)
        wa_pad = jnp.zeros((LANES, GLA_HEADS * HEAD_DIM), F32).at[3 * NSA_HEADS:3 * NSA_HEADS + GLA_RANK].set(
            gla_w_a2[l]).astype(BF16)
        ba = gla_b_a[l][None, :]
        w_nu, w_gu, w_o = w_nsa_up[l].astype(BF16), w_gla_up[l].astype(BF16), w_out[l].astype(BF16)
        w1, w2 = mlp_w1[l].astype(BF16), mlp_w2[l].astype(BF16)
        gn, g1, b1, g2, b2 = (gla_norm_g[l][None, :], ln1_g[l][None, :], ln1_b[l][None, :], ln2_g[l][None, :],
                              ln2_b[l][None, :])

        wck, pek, w2k = _compress_weights(cmp_k_pe[l], cmp_k_w1[l], cmp_k_w2[l])
        wcv, pev, w2v = _compress_weights(cmp_v_pe[l], cmp_v_w1[l], cmp_v_w2[l])
        wc = jnp.stack([wck, wcv])
        pe_pair = _pe_bias(jnp.stack([pek, pev]), jnp.stack([cmp_k_w1[l], cmp_v_w1[l]]))
        w2c = jnp.stack([w2k, w2v])

        qp, qr, kv, win, gq, gk, gv, la, ng, kvt, wint = _inproj(h_p, w_a, cs_p, sn_p, wa_pad, ba, (l, depth), t_bufs,
                                                                 ln_in if l == 0 else None)
        t_bufs = (kvt, wint)
        kc, vc = _compress_prompt(kv, wc, pe_pair, w2c, batch, seq)
        o_nsa = _nsa_prompt(qp, qr, ng, kc, vc, kv, win, kvt, wint, ovl, batch, seq, l)
        o_gla, st_p = _gla_prompt(gq, gk, gv, la, wlev, batch, seq)
        gla_p.append(st_p)
        h_p = _outproj(h_p, tuple(o_nsa), o_gla, w_b, gn, w_nu, w_gu, w_o, g1, b1, alpha, seq,
                       ln_in if l == 0 else None)
        h_p = _mlp(h_p, w1, w2, g2, b2, alpha)

        qp, qr, kv, win, gq, gk, gv, la, ng = _inproj(h_s, w_a, cs_s, sn_s, wa_pad, ba,
                                                      ln=ln_in if l == 0 else None)
        o8, win_buf = _nsa_decode(page_table, cache_t, l, wc, pe_pair, w2c,
                                  qp.reshape(dec_b, NSA_HEADS, HEAD_DIM), qr.reshape(dec_b, NSA_HEADS, HEAD_DIM),
                                  ng[:, None, :], kv[:, None, :], win[:, None, :], win_t, ovl, gg, selr,
                                  expand, perm, win_buf)
        o_gla, st_s = _gla_decode(gq, gk, la, gv, state_gla, l)
        kv_s.append(kv.reshape(dec_b, 1, 4, NSA_GROUPS, HEAD_DIM))
        gla_s.append(st_s)
        h_s = _outproj(h_s, o8.reshape(dec_b, qw), o_gla, w_b, gn, w_nu, w_gu, w_o, g1, b1, alpha,
                       ln=ln_in if l == 0 else None)
        h_s = _mlp(h_s, w1, w2, g2, b2, alpha)

    kvt, wint = t_bufs
    wn = min(WINDOW, seq)
    return (h_p.reshape(batch, seq, dm), h_s.reshape(dec_b, 1, dm),
            jnp.transpose(kvt, (0, 1, 5, 2, 3, 4)), jnp.transpose(wint[..., seq - wn:], (0, 1, 5, 2, 3, 4)),
            jnp.stack(gla_p, axis=1), jnp.stack(kv_s, axis=1), jnp.transpose(win_buf, (0, 1, 5, 2, 3, 4)),
            jnp.stack(gla_s, axis=1))
```

```python
import functools
import math

import numpy as np
import jax
import jax.numpy as jnp
from jax import lax
from jax.experimental import pallas as pl
from jax.experimental.pallas import tpu as pltpu

F32 = jnp.float32
BF16 = jnp.bfloat16

NSA_HEADS = 8
NSA_GROUPS = 2
NSA_HPG = NSA_HEADS // NSA_GROUPS
HEAD_DIM = 64
SCALE = HEAD_DIM ** -0.5
CMP_LEN = 32
CMP_STRIDE = 16
SEL_LEN = 64
N_SEL = 16
WINDOW = 512
GLA_HEADS = 4
GLA_RANK = 16
GLA_TAU = 16.0
ROPE_THETA = 10000.0
EPS = 1e-5
BIG = 1e6
NEG = -1e30
LOG2E = math.log2(math.e)

LANES = 128
ROW_TILE = 256
MLP_ROW_TILE = 512
Q_TILE = 128
KV_TILE = 256
GLA_CHUNK = 256
GLA_DEC_TILE = 16
NSA_DEC_TILE = 4
VMEM_LIMIT = 56 * 1024 * 1024


def _cparams(n_axes):
    return pltpu.CompilerParams(dimension_semantics=("arbitrary",) * n_axes,
                                vmem_limit_bytes=VMEM_LIMIT)


def _dot(a, b):
    return jnp.dot(a, b, preferred_element_type=F32)


def _dot_nt(a, b):
    return lax.dot_general(a, b, (((1,), (1,)), ((), ())), preferred_element_type=F32)


def _split3(x):
    hi = x.astype(BF16)
    r = x - hi.astype(F32)
    mid = r.astype(BF16)
    lo = (r - mid.astype(F32)).astype(BF16)
    return hi, mid, lo


def _layer_norm(x, g, b):
    mu = jnp.mean(x, axis=-1, keepdims=True)
    xc = x - mu
    var = jnp.mean(xc * xc, axis=-1, keepdims=True)
    return xc * lax.rsqrt(var + EPS) * g + b


def _sigmoid(x):
    return 1.0 / (1.0 + jnp.exp(-x))


def _row_tile(n, tile=ROW_TILE):
    return tile if n % tile == 0 else n


C_Q, C_KV, C_GQ, C_GK, C_GV, C_MISC, C_END = 0, 512, 1280, 1536, 1792, 2304, 2432


def _rope128(x, cs, sn):
    lane = lax.broadcasted_iota(jnp.int32, x.shape, 1)
    first = (lane % HEAD_DIM) < (HEAD_DIM // 2)
    swapped = jnp.where(first, pltpu.roll(x, LANES - HEAD_DIM // 2, 1), pltpu.roll(x, HEAD_DIM // 2, 1))
    return x * cs + swapped * sn


def _inproj_kernel(h_ref, w_ref, cs_ref, sn_ref, wa_ref, ba_ref, *rest, with_t, pre_ln):
    n_out = 11 if with_t else 9
    qp_ref, qr_ref, kv_ref, win_ref, gq_ref, gk_ref, gv_ref, la_ref, ng_ref = rest[-n_out:][:9]
    kvt_ref, wint_ref = rest[-2:] if with_t else (None, None)
    h = h_ref[...]
    if pre_ln:
        h = _layer_norm(h, rest[0][...], rest[1][...])
    hb = h.astype(BF16)
    cs = cs_ref[...]
    sn = sn_ref[...]

    def seg(lo, hi):
        return _dot(hb, w_ref[:, lo:hi])

    for j2 in range(2):
        q2 = seg(C_Q + j2 * 2 * LANES, C_Q + (j2 + 1) * 2 * LANES)
        for jj in range(2):
            qj = q2[:, jj * LANES:(jj + 1) * LANES]
            j = 2 * j2 + jj
            qp_ref[:, j * LANES:(j + 1) * LANES] = qj
            qr_ref[:, j * LANES:(j + 1) * LANES] = _rope128(qj, cs, sn)
    for s2 in range(3):
        x2 = seg(C_KV + s2 * 2 * LANES, C_KV + (s2 + 1) * 2 * LANES)
        for ss in range(2):
            s = 2 * s2 + ss
            x = x2[:, ss * LANES:(ss + 1) * LANES]
            if s in (2, 4):
                x = _rope128(x, cs, sn)
            if s < 4:
                kv_ref[:, s * LANES:(s + 1) * LANES] = x
            else:
                win_ref[:, (s - 4) * LANES:(s - 3) * LANES] = x
            if with_t:
                xt = x.T.reshape(NSA_GROUPS, HEAD_DIM, x.shape[0])
                if s < 4:
                    kvt_ref[s] = xt
                else:
                    wint_ref[s - 4] = xt
    gq_ref[...] = seg(C_GQ, C_GK) * (HEAD_DIM ** -0.5)
    gk_ref[...] = seg(C_GK, C_GV)
    gv_ref[...] = seg(C_GV, C_MISC)
    misc = seg(C_MISC, C_END)
    ng_ref[...] = misc
    x = _dot(misc.astype(BF16), wa_ref[...]) + ba_ref[...]
    la_ref[...] = (jnp.minimum(x, 0.0) - jnp.log(1.0 + jnp.exp(-jnp.abs(x)))) * (1.0 / GLA_TAU)


def _inproj(h, w_a, cs_tab, sn_tab, wa_pad, ba, t_layout=None, t_prev=None, ln=None):
    n, d = h.shape
    tm = _row_tile(n)
    tab_tiles = cs_tab.shape[0] // tm

    def tab_map(i):
        return (i % tab_tiles, 0)

    row = lambda w: pl.BlockSpec((tm, w), lambda i: (i, 0))
    full = lambda a: pl.BlockSpec(a.shape, lambda i: (0,) * a.ndim)
    widths = (512, 512, 512, 256, 256, 256, 512, 256, 128)
    out_specs = [row(w) for w in widths]
    out_shape = [jax.ShapeDtypeStruct((n, w), F32) for w in widths]
    in_specs = [row(d), full(w_a), pl.BlockSpec((tm, LANES), tab_map), pl.BlockSpec((tm, LANES), tab_map),
                full(wa_pad), full(ba)]
    args = [h, w_a, cs_tab, sn_tab, wa_pad, ba]
    if ln is not None:
        in_specs += [full(ln[0]), full(ln[1])]
        args += list(ln)
    aliases = {}
    if t_layout is not None:
        layer, depth = t_layout
        batch = n // cs_tab.shape[0]
        for n_slots in (4, 2):
            out_specs.append(pl.BlockSpec((None, None, n_slots, NSA_GROUPS, HEAD_DIM, tm),
                                          lambda i: (i // tab_tiles, layer, 0, 0, 0, i % tab_tiles)))
            out_shape.append(jax.ShapeDtypeStruct((batch, depth, n_slots, NSA_GROUPS, HEAD_DIM, cs_tab.shape[0]), F32))
        if t_prev is not None:
            for k, buf in enumerate(t_prev):
                in_specs.append(pl.BlockSpec(memory_space=pl.ANY))
                aliases[len(args)] = len(widths) + k
                args.append(buf)
    return pl.pallas_call(
        functools.partial(_inproj_kernel, with_t=t_layout is not None, pre_ln=ln is not None),
        name="inproj",
        grid=(n // tm,),
        in_specs=in_specs,
        out_specs=out_specs,
        out_shape=out_shape,
        input_output_aliases=aliases,
        compiler_params=_cparams(1),
    )(*args)


def _gelu_tanh(x):
    return 0.5 * x * (1.0 + jnp.tanh(math.sqrt(2.0 / math.pi) * (x + 0.044715 * x * x * x)))


def _compress_rows(chunk_pair, n_chunks, wc_ref, pe_ref, w2_ref, seq_chunks=None):
    seq_chunks = n_chunks if seq_chunks is None else seq_chunks
    outs = []
    for s in range(2):
        acc = [jnp.zeros((n_chunks, 2 * LANES), F32) for _ in range(NSA_GROUPS)]
        for lp in range(CMP_STRIDE // 2):
            a = chunk_pair(s, lp).astype(BF16)
            for g in range(NSA_GROUPS):
                acc[g] = acc[g] + _dot(a, wc_ref[s, g, lp])
        row = lax.broadcasted_iota(jnp.int32, (n_chunks, LANES), 0)
        parts = []
        for g in range(NSA_GROUPS):
            hid = acc[g][:, :LANES] + pltpu.roll(acc[g][:, LANES:], n_chunks - 1, 0) + pe_ref[s]
            parts.append(_dot(_gelu_tanh(hid).astype(BF16), w2_ref[s]))
        out = jnp.concatenate(parts, axis=1)
        outs.append(jnp.where(row % seq_chunks < seq_chunks - 1, out, 0.0))
    return outs


def _pe_bias_kernel(pe_ref, w1_ref, o_ref):
    for s in range(2):
        o_ref[s] = _dot(pe_ref[s].astype(BF16), w1_ref[s].astype(BF16))[0:1]


def _pe_bias(pe_flat, w1):
    hid = w1.shape[2]
    return pl.pallas_call(
        _pe_bias_kernel,
        name="pe_bias",
        out_shape=jax.ShapeDtypeStruct((2, 1, hid), F32),
    )(pe_flat, w1)


def _compress_kernel(xk_ref, xv_ref, wc_ref, pe_ref, w2_ref, kc_ref, vc_ref):
    x_refs = (xk_ref, xv_ref)
    n_chunks = xk_ref.shape[0] // CMP_STRIDE

    def chunk_pair(slot, lp):
        return jnp.concatenate([x_refs[slot][pl.ds(2 * lp, n_chunks, stride=CMP_STRIDE), :],
                                x_refs[slot][pl.ds(2 * lp + 1, n_chunks, stride=CMP_STRIDE), :]], axis=1)

    kc, vc = _compress_rows(chunk_pair, n_chunks, wc_ref, pe_ref, w2_ref)
    kc_ref[0] = kc
    vc_ref[0] = vc


def _compress_prompt(kv, wc, pe_pair, w2, batch, seq):
    nc = seq // CMP_STRIDE
    full = lambda a: pl.BlockSpec(a.shape, lambda b: (0,) * a.ndim)
    return pl.pallas_call(
        _compress_kernel,
        name="compress",
        grid=(batch,),
        in_specs=[pl.BlockSpec((seq, LANES), lambda b: (b, 0)), pl.BlockSpec((seq, LANES), lambda b: (b, 1)),
                  full(wc), full(pe_pair), full(w2)],
        out_specs=[pl.BlockSpec((1, nc, LANES), lambda b: (b, 0, 0))] * 2,
        out_shape=[jax.ShapeDtypeStruct((batch, nc, LANES), F32)] * 2,
        compiler_params=_cparams(1),
    )(kv, kv, wc, pe_pair, w2)


def _select_mask(imp_t, qpos, n_blocks):
    nbp, nq = imp_t.shape
    blk = lax.broadcasted_iota(jnp.int32, (nbp, nq), 0)
    cur = qpos // SEL_LEN
    causal = blk * SEL_LEN <= qpos
    forced = (blk == 0) | (blk == cur) | (blk == cur - 1)
    score = jnp.where(forced, BIG, jnp.where(causal, imp_t, -jnp.inf))
    score = jnp.where(blk < n_blocks, score, -jnp.inf)
    rank = jnp.zeros((nbp, nq), jnp.int32)
    for j in range(n_blocks):
        other = score[j:j + 1, :]
        rank = rank + jnp.where(blk > j, jnp.where(other >= score, 1, 0), jnp.where(other > score, 1, 0))
    return (rank < N_SEL) & (score > -jnp.inf)


V_ROWS = LANES + 16


def _softmax_step(state, s, pv_fn):
    m, acc = state
    m_new = jnp.maximum(m, jnp.max(s, axis=0, keepdims=True))
    alpha = jnp.exp2(m - m_new)
    p = jnp.exp2(s - m_new)
    return m_new, alpha * acc + pv_fn(p.astype(BF16))


def _softmax_init(nq):
    return (jnp.full((1, nq), NEG, F32), jnp.zeros((V_ROWS, nq), F32))


def _nsa_prompt_kernel(qpa_ref, qra_ref, nga_ref, qpb_ref, qrb_ref, ngb_ref, kc_ref, vc_ref, ks_ref, vs_ref,
                       kw_ref, vw_ref, ovl_ref, oa_ref, ob_ref, ksa_ref, vst_ref, kwa_ref, vwt_ref,
                       *, seq, n_blocks):
    g = pl.program_id(1)
    i = pl.program_id(2)
    tq = Q_TILE
    nqt = seq // tq
    n_kt = seq // KV_TILE
    nq = NSA_HPG * tq

    @pl.when((g == 0) & (i == 0))
    def _():
        for kt in range(n_kt):
            rows = pl.ds(kt * KV_TILE, KV_TILE)
            key = kt * KV_TILE + lax.broadcasted_iota(jnp.int32, (KV_TILE, LANES), 0)
            lane = lax.broadcasted_iota(jnp.int32, (KV_TILE, LANES), 1)
            onehot = jnp.where(key // SEL_LEN == lane, 1.0, 0.0).astype(BF16)
            ksa_ref[kt] = jnp.concatenate([ks_ref[rows, :].astype(BF16), onehot], axis=1)
            kwa_ref[kt] = kw_ref[rows, :].astype(BF16)
            cols = slice(kt * KV_TILE, (kt + 1) * KV_TILE)
            ones = jnp.ones((V_ROWS - LANES, KV_TILE), BF16)
            vst_ref[kt] = jnp.concatenate([vs_ref[:, :, cols].reshape(LANES, KV_TILE).astype(BF16), ones], axis=0)
            vwt_ref[kt] = jnp.concatenate([vw_ref[:, :, cols].reshape(LANES, KV_TILE).astype(BF16), ones], axis=0)

    lane_q = lax.broadcasted_iota(jnp.int32, (1, nq), 1) % tq
    lane1 = lax.broadcasted_iota(jnp.int32, (1, tq), 1)
    lane = lax.broadcasted_iota(jnp.int32, (tq, LANES), 1)
    own = (lane // HEAD_DIM) == g
    ovl = ovl_ref[...]
    kc = kc_ref[0].astype(BF16)
    vct = vc_ref[0].T.astype(BF16)
    nc = kc.shape[0]
    nbp = -(-n_blocks // 8) * 8

    def own_rows(x):
        return jnp.where(g == 0, x[:HEAD_DIM], x[HEAD_DIM:])

    def prepare(qp_ref, qr_ref, qs, max_pos):
        def stack_heads(ref, scale):
            parts = []
            for h in range(NSA_HPG):
                qh = ref[:, h * HEAD_DIM:(h + 1) * HEAD_DIM] * scale
                parts.append(jnp.where(own, jnp.concatenate([qh, qh], axis=1), 0.0))
            return parts

        qpos = qs + lane_q
        qp = jnp.concatenate(stack_heads(qp_ref, SCALE), axis=0).astype(BF16)
        s = _dot_nt(kc, qp)
        cblk = lax.broadcasted_iota(jnp.int32, (nc, nq), 0)
        valid = cblk * CMP_STRIDE + (CMP_LEN - 1) <= qpos
        s = jnp.where(valid, s, NEG)
        m = jnp.max(s, axis=0, keepdims=True)
        p = jnp.where(valid, jnp.exp(s - m), 0.0)
        p = p / jnp.maximum(jnp.sum(p, axis=0, keepdims=True), 1e-30)
        o_cmp = own_rows(_dot(vct, p.astype(BF16)))
        psum = p[:, 0:tq]
        for h in range(1, NSA_HPG):
            psum = psum + p[:, h * tq:(h + 1) * tq]
        imp_t = sum(_dot(ovl, piece) for piece in _split3(psum))
        if max_pos // SEL_LEN + 1 <= N_SEL:
            sel = lax.broadcasted_iota(jnp.int32, (nbp, tq), 0) * SEL_LEN <= qs + lane1
        else:
            sel = _select_mask(imp_t[:nbp], qs + lane1, n_blocks)
        bias_t = jnp.where(sel, 0.0, NEG)
        bias = jnp.concatenate([bias_t, jnp.zeros((LANES - nbp, tq), F32)], axis=0).T
        qr_parts = stack_heads(qr_ref, SCALE * LOG2E)
        q_sel = jnp.concatenate([jnp.concatenate([q, bias], axis=1) for q in qr_parts], axis=0).astype(BF16)
        q_win = jnp.concatenate(qr_parts, axis=0).astype(BF16)
        return qpos, o_cmp, q_sel, q_win

    qs_a = i * tq
    qs_b = (nqt - 1 - i) * tq
    qpos_a, o_cmp_a, q_sel_a, q_win_a = prepare(qpa_ref, qra_ref, qs_a, (nqt // 2) * tq - 1)
    qpos_b, o_cmp_b, q_sel_b, q_win_b = prepare(qpb_ref, qrb_ref, qs_b, seq - 1)

    kd_a = qs_a // KV_TILE
    kd_b = qs_b // KV_TILE
    max_kd_a = ((nqt // 2 - 1) * tq) // KV_TILE
    n_sel = n_kt + 1
    n_win = WINDOW // KV_TILE + 1

    krow = lax.broadcasted_iota(jnp.int32, (KV_TILE, nq), 0)
    off_a = qs_a % KV_TILE + lane_q
    off_b = qs_b % KV_TILE + lane_q
    edge_a = jnp.where(krow <= off_a, 0.0, NEG)
    edge_b = jnp.where(krow <= off_b, 0.0, NEG)
    start_a = jnp.where(krow >= off_a, 0.0, NEG)
    start_b = jnp.where(krow >= off_b, 0.0, NEG)

    def sel_task(j):
        if j == 0:
            return 0, q_sel_a, jnp.where(kd_a == 0, edge_a, 0.0)
        if j == n_sel - 1:
            return j - kd_a - 1, q_sel_b, edge_b
        if j > max_kd_a:
            return j - kd_a - 1, q_sel_b, None
        in_a = j <= kd_a
        return (jnp.where(in_a, j, j - kd_a - 1), jnp.where(in_a, q_sel_a, q_sel_b),
                jnp.where(j == kd_a, edge_a, 0.0))

    tasks = []
    for j in range(n_sel):
        kt, q, mask = sel_task(j)
        tasks.append(("sel", j, ksa_ref, vst_ref, kt, q, mask))
    for name, kd, q, edge, start in (("wa", kd_a, q_win_a, edge_a, start_a), ("wb", kd_b, q_win_b, edge_b, start_b)):
        for j in range(n_win):
            kt = kd - (n_win - 1) + j
            inside = jnp.where(kt >= 0, 0.0, NEG)
            if j == n_win - 1:
                mask = edge
            elif j == 0:
                mask = start + inside
            else:
                mask = inside
            tasks.append((name, j, kwa_ref, vwt_ref, jnp.maximum(kt, 0), q, mask))

    def scores(task):
        _, _, k_ref, _, kt, q, _ = task
        return _dot_nt(k_ref[kt], q)

    sel_tasks = tasks[:n_sel]
    win_tasks = tasks[n_sel:]
    tasks = []
    while sel_tasks or win_tasks:
        if sel_tasks:
            tasks.append(sel_tasks.pop(0))
        if win_tasks:
            tasks.append(win_tasks.pop(0))

    results = {}
    states = {}
    s_next = scores(tasks[0])
    for t, task in enumerate(tasks):
        name, j, _, vt_ref, kt, _, mask = task
        s = s_next
        if t + 1 < len(tasks):
            s_next = scores(tasks[t + 1])
        if j == 0:
            states[name] = _softmax_init(nq)
        state = states[name]
        if name == "sel" and 1 <= j <= max_kd_a + 1:
            switch = j == kd_a + 1
            prev = results.get("sa", state)
            results["sa"] = tuple(jnp.where(switch, x, y) for x, y in zip(state, prev))
            state = tuple(jnp.where(switch, x, y) for x, y in zip(_softmax_init(nq), state))
        if mask is not None:
            s = mask + s
        state = _softmax_step(state, s, lambda p: _dot(vt_ref[kt], p))
        states[name] = state
        last = (name == "sel" and j == n_sel - 1) or (name != "sel" and j == n_win - 1)
        if last:
            results["sb" if name == "sel" else name] = state

    def finish(key):
        _, acc = results[key]
        return own_rows(acc[:LANES] / acc[LANES:LANES + 1])

    for ng_ref, o_ref, branches in ((nga_ref, oa_ref, (o_cmp_a, finish("sa"), finish("wa"))),
                                    (ngb_ref, ob_ref, (o_cmp_b, finish("sb"), finish("wb")))):
        ng_t = ng_ref[...].T
        outs = []
        for h in range(NSA_HPG):
            cols = slice(h * tq, (h + 1) * tq)
            tot = jnp.zeros((HEAD_DIM, tq), F32)
            for r, o in enumerate(branches):
                i0 = h * 3 + r
                i1 = (NSA_HPG + h) * 3 + r
                gate = _sigmoid(jnp.where(g == 0, ng_t[i0:i0 + 1], ng_t[i1:i1 + 1]))
                tot = tot + gate * o[:, cols]
            outs.append(tot.T)
        o_ref[...] = jnp.concatenate(outs, axis=1)


def _nsa_prompt(qp, qr, ng, kc, vc, kv, win, kvt, wint, ovl, batch, seq, layer):
    n_blocks = -(-seq // SEL_LEN)
    nqt = seq // Q_TILE
    n_kt = seq // KV_TILE
    assert nqt % 2 == 0 and KV_TILE == 2 * Q_TILE and WINDOW % KV_TILE == 0
    gw = NSA_HPG * HEAD_DIM
    half = nqt // 2
    qa = pl.BlockSpec((Q_TILE, gw), lambda b, g, i: (b * nqt + i, g))
    qb = pl.BlockSpec((Q_TILE, gw), lambda b, g, i: (b * nqt + nqt - 1 - i, g))
    na = pl.BlockSpec((Q_TILE, LANES), lambda b, g, i: (b * nqt + i, 0))
    nb = pl.BlockSpec((Q_TILE, LANES), lambda b, g, i: (b * nqt + nqt - 1 - i, 0))
    cspec = pl.BlockSpec((1,) + kc.shape[1:], lambda b, g, i: (b, 0, 0))
    ospec = pl.BlockSpec((Q_TILE, gw), lambda b, g, i: (b * half + i, g))

    def rowspec(col):
        return pl.BlockSpec((seq, LANES), lambda b, g, i: (b, col))

    def tspec(slot):
        return pl.BlockSpec((None, None, None, NSA_GROUPS, HEAD_DIM, seq), lambda b, g, i: (b, layer, slot, 0, 0, 0))

    oshape = jax.ShapeDtypeStruct((batch * half * Q_TILE, qp.shape[1]), F32)
    return pl.pallas_call(
        functools.partial(_nsa_prompt_kernel, seq=seq, n_blocks=n_blocks),
        name="nsa_prompt",
        grid=(batch, NSA_GROUPS, half),
        in_specs=[qa, qa, na, qb, qb, nb, cspec, cspec, rowspec(2), tspec(3), rowspec(0), tspec(1),
                  pl.BlockSpec(ovl.shape, lambda b, g, i: (0, 0))],
        out_specs=[ospec, ospec],
        out_shape=[oshape, oshape],
        scratch_shapes=[pltpu.VMEM((n_kt, KV_TILE, 2 * LANES), BF16), pltpu.VMEM((n_kt, V_ROWS, KV_TILE), BF16),
                        pltpu.VMEM((n_kt, KV_TILE, LANES), BF16), pltpu.VMEM((n_kt, V_ROWS, KV_TILE), BF16)],
        compiler_params=_cparams(3),
    )(qp, qr, ng, qp, qr, ng, kc, vc, kv, kvt, win, wint, ovl)


def _gla_level_matrix(c):
    t = np.arange(c)[:, None]
    u = np.arange(c)[None, :]
    mats = [(u <= t), (u > t)]
    m = c
    while m >= 2:
        split = (t // m) * m + m // 2
        upper = (t % m) >= m // 2
        mats.append(np.where(upper, (u >= split) & (u <= t), (u > t) & (u < split)))
        m //= 2
    return np.concatenate(mats, axis=0).astype(np.float32)


def _gla_prompt_kernel(q_ref, k_ref, v_ref, la_ref, w_ref, o_ref, st_ref, e_ref, s_ref, *, seq):
    c = GLA_CHUNK
    n_levels = int(math.log2(c))
    s_ref[...] = jnp.zeros_like(s_ref)
    row = lax.broadcasted_iota(jnp.int32, (c, c), 0)
    col = lax.broadcasted_iota(jnp.int32, (c, c), 1)
    rowl = lax.broadcasted_iota(jnp.int32, (c, LANES), 0)
    lanel = lax.broadcasted_iota(jnp.int32, (c, LANES), 1)

    def chunk(ci, _):
        rows = pl.ds(pl.multiple_of(ci * c, c), c)
        la = la_ref[rows, :]
        hi = la.astype(BF16)
        lo = (la - hi.astype(F32)).astype(BF16)
        e_ref[...] = jnp.exp(_dot(w_ref[...], hi) + _dot(w_ref[...], lo))
        for pair in range(GLA_HEADS // 2):
            lanes = slice(pair * LANES, (pair + 1) * LANES)
            q = q_ref[rows, lanes]
            k = k_ref[rows, lanes]
            q0 = (q * e_ref[0:c, lanes]).astype(BF16)
            kdec = k * e_ref[c:2 * c, lanes]
            a_last = e_ref[c - 1:c, lanes]
            qls, kls = [], []
            for lv in range(n_levels):
                m = c >> lv
                x = e_ref[(2 + lv) * c:(3 + lv) * c, lanes]
                upper = (rowl % m) >= (m // 2)
                qls.append(jnp.where(upper, q * x, 0.0).astype(BF16))
                kls.append(jnp.where(upper, 0.0, k * x))
            for hh in range(2):
                head = pair * 2 + hh
                mine = (lanel // HEAD_DIM) == hh
                a = jnp.where(row == col, _dot_nt(q.astype(BF16), jnp.where(mine, k, 0.0).astype(BF16)), 0.0)
                for lv in range(n_levels):
                    m = c >> lv
                    same = (row // m) == (col // m)
                    a = a + jnp.where(same, _dot_nt(qls[lv], jnp.where(mine, kls[lv], 0.0).astype(BF16)), 0.0)
                v = v_ref[rows, head * LANES:(head + 1) * LANES]
                st = s_ref[head]
                o = _dot(a.astype(BF16), v.astype(BF16)) + _dot_nt(q0, st.astype(BF16))
                o_ref[rows, head * LANES:(head + 1) * LANES] = o
                kd = jnp.where(mine, kdec, 0.0).astype(BF16)
                s_ref[head] = st * a_last + _dot(v.T.astype(BF16), kd)
        return 0

    lax.fori_loop(0, seq // c, chunk, 0)
    for head in range(GLA_HEADS):
        st = s_ref[head].T
        off = (head % 2) * HEAD_DIM
        st_ref[0, head] = st[off:off + HEAD_DIM]


def _gla_prompt(gq, gk, gv, la, wlev, batch, seq):
    dk2 = gq.shape[1]
    dv4 = gv.shape[1]
    return pl.pallas_call(
        functools.partial(_gla_prompt_kernel, seq=seq),
        name="gla_prompt",
        grid=(batch,),
        in_specs=[pl.BlockSpec((seq, dk2), lambda b: (b, 0)), pl.BlockSpec((seq, dk2), lambda b: (b, 0)),
                  pl.BlockSpec((seq, dv4), lambda b: (b, 0)), pl.BlockSpec((seq, dk2), lambda b: (b, 0)),
                  pl.BlockSpec(wlev.shape, lambda b: (0, 0))],
        out_specs=[pl.BlockSpec((seq, dv4), lambda b: (b, 0)),
                   pl.BlockSpec((1, GLA_HEADS, HEAD_DIM, LANES), lambda b: (b, 0, 0, 0))],
        out_shape=[jax.ShapeDtypeStruct((gq.shape[0], dv4), F32),
                   jax.ShapeDtypeStruct((batch, GLA_HEADS, HEAD_DIM, LANES), F32)],
        scratch_shapes=[pltpu.VMEM((wlev.shape[0], dk2), F32), pltpu.VMEM((GLA_HEADS, LANES, LANES), F32)],
        compiler_params=_cparams(1),
    )(gq, gk, gv, la, wlev)


def _outproj_kernel(h_ref, *refs, alpha, tiles_per_seq, pre_ln):
    if tiles_per_seq:
        lo_ref, hi_ref = refs[:2]
        refs = refs[2:]
        t = pl.program_id(0) % tiles_per_seq
        hi = hi_ref[...]
        n_q = hi.shape[0] // Q_TILE
        mirrored = jnp.concatenate([hi[k * Q_TILE:(k + 1) * Q_TILE] for k in reversed(range(n_q))], axis=0)
        on = jnp.where(t < tiles_per_seq // 2, lo_ref[...], mirrored)
    else:
        on = refs[0][...]
        refs = refs[1:]
    og_ref, wb_ref, gn_ref, wn_ref, wg_ref, wo_ref, g1_ref, b1_ref = refs[:8]
    o_ref = refs[-1]
    h = h_ref[...]
    if pre_ln:
        h = _layer_norm(h, refs[8][...], refs[9][...])
    hb = h.astype(BF16)
    dm = h.shape[1]
    gw = og_ref.shape[1]
    g_r = _dot(hb, wb_ref[:, :gw])
    parts = []
    for head in range(GLA_HEADS):
        x = og_ref[:, head * LANES:(head + 1) * LANES]
        x = x * lax.rsqrt(jnp.mean(x * x, axis=-1, keepdims=True) + EPS) * gn_ref[...]
        gr = g_r[:, head * LANES:(head + 1) * LANES]
        parts.append(x * (gr * _sigmoid(gr)))
    og = jnp.concatenate(parts, axis=1).astype(BF16)
    a = _dot(on.astype(BF16), wn_ref[...])
    c = _dot(og, wg_ref[...])
    m_a = _dot(hb, wb_ref[:, gw:gw + dm])
    m_c = _dot(hb, wb_ref[:, gw + dm:gw + 2 * dm])
    mix = (_sigmoid(m_a) * a + _sigmoid(m_c) * c).astype(BF16)
    y = _dot(mix, wo_ref[...])
    o_ref[...] = _layer_norm(alpha * h + y, g1_ref[...], b1_ref[...])


def _outproj(h, o_nsa, o_gla, w_b, gn, w_nsa, w_gla, w_out, g1, b1, alpha, seq=None, ln=None):
    n, d = h.shape
    tm = _row_tile(n, MLP_ROW_TILE)
    row = lambda w: pl.BlockSpec((tm, w), lambda i: (i, 0))
    full = lambda a: pl.BlockSpec(a.shape, lambda i: (0,) * a.ndim)
    if isinstance(o_nsa, tuple):
        assert tm % Q_TILE == 0 and (seq // 2) % tm == 0
        tps = seq // tm
        hs = tps // 2
        qw = o_nsa[0].shape[1]
        lo_spec = pl.BlockSpec((tm, qw), lambda i: (i // tps * hs + jnp.minimum(i % tps, hs - 1), 0))
        hi_spec = pl.BlockSpec((tm, qw), lambda i: (i // tps * hs + jnp.clip(tps - 1 - i % tps, 0, hs - 1), 0))
        nsa_specs, nsa_args = [lo_spec, hi_spec], list(o_nsa)
    else:
        tps = 0
        nsa_specs, nsa_args = [row(o_nsa.shape[1])], [o_nsa]
    ln_args = [] if ln is None else list(ln)
    return pl.pallas_call(
        functools.partial(_outproj_kernel, alpha=alpha, tiles_per_seq=tps, pre_ln=ln is not None),
        name="outproj",
        grid=(n // tm,),
        in_specs=[row(d)] + nsa_specs + [row(o_gla.shape[1]), full(w_b), full(gn), full(w_nsa),
                                         full(w_gla), full(w_out), full(g1), full(b1)] + [full(a) for a in ln_args],
        out_specs=row(d),
        out_shape=jax.ShapeDtypeStruct((n, d), F32),
        compiler_params=_cparams(1),
    )(h, *nsa_args, o_gla, w_b, gn, w_nsa, w_gla, w_out, g1, b1, *ln_args)


def _mlp_kernel(h_ref, w1_ref, w2_ref, g_ref, b_ref, o_ref, *, alpha):
    h = h_ref[...]
    hb = h.astype(BF16)
    dff = w1_ref.shape[1]
    step = 1024
    f = jnp.zeros(h.shape, F32)
    for c0 in range(0, dff, step):
        u = jnp.maximum(_dot(hb, w1_ref[:, c0:c0 + step]), 0.0)
        f = f + _dot((u * u).astype(BF16), w2_ref[c0:c0 + step, :])
    o_ref[...] = _layer_norm(alpha * h + f, g_ref[...], b_ref[...])


def _mlp(h, w1, w2, g, b, alpha):
    n, d = h.shape
    tm = _row_tile(n, MLP_ROW_TILE)
    row = pl.BlockSpec((tm, d), lambda i: (i, 0))
    full = lambda a: pl.BlockSpec(a.shape, lambda i: (0,) * a.ndim)
    return pl.pallas_call(
        functools.partial(_mlp_kernel, alpha=alpha),
        name="mlp",
        grid=(n // tm,),
        in_specs=[row, full(w1), full(w2), full(g), full(b)],
        out_specs=row,
        out_shape=jax.ShapeDtypeStruct((n, d), F32),
        compiler_params=_cparams(1),
    )(h, w1, w2, g, b)


def _softmax_rows(parts, extra, valid_extra=None):
    m = extra
    for s in parts:
        m = jnp.maximum(m, jnp.max(s, axis=-1, keepdims=True))
    ps = [jnp.exp(s - m) for s in parts]
    pe = jnp.exp(extra - m)
    tot = pe
    for p in ps:
        tot = tot + jnp.sum(p, axis=-1, keepdims=True)
    inv = 1.0 / tot
    return [p * inv for p in ps], pe * inv


def _nsa_decode_kernel(pt_ref, *refs, n_pages, n_blocks, past, has_prev):
    del pt_ref
    nb = NSA_DEC_TILE
    pages_all = [refs[bb * n_pages:(bb + 1) * n_pages] for bb in range(nb)]
    rest = refs[nb * n_pages:]
    (wc_ref, pe_ref, w2_ref, qp_ref, qr_ref, ng_ref, kvn_ref, winn_ref, wb_ref, ovl_ref, gg_ref, selr_ref,
     exp_ref, perm_ref) = rest[:14]
    o_ref, wo_ref = rest[-2:]
    page = pages_all[0][0].shape[-1]
    wbuf = wb_ref.shape[-1]
    n_rows = qp_ref.shape[1]
    seq_chunks = past // CMP_STRIDE
    n_chunks = nb * seq_chunks
    cpp = page // CMP_STRIDE
    n_lp = CMP_STRIDE // 2

    def chunk_pairs(slot):
        gathered = []
        for pages in pages_all:
            for pg in pages:
                xt = pg[slot].reshape(NSA_GROUPS * HEAD_DIM, page).astype(BF16)
                gathered.append(_dot_nt(perm_ref[...], xt))
        half = n_lp * cpp
        return [jnp.concatenate(
            [jnp.concatenate([x[lp * cpp:(lp + 1) * cpp], x[half + lp * cpp:half + (lp + 1) * cpp]], axis=1)
             for x in gathered], axis=0) for lp in range(n_lp)]

    pairs = [chunk_pairs(slot) for slot in range(2)]
    kc_all, vc_all = _compress_rows(lambda slot, lp: pairs[slot][lp], n_chunks, wc_ref, pe_ref, w2_ref, seq_chunks)
    streams = [_nsa_decode_one(bb, kc_all[bb * seq_chunks:(bb + 1) * seq_chunks],
                               vc_all[bb * seq_chunks:(bb + 1) * seq_chunks], pages_all[bb], qp_ref, qr_ref, ng_ref,
                               kvn_ref, winn_ref, wb_ref, ovl_ref, gg_ref, selr_ref, exp_ref, o_ref, wo_ref,
                               n_blocks, past) for bb in range(nb)]
    while streams:
        streams = [st for st in streams if next(st, "done") is None]


def _nsa_decode_one(bb, kc, vc, sel_pages, qp_ref, qr_ref, ng_ref, kvn_ref, winn_ref, wb_ref, ovl_ref, gg_ref,
                    selr_ref, exp_ref, o_ref, wo_ref, n_blocks, past):
    page = sel_pages[0].shape[-1]
    wbuf = wb_ref.shape[-1]
    n_rows = qp_ref.shape[1]
    row8 = lax.broadcasted_iota(jnp.int32, (n_rows, LANES), 0)
    lane8 = lax.broadcasted_iota(jnp.int32, (n_rows, LANES), 1)
    own = (lane8 // HEAD_DIM) == (row8 // NSA_HPG)

    def by_group(x0, x1):
        return jnp.where(lax.broadcasted_iota(jnp.int32, x0.shape, 0) < NSA_HPG, x0, x1)

    qp = qp_ref[bb] * SCALE
    qr = qr_ref[bb] * SCALE
    qrb = qr.astype(BF16)
    kvn = kvn_ref[bb]
    winn = winn_ref[bb]

    def new_key_scores(krow):
        prod = jnp.concatenate([qr, qr], axis=1) * krow
        return jnp.sum(jnp.where(own, prod, 0.0), axis=-1, keepdims=True)

    def new_value(vrow):
        v = jnp.broadcast_to(vrow, (n_rows, LANES))
        return by_group(v[:, :HEAD_DIM], v[:, HEAD_DIM:])

    nc = kc.shape[0]
    qp_pair = jnp.where(own, jnp.concatenate([qp, qp], axis=1), 0.0)
    s_cmp = _dot_nt(qp_pair.astype(BF16), kc.astype(BF16))
    s_sel = [by_group(_dot(qrb, pg[2, 0].astype(BF16)), _dot(qrb, pg[2, 1].astype(BF16))) for pg in sel_pages]
    s_win = by_group(_dot(qrb, wb_ref[bb, 0, 0].astype(BF16)), _dot(qrb, wb_ref[bb, 0, 1].astype(BF16)))
    pieces = _split3(jnp.broadcast_to(_sigmoid(ng_ref[bb]), (LANES, LANES)))
    gates = [sum(_dot_nt(selr_ref[r], piece) for piece in pieces)[:, :HEAD_DIM] for r in range(3)]
    yield

    cblk = lax.broadcasted_iota(jnp.int32, (n_rows, nc), 1)
    valid = cblk * CMP_STRIDE + (CMP_LEN - 1) <= past
    s = jnp.where(valid, s_cmp, NEG)
    p = jnp.where(valid, jnp.exp(s - jnp.max(s, axis=-1, keepdims=True)), 0.0)
    p = p / jnp.maximum(jnp.sum(p, axis=-1, keepdims=True), 1e-30)
    o = _dot(p.astype(BF16), vc.astype(BF16))
    imp_h = sum(_dot_nt(piece, ovl_ref[...]) for piece in _split3(p))
    ps, p_new_w = _softmax_rows([s_win], new_key_scores(winn[:, 0:LANES]))
    pb = ps[0].astype(BF16)
    o_win = by_group(_dot_nt(pb, wb_ref[bb, 1, 0].astype(BF16)), _dot_nt(pb, wb_ref[bb, 1, 1].astype(BF16)))
    yield
    o_cmp = by_group(o[:, :HEAD_DIM], o[:, HEAD_DIM:])
    o_win = o_win + p_new_w * new_value(winn[:, LANES:2 * LANES])
    imp_h = jnp.concatenate([imp_h, jnp.zeros((LANES - n_rows, LANES), F32)], axis=0)
    imp = sum(_dot(gg_ref[...], piece) for piece in _split3(imp_h))
    yield

    nbp = -(-n_blocks // 8) * 8
    sel = _select_mask(imp.T[:nbp], jnp.full((1, LANES), past, jnp.int32), n_blocks)
    bias_t = jnp.where(sel, 0.0, NEG)
    bias_t = jnp.concatenate([bias_t, jnp.zeros((LANES - nbp, LANES), F32)], axis=0)
    bias = bias_t.T[:n_rows]
    bias_keys = _dot(bias.astype(BF16), exp_ref[...])
    yield

    parts = [s + bias_keys[:, pi * page:(pi + 1) * page] for pi, s in enumerate(s_sel)]
    blk_new = past // SEL_LEN
    s_new = new_key_scores(kvn[:, 2 * LANES:3 * LANES]) + bias[:, blk_new:blk_new + 1]
    ps, p_new = _softmax_rows(parts, s_new)
    acc0 = jnp.zeros((n_rows, HEAD_DIM), F32)
    acc1 = jnp.zeros((n_rows, HEAD_DIM), F32)
    for pg, p in zip(sel_pages, ps):
        pb = p.astype(BF16)
        acc0 = acc0 + _dot_nt(pb, pg[3, 0].astype(BF16))
        acc1 = acc1 + _dot_nt(pb, pg[3, 1].astype(BF16))
    yield
    o_sel = by_group(acc0, acc1) + p_new * new_value(kvn[:, 3 * LANES:4 * LANES])
    o_ref[bb] = gates[0] * o_cmp + gates[1] * o_sel + gates[2] * o_win

    lane_w = lax.broadcasted_iota(jnp.int32, (HEAD_DIM, LANES), 1)
    n_col = wbuf // LANES
    for kv in range(2):
        tile = jnp.concatenate([winn[:, kv * LANES:(kv + 1) * LANES], jnp.zeros((LANES - 1, LANES), F32)], axis=0)
        new_t = tile.T
        for grp in range(NSA_GROUPS):
            col = new_t[grp * HEAD_DIM:(grp + 1) * HEAD_DIM, 0:1]
            rolled = [pltpu.roll(wb_ref[bb, kv, grp, :, c * LANES:(c + 1) * LANES], LANES - 1, 1)
                      for c in range(n_col)]
            for c in range(n_col):
                nxt = rolled[c + 1] if c + 1 < n_col else jnp.broadcast_to(col, (HEAD_DIM, LANES))
                wo_ref[bb, kv, grp, :, c * LANES:(c + 1) * LANES] = jnp.where(lane_w == LANES - 1, nxt, rolled[c])


def _nsa_decode(page_table, cache_t, layer, wc, pe_pair, w2, qp8, qr8, ng3, kvn3, winn3, win_t, ovl, gg,
                selr, expand, perm, win_prev):
    dec_b, n_pages = page_table.shape
    page = cache_t.shape[-1]
    past = n_pages * page
    wbuf = win_t.shape[-1]
    depth = win_t.shape[1]
    assert wbuf <= WINDOW and wbuf % LANES == 0
    n_blocks = -(-(past + 1) // SEL_LEN)
    nb = NSA_DEC_TILE
    assert dec_b % nb == 0
    full = lambda a: pl.BlockSpec(a.shape, lambda b, pt: (0,) * a.ndim)
    per_b = lambda a: pl.BlockSpec((nb,) + a.shape[1:], lambda b, pt: (b,) + (0,) * (a.ndim - 1))
    wshape = (nb, None, 2, NSA_GROUPS, HEAD_DIM, wbuf)

    def page_spec(bb, p):
        return pl.BlockSpec((None, None) + cache_t.shape[2:], lambda b, pt: (pt[b * nb + bb, p], layer, 0, 0, 0, 0))

    in_specs = [page_spec(bb, p) for bb in range(nb) for p in range(n_pages)] + [
        full(wc), full(pe_pair), full(w2), per_b(qp8), per_b(qr8), per_b(ng3), per_b(kvn3), per_b(winn3),
        pl.BlockSpec(wshape, lambda b, pt: (b, layer, 0, 0, 0, 0)), full(ovl), full(gg), full(selr), full(expand),
        full(perm)]
    args = [page_table] + [cache_t] * (nb * n_pages) + [
        wc, pe_pair, w2, qp8, qr8, ng3, kvn3, winn3, win_t, ovl, gg, selr, expand, perm]
    aliases = {}
    if win_prev is not None:
        in_specs.append(pl.BlockSpec(memory_space=pl.ANY))
        aliases = {len(args): 1}
        args.append(win_prev)
    grid_spec = pltpu.PrefetchScalarGridSpec(
        num_scalar_prefetch=1,
        grid=(dec_b // nb,),
        in_specs=in_specs,
        out_specs=[per_b(qp8), pl.BlockSpec(wshape, lambda b, pt: (b, layer, 0, 0, 0, 0))])
    return pl.pallas_call(
        functools.partial(_nsa_decode_kernel, n_pages=n_pages, n_blocks=n_blocks, past=past,
                          has_prev=win_prev is not None),
        name="nsa_decode",
        grid_spec=grid_spec,
        out_shape=[jax.ShapeDtypeStruct(qp8.shape, F32),
                   jax.ShapeDtypeStruct((dec_b, depth, 2, NSA_GROUPS, HEAD_DIM, wbuf), F32)],
        input_output_aliases=aliases,
        compiler_params=_cparams(1),
    )(*args)


def _gla_decode_kernel(q_ref, k_ref, la_ref, v_ref, s_ref, o_ref, so_ref, qt_ref, kt_ref, at_ref):
    i = pl.program_id(0)
    bt = GLA_DEC_TILE
    n_tiles = qt_ref.shape[0]

    @pl.when(i == 0)
    def _():
        qt = q_ref[...].T
        kt = k_ref[...].T
        at = jnp.exp(la_ref[...]).T
        for j in range(n_tiles):
            qt_ref[j] = qt[:, j * bt:(j + 1) * bt]
            kt_ref[j] = kt[:, j * bt:(j + 1) * bt]
            at_ref[j] = at[:, j * bt:(j + 1) * bt]

    qt = qt_ref[i]
    kt = kt_ref[i]
    at = at_ref[i]
    for bb in range(bt):
        for head in range(GLA_HEADS):
            rows = slice(head * HEAD_DIM, (head + 1) * HEAD_DIM)
            v = v_ref[bb:bb + 1, head * LANES:(head + 1) * LANES]
            st = at[rows, bb:bb + 1] * s_ref[bb, 0, head] + kt[rows, bb:bb + 1] * v
            so_ref[bb, head] = st
            o_ref[bb:bb + 1, head * LANES:(head + 1) * LANES] = jnp.sum(qt[rows, bb:bb + 1] * st, axis=0,
                                                                        keepdims=True)


def _gla_decode(gq_s, gk_s, la_s, gv_s, state, layer):
    dec_b, dk4 = gq_s.shape
    bt = GLA_DEC_TILE
    n_tiles = dec_b // bt
    full = lambda a: pl.BlockSpec(a.shape, lambda i: (0,) * a.ndim)
    sblk = (bt, None, GLA_HEADS, HEAD_DIM, LANES)
    return pl.pallas_call(
        _gla_decode_kernel,
        name="gla_decode",
        grid=(n_tiles,),
        in_specs=[full(gq_s), full(gk_s), full(la_s), pl.BlockSpec((bt, gv_s.shape[1]), lambda i: (i, 0)),
                  pl.BlockSpec((bt, 1, GLA_HEADS, HEAD_DIM, LANES), lambda i: (i, layer, 0, 0, 0))],
        out_specs=[pl.BlockSpec((bt, gv_s.shape[1]), lambda i: (i, 0)),
                   pl.BlockSpec((bt, GLA_HEADS, HEAD_DIM, LANES), lambda i: (i, 0, 0, 0))],
        out_shape=[jax.ShapeDtypeStruct(gv_s.shape, F32),
                   jax.ShapeDtypeStruct((dec_b, GLA_HEADS, HEAD_DIM, LANES), F32)],
        scratch_shapes=[pltpu.VMEM((n_tiles, dk4, bt), F32)] * 3,
        compiler_params=_cparams(1),
    )(gq_s, gk_s, la_s, gv_s, state)


def _overlap_t():
    r, w = SEL_LEN // CMP_STRIDE, CMP_LEN // CMP_STRIDE
    off = (np.arange(r)[:, None] + np.arange(w)[None, :]).reshape(-1)
    j = np.arange(LANES)
    c = np.arange(LANES)
    ov = np.sum(c[None, :, None] == (r * j[:, None, None] + off[None, None, :]), axis=-1)
    return ov.astype(np.float32)


def _rope_tables(pos):
    half = HEAD_DIM // 2
    inv = ROPE_THETA ** (-jnp.arange(half, dtype=F32) / half)
    ang = pos[:, None] * inv[None, :]
    cos, sin = jnp.cos(ang), jnp.sin(ang)
    cs = jnp.concatenate([cos, cos, cos, cos], axis=1)
    sn = jnp.concatenate([-sin, sin, -sin, sin], axis=1)
    return cs, sn


def _compress_weights(pe, w1, w2):
    dh = HEAD_DIM
    w1r = w1.reshape(CMP_LEN, dh, -1)
    hid = w1r.shape[-1]
    z = jnp.zeros((dh, hid), w1.dtype)
    groups = []
    for g in range(NSA_GROUPS):
        mats = []
        for lp in range(CMP_STRIDE // 2):
            halves = []
            for base in (0, CMP_STRIDE):
                blocks = []
                for l in (2 * lp, 2 * lp + 1):
                    blocks += [w1r[base + l], z] if g == 0 else [z, w1r[base + l]]
                halves.append(jnp.concatenate(blocks, axis=0))
            mats.append(jnp.concatenate(halves, axis=1))
        groups.append(jnp.stack(mats))
    wc = jnp.stack(groups).astype(BF16)
    pe_flat = jnp.broadcast_to(pe.reshape(1, -1), (8, pe.size))
    return wc, pe_flat, w2.astype(BF16)


def kernel(x_prompt, x_sample, cache_nsa_kv, cache_win_kv, state_gla, page_table, ln_in_g, ln_in_b, w_in, cmp_k_pe, cmp_k_w1, cmp_k_w2, cmp_v_pe, cmp_v_w1, cmp_v_w2, gla_w_a2, gla_b_a, gla_norm_g, w_nsa_up, w_gla_up, w_out, ln1_g, ln1_b, mlp_w1, mlp_w2, ln2_g, ln2_b):
    batch, seq, dm = x_prompt.shape
    dec_b = x_sample.shape[0]
    depth = w_in.shape[0]
    n_phys, _, page = cache_nsa_kv.shape[:3]
    past = page_table.shape[1] * page
    wbuf = cache_win_kv.shape[2]
    alpha = (2.0 * depth) ** 0.25
    n_p = batch * seq
    qw = NSA_HEADS * HEAD_DIM

    h_p = x_prompt.reshape(n_p, dm)
    h_s = x_sample.reshape(dec_b, dm)
    ln_in = (ln_in_g[None, :], ln_in_b[None, :])

    cs_p, sn_p = _rope_tables(jnp.arange(seq, dtype=F32))
    cs_s, sn_s = _rope_tables(jnp.full((dec_b,), past, F32))
    ovl = jnp.asarray(_overlap_t())
    wlev = jnp.asarray(_gla_level_matrix(GLA_CHUNK)).astype(BF16)
    col = np.arange(LANES)
    gg = jnp.asarray(((col[:, None] // NSA_HPG == col[None, :] // NSA_HPG)
                      & (col[:, None] < NSA_HEADS) & (col[None, :] < NSA_HEADS)).astype(np.float32)).astype(BF16)
    selr = jnp.asarray(np.stack([col[None, :] == np.arange(NSA_HEADS)[:, None] * 3 + r
                                 for r in range(3)]).astype(np.float32)).astype(BF16)
    cache_t = jnp.transpose(cache_nsa_kv, (0, 1, 3, 4, 5, 2))
    win_t = jnp.transpose(cache_win_kv, (0, 1, 3, 4, 5, 2))
    expand = jnp.asarray((np.arange(past)[None, :] // SEL_LEN == col[:, None]).astype(np.float32)).astype(BF16)
    r_idx = np.arange(page)[None, :]
    m_idx = np.arange(page // 2)[:, None]
    cpp = page // CMP_STRIDE
    perm = jnp.asarray(np.concatenate([r_idx == (m_idx % cpp) * CMP_STRIDE + 2 * (m_idx // cpp) + par
                                       for par in range(2)]).astype(np.float32)).astype(BF16)
    win_buf = None

    sizes = (qw, 6 * NSA_GROUPS * HEAD_DIM, 3 * NSA_HEADS, GLA_HEADS * HEAD_DIM, GLA_HEADS * HEAD_DIM,
             GLA_HEADS * LANES, GLA_RANK, GLA_HEADS * LANES, 2 * dm)
    pts = np.concatenate([[0], np.cumsum(sizes)])
    seg = lambda w, i: w[:, pts[i]:pts[i + 1]]

    t_bufs = None
    gla_p, kv_s, gla_s = [], [], []
    for l in range(depth):
        wl = w_in[l]
        misc = jnp.concatenate([seg(wl, 2), seg(wl, 6), jnp.zeros((dm, LANES - 3 * NSA_HEADS - GLA_RANK), F32)], 1)
        w_a = jnp.concatenate([seg(wl, 0), seg(wl, 1), seg(wl, 3), seg(wl, 4), seg(wl, 5), misc], 1).astype(BF16)
        w_b = jnp.concatenate([seg(wl, 7), seg(wl, 8)], axis=1).astype(BF16)
        wa_pad = jnp.zeros((LANES, GLA_HEADS * HEAD_DIM), F32).at[3 * NSA_HEADS:3 * NSA_HEADS + GLA_RANK].set(
            gla_w_a2[l]).astype(BF16)
        ba = gla_b_a[l][None, :]
        w_nu, w_gu, w_o = w_nsa_up[l].astype(BF16), w_gla_up[l].astype(BF16), w_out[l].astype(BF16)
        w1, w2 = mlp_w1[l].astype(BF16), mlp_w2[l].astype(BF16)
        gn, g1, b1, g2, b2 = (gla_norm_g[l][None, :], ln1_g[l][None, :], ln1_b[l][None, :], ln2_g[l][None, :],
                              ln2_b[l][None, :])

        wck, pek, w2k = _compress_weights(cmp_k_pe[l], cmp_k_w1[l], cmp_k_w2[l])
        wcv, pev, w2v = _compress_weights(cmp_v_pe[l], cmp_v_w1[l], cmp_v_w2[l])
        wc = jnp.stack([wck, wcv])
        pe_pair = _pe_bias(jnp.stack([pek, pev]), jnp.stack([cmp_k_w1[l], cmp_v_w1[l]]))
        w2c = jnp.stack([w2k, w2v])

        qp, qr, kv, win, gq, gk, gv, la, ng, kvt, wint = _inproj(h_p, w_a, cs_p, sn_p, wa_pad, ba, (l, depth), t_bufs,
                                                                 ln_in if l == 0 else None)
        t_bufs = (kvt, wint)
        kc, vc = _compress_prompt(kv, wc, pe_pair, w2c, batch, seq)
        o_nsa = _nsa_prompt(qp, qr, ng, kc, vc, kv, win, kvt, wint, ovl, batch, seq, l)
        o_gla, st_p = _gla_prompt(gq, gk, gv, la, wlev, batch, seq)
        gla_p.append(st_p)
        h_p = _outproj(h_p, tuple(o_nsa), o_gla, w_b, gn, w_nu, w_gu, w_o, g1, b1, alpha, seq,
                       ln_in if l == 0 else None)
        h_p = _mlp(h_p, w1, w2, g2, b2, alpha)

        qp, qr, kv, win, gq, gk, gv, la, ng = _inproj(h_s, w_a, cs_s, sn_s, wa_pad, ba,
                                                      ln=ln_in if l == 0 else None)
        o8, win_buf = _nsa_decode(page_table, cache_t, l, wc, pe_pair, w2c,
                                  qp.reshape(dec_b, NSA_HEADS, HEAD_DIM), qr.reshape(dec_b, NSA_HEADS, HEAD_DIM),
                                  ng[:, None, :], kv[:, None, :], win[:, None, :], win_t, ovl, gg, selr,
                                  expand, perm, win_buf)
        o_gla, st_s = _gla_decode(gq, gk, la, gv, state_gla, l)
        kv_s.append(kv.reshape(dec_b, 1, 4, NSA_GROUPS, HEAD_DIM))
        gla_s.append(st_s)
        h_s = _outproj(h_s, o8.reshape(dec_b, qw), o_gla, w_b, gn, w_nu, w_gu, w_o, g1, b1, alpha,
                       ln=ln_in if l == 0 else None)
        h_s = _mlp(h_s, w1, w2, g2, b2, alpha)

    kvt, wint = t_bufs
    wn = min(WINDOW, seq)
    return (h_p.reshape(batch, seq, dm), h_s.reshape(dec_b, 1, dm),
            jnp.transpose(kvt, (0, 1, 5, 2, 3, 4)), jnp.transpose(wint[..., seq - wn:], (0, 1, 5, 2, 3, 4)),
            jnp.stack(gla_p, axis=1), jnp.stack(kv_s, axis=1), jnp.transpose(win_buf, (0, 1, 5, 2, 3, 4)),
            jnp.stack(gla_s, axis=1))
```

```python
import functools
import math

import numpy as np
import jax
import jax.numpy as jnp
from jax import lax
from jax.experimental import pallas as pl
from jax.experimental.pallas import tpu as pltpu

F32 = jnp.float32
BF16 = jnp.bfloat16

NSA_HEADS = 8
NSA_GROUPS = 2
NSA_HPG = NSA_HEADS // NSA_GROUPS
HEAD_DIM = 64
SCALE = HEAD_DIM ** -0.5
CMP_LEN = 32
CMP_STRIDE = 16
SEL_LEN = 64
N_SEL = 16
WINDOW = 512
GLA_HEADS = 4
GLA_RANK = 16
GLA_TAU = 16.0
ROPE_THETA = 10000.0
EPS = 1e-5
BIG = 1e6
NEG = -1e30
LOG2E = math.log2(math.e)

LANES = 128
ROW_TILE = 256
MLP_ROW_TILE = 512
Q_TILE = 128
KV_TILE = 256
GLA_CHUNK = 256
GLA_DEC_TILE = 16
NSA_DEC_TILE = 4
VMEM_LIMIT = 56 * 1024 * 1024


def _cparams(n_axes):
    return pltpu.CompilerParams(dimension_semantics=("arbitrary",) * n_axes,
                                vmem_limit_bytes=VMEM_LIMIT)


def _dot(a, b):
    return jnp.dot(a, b, preferred_element_type=F32)


def _dot_nt(a, b):
    return lax.dot_general(a, b, (((1,), (1,)), ((), ())), preferred_element_type=F32)


def _split3(x):
    hi = x.astype(BF16)
    r = x - hi.astype(F32)
    mid = r.astype(BF16)
    lo = (r - mid.astype(F32)).astype(BF16)
    return hi, mid, lo


def _layer_norm(x, g, b):
    mu = jnp.mean(x, axis=-1, keepdims=True)
    xc = x - mu
    var = jnp.mean(xc * xc, axis=-1, keepdims=True)
    return xc * lax.rsqrt(var + EPS) * g + b


def _sigmoid(x):
    return 1.0 / (1.0 + jnp.exp(-x))


def _row_tile(n, tile=ROW_TILE):
    return tile if n % tile == 0 else n


C_Q, C_KV, C_GQ, C_GK, C_GV, C_MISC, C_END = 0, 512, 1280, 1536, 1792, 2304, 2432


def _rope128(x, cs, sn):
    lane = lax.broadcasted_iota(jnp.int32, x.shape, 1)
    first = (lane % HEAD_DIM) < (HEAD_DIM // 2)
    swapped = jnp.where(first, pltpu.roll(x, LANES - HEAD_DIM // 2, 1), pltpu.roll(x, HEAD_DIM // 2, 1))
    return x * cs + swapped * sn


def _inproj_kernel(h_ref, w_ref, cs_ref, sn_ref, wa_ref, ba_ref, *rest, with_t, pre_ln):
    n_out = 11 if with_t else 9
    qp_ref, qr_ref, kv_ref, win_ref, gq_ref, gk_ref, gv_ref, la_ref, ng_ref = rest[-n_out:][:9]
    kvt_ref, wint_ref = rest[-2:] if with_t else (None, None)
    h = h_ref[...]
    if pre_ln:
        h = _layer_norm(h, rest[0][...], rest[1][...])
    hb = h.astype(BF16)
    cs = cs_ref[...]
    sn = sn_ref[...]

    def seg(lo, hi):
        return _dot(hb, w_ref[:, lo:hi])

    for j2 in range(2):
        q2 = seg(C_Q + j2 * 2 * LANES, C_Q + (j2 + 1) * 2 * LANES)
        for jj in range(2):
            qj = q2[:, jj * LANES:(jj + 1) * LANES]
            j = 2 * j2 + jj
            qp_ref[:, j * LANES:(j + 1) * LANES] = qj
            qr_ref[:, j * LANES:(j + 1) * LANES] = _rope128(qj, cs, sn)
    for s2 in range(3):
        x2 = seg(C_KV + s2 * 2 * LANES, C_KV + (s2 + 1) * 2 * LANES)
        for ss in range(2):
            s = 2 * s2 + ss
            x = x2[:, ss * LANES:(ss + 1) * LANES]
            if s in (2, 4):
                x = _rope128(x, cs, sn)
            if s < 4:
                kv_ref[:, s * LANES:(s + 1) * LANES] = x
            else:
                win_ref[:, (s - 4) * LANES:(s - 3) * LANES] = x
            if with_t:
                xt = x.T.reshape(NSA_GROUPS, HEAD_DIM, x.shape[0])
                if s < 4:
                    kvt_ref[s] = xt
                else:
                    wint_ref[s - 4] = xt
    gq_ref[...] = seg(C_GQ, C_GK) * (HEAD_DIM ** -0.5)
    gk_ref[...] = seg(C_GK, C_GV)
    gv_ref[...] = seg(C_GV, C_MISC)
    misc = seg(C_MISC, C_END)
    ng_ref[...] = misc
    x = _dot(misc.astype(BF16), wa_ref[...]) + ba_ref[...]
    la_ref[...] = (jnp.minimum(x, 0.0) - jnp.log(1.0 + jnp.exp(-jnp.abs(x)))) * (1.0 / GLA_TAU)


def _inproj(h, w_a, cs_tab, sn_tab, wa_pad, ba, t_layout=None, ln=None):
    n, d = h.shape
    tm = _row_tile(n)
    tab_tiles = cs_tab.shape[0] // tm

    def tab_map(i):
        return (i % tab_tiles, 0)

    row = lambda w: pl.BlockSpec((tm, w), lambda i: (i, 0))
    full = lambda a: pl.BlockSpec(a.shape, lambda i: (0,) * a.ndim)
    widths = (512, 512, 512, 256, 256, 256, 512, 256, 128)
    out_specs = [row(w) for w in widths]
    out_shape = [jax.ShapeDtypeStruct((n, w), F32) for w in widths]
    in_specs = [row(d), full(w_a), pl.BlockSpec((tm, LANES), tab_map), pl.BlockSpec((tm, LANES), tab_map),
                full(wa_pad), full(ba)]
    args = [h, w_a, cs_tab, sn_tab, wa_pad, ba]
    if ln is not None:
        in_specs += [full(ln[0]), full(ln[1])]
        args += list(ln)
    aliases = {}
    if t_layout is not None:
        layer, kvt_buf = t_layout
        batch = n // cs_tab.shape[0]
        out_specs.append(pl.BlockSpec((None, None, 4, NSA_GROUPS, HEAD_DIM, tm),
                                      lambda i: (i // tab_tiles, layer, 0, 0, 0, i % tab_tiles)))
        out_shape.append(jax.ShapeDtypeStruct(kvt_buf.shape, F32))
        out_specs.append(pl.BlockSpec((None, 2, NSA_GROUPS, HEAD_DIM, tm),
                                      lambda i: (i // tab_tiles, 0, 0, 0, i % tab_tiles)))
        out_shape.append(jax.ShapeDtypeStruct((batch, 2, NSA_GROUPS, HEAD_DIM, cs_tab.shape[0]), F32))
        in_specs.append(pl.BlockSpec(memory_space=pl.ANY))
        aliases = {len(args): len(widths)}
        args.append(kvt_buf)
    return pl.pallas_call(
        functools.partial(_inproj_kernel, with_t=t_layout is not None, pre_ln=ln is not None),
        name="inproj",
        grid=(n // tm,),
        in_specs=in_specs,
        out_specs=out_specs,
        out_shape=out_shape,
        input_output_aliases=aliases,
        compiler_params=_cparams(1),
    )(*args)


def _gelu_tanh(x):
    return 0.5 * x * (1.0 + jnp.tanh(math.sqrt(2.0 / math.pi) * (x + 0.044715 * x * x * x)))


def _compress_rows(chunk_pair, n_chunks, wc_ref, pe_ref, w2_ref, seq_chunks=None):
    seq_chunks = n_chunks if seq_chunks is None else seq_chunks
    outs = []
    for s in range(2):
        acc = [jnp.zeros((n_chunks, 2 * LANES), F32) for _ in range(NSA_GROUPS)]
        for lp in range(CMP_STRIDE // 2):
            a = chunk_pair(s, lp).astype(BF16)
            for g in range(NSA_GROUPS):
                acc[g] = acc[g] + _dot(a, wc_ref[s, g, lp])
        row = lax.broadcasted_iota(jnp.int32, (n_chunks, LANES), 0)
        parts = []
        for g in range(NSA_GROUPS):
            hid = acc[g][:, :LANES] + pltpu.roll(acc[g][:, LANES:], n_chunks - 1, 0) + pe_ref[s]
            parts.append(_dot(_gelu_tanh(hid).astype(BF16), w2_ref[s]))
        out = jnp.concatenate(parts, axis=1)
        outs.append(jnp.where(row % seq_chunks < seq_chunks - 1, out, 0.0))
    return outs


def _pe_bias_kernel(pe_ref, w1_ref, o_ref):
    for s in range(2):
        o_ref[s] = _dot(pe_ref[s].astype(BF16), w1_ref[s].astype(BF16))[0:1]


def _pe_bias(pe_flat, w1):
    hid = w1.shape[2]
    return pl.pallas_call(
        _pe_bias_kernel,
        name="pe_bias",
        out_shape=jax.ShapeDtypeStruct((2, 1, hid), F32),
    )(pe_flat, w1)


def _compress_kernel(xk_ref, xv_ref, wc_ref, pe_ref, w2_ref, kc_ref, vc_ref):
    x_refs = (xk_ref, xv_ref)
    n_chunks = xk_ref.shape[0] // CMP_STRIDE

    def chunk_pair(slot, lp):
        return jnp.concatenate([x_refs[slot][pl.ds(2 * lp, n_chunks, stride=CMP_STRIDE), :],
                                x_refs[slot][pl.ds(2 * lp + 1, n_chunks, stride=CMP_STRIDE), :]], axis=1)

    kc, vc = _compress_rows(chunk_pair, n_chunks, wc_ref, pe_ref, w2_ref)
    kc_ref[0] = kc
    vc_ref[0] = vc


def _compress_prompt(kv, wc, pe_pair, w2, batch, seq):
    nc = seq // CMP_STRIDE
    full = lambda a: pl.BlockSpec(a.shape, lambda b: (0,) * a.ndim)
    return pl.pallas_call(
        _compress_kernel,
        name="compress",
        grid=(batch,),
        in_specs=[pl.BlockSpec((seq, LANES), lambda b: (b, 0)), pl.BlockSpec((seq, LANES), lambda b: (b, 1)),
                  full(wc), full(pe_pair), full(w2)],
        out_specs=[pl.BlockSpec((1, nc, LANES), lambda b: (b, 0, 0))] * 2,
        out_shape=[jax.ShapeDtypeStruct((batch, nc, LANES), F32)] * 2,
        compiler_params=_cparams(1),
    )(kv, kv, wc, pe_pair, w2)


def _select_mask(imp_t, qpos, n_blocks):
    nbp, nq = imp_t.shape
    blk = lax.broadcasted_iota(jnp.int32, (nbp, nq), 0)
    cur = qpos // SEL_LEN
    causal = blk * SEL_LEN <= qpos
    forced = (blk == 0) | (blk == cur) | (blk == cur - 1)
    score = jnp.where(forced, BIG, jnp.where(causal, imp_t, -jnp.inf))
    score = jnp.where(blk < n_blocks, score, -jnp.inf)
    rank = jnp.zeros((nbp, nq), jnp.int32)
    for j in range(n_blocks):
        other = score[j:j + 1, :]
        rank = rank + jnp.where(blk > j, jnp.where(other >= score, 1, 0), jnp.where(other > score, 1, 0))
    return (rank < N_SEL) & (score > -jnp.inf)


V_ROWS = LANES + 16


def _softmax_step(state, s, pv_fn):
    m, acc = state
    m_new = jnp.maximum(m, jnp.max(s, axis=0, keepdims=True))
    alpha = jnp.exp2(m - m_new)
    p = jnp.exp2(s - m_new)
    return m_new, alpha * acc + pv_fn(p.astype(BF16))


def _softmax_init(nq):
    return (jnp.full((1, nq), NEG, F32), jnp.zeros((V_ROWS, nq), F32))


def _nsa_prompt_kernel(qpa_ref, qra_ref, nga_ref, qpb_ref, qrb_ref, ngb_ref, kc_ref, vc_ref, ks_ref, vs_ref,
                       kw_ref, vw_ref, ovl_ref, oa_ref, ob_ref, ksa_ref, vst_ref, kwa_ref, vwt_ref,
                       *, seq, n_blocks):
    g = pl.program_id(1)
    i = pl.program_id(2)
    tq = Q_TILE
    nqt = seq // tq
    n_kt = seq // KV_TILE
    nq = NSA_HPG * tq

    @pl.when((g == 0) & (i == 0))
    def _():
        for kt in range(n_kt):
            rows = pl.ds(kt * KV_TILE, KV_TILE)
            key = kt * KV_TILE + lax.broadcasted_iota(jnp.int32, (KV_TILE, LANES), 0)
            lane = lax.broadcasted_iota(jnp.int32, (KV_TILE, LANES), 1)
            onehot = jnp.where(key // SEL_LEN == lane, 1.0, 0.0).astype(BF16)
            ksa_ref[kt] = jnp.concatenate([ks_ref[rows, :].astype(BF16), onehot], axis=1)
            kwa_ref[kt] = kw_ref[rows, :].astype(BF16)
            cols = slice(kt * KV_TILE, (kt + 1) * KV_TILE)
            ones = jnp.ones((V_ROWS - LANES, KV_TILE), BF16)
            vst_ref[kt] = jnp.concatenate([vs_ref[:, :, cols].reshape(LANES, KV_TILE).astype(BF16), ones], axis=0)
            vwt_ref[kt] = jnp.concatenate([vw_ref[:, :, cols].reshape(LANES, KV_TILE).astype(BF16), ones], axis=0)

    lane_q = lax.broadcasted_iota(jnp.int32, (1, nq), 1) % tq
    lane1 = lax.broadcasted_iota(jnp.int32, (1, tq), 1)
    lane = lax.broadcasted_iota(jnp.int32, (tq, LANES), 1)
    own = (lane // HEAD_DIM) == g
    ovl = ovl_ref[...]
    kc = kc_ref[0].astype(BF16)
    vct = vc_ref[0].T.astype(BF16)
    nc = kc.shape[0]
    nbp = -(-n_blocks // 8) * 8

    def own_rows(x):
        return jnp.where(g == 0, x[:HEAD_DIM], x[HEAD_DIM:])

    def prepare(qp_ref, qr_ref, qs, max_pos):
        def stack_heads(ref, scale):
            parts = []
            for h in range(NSA_HPG):
                qh = ref[:, h * HEAD_DIM:(h + 1) * HEAD_DIM] * scale
                parts.append(jnp.where(own, jnp.concatenate([qh, qh], axis=1), 0.0))
            return parts

        qpos = qs + lane_q
        qp = jnp.concatenate(stack_heads(qp_ref, SCALE), axis=0).astype(BF16)
        s = _dot_nt(kc, qp)
        cblk = lax.broadcasted_iota(jnp.int32, (nc, nq), 0)
        valid = cblk * CMP_STRIDE + (CMP_LEN - 1) <= qpos
        s = jnp.where(valid, s, NEG)
        m = jnp.max(s, axis=0, keepdims=True)
        p = jnp.where(valid, jnp.exp(s - m), 0.0)
        p = p / jnp.maximum(jnp.sum(p, axis=0, keepdims=True), 1e-30)
        o_cmp = own_rows(_dot(vct, p.astype(BF16)))
        psum = p[:, 0:tq]
        for h in range(1, NSA_HPG):
            psum = psum + p[:, h * tq:(h + 1) * tq]
        imp_t = sum(_dot(ovl, piece) for piece in _split3(psum))
        if max_pos // SEL_LEN + 1 <= N_SEL:
            sel = lax.broadcasted_iota(jnp.int32, (nbp, tq), 0) * SEL_LEN <= qs + lane1
        else:
            sel = _select_mask(imp_t[:nbp], qs + lane1, n_blocks)
        bias_t = jnp.where(sel, 0.0, NEG)
        bias = jnp.concatenate([bias_t, jnp.zeros((LANES - nbp, tq), F32)], axis=0).T
        qr_parts = stack_heads(qr_ref, SCALE * LOG2E)
        q_sel = jnp.concatenate([jnp.concatenate([q, bias], axis=1) for q in qr_parts], axis=0).astype(BF16)
        q_win = jnp.concatenate(qr_parts, axis=0).astype(BF16)
        return qpos, o_cmp, q_sel, q_win

    qs_a = i * tq
    qs_b = (nqt - 1 - i) * tq
    qpos_a, o_cmp_a, q_sel_a, q_win_a = prepare(qpa_ref, qra_ref, qs_a, (nqt // 2) * tq - 1)
    qpos_b, o_cmp_b, q_sel_b, q_win_b = prepare(qpb_ref, qrb_ref, qs_b, seq - 1)

    kd_a = qs_a // KV_TILE
    kd_b = qs_b // KV_TILE
    max_kd_a = ((nqt // 2 - 1) * tq) // KV_TILE
    n_sel = n_kt + 1
    n_win = WINDOW // KV_TILE + 1

    krow = lax.broadcasted_iota(jnp.int32, (KV_TILE, nq), 0)
    off_a = qs_a % KV_TILE + lane_q
    off_b = qs_b % KV_TILE + lane_q
    edge_a = jnp.where(krow <= off_a, 0.0, NEG)
    edge_b = jnp.where(krow <= off_b, 0.0, NEG)
    start_a = jnp.where(krow >= off_a, 0.0, NEG)
    start_b = jnp.where(krow >= off_b, 0.0, NEG)

    def sel_task(j):
        if j == 0:
            return 0, q_sel_a, jnp.where(kd_a == 0, edge_a, 0.0)
        if j == n_sel - 1:
            return j - kd_a - 1, q_sel_b, edge_b
        if j > max_kd_a:
            return j - kd_a - 1, q_sel_b, None
        in_a = j <= kd_a
        return (jnp.where(in_a, j, j - kd_a - 1), jnp.where(in_a, q_sel_a, q_sel_b),
                jnp.where(j == kd_a, edge_a, 0.0))

    tasks = []
    for j in range(n_sel):
        kt, q, mask = sel_task(j)
        tasks.append(("sel", j, ksa_ref, vst_ref, kt, q, mask))
    for name, kd, q, edge, start in (("wa", kd_a, q_win_a, edge_a, start_a), ("wb", kd_b, q_win_b, edge_b, start_b)):
        for j in range(n_win):
            kt = kd - (n_win - 1) + j
            inside = jnp.where(kt >= 0, 0.0, NEG)
            if j == n_win - 1:
                mask = edge
            elif j == 0:
                mask = start + inside
            else:
                mask = inside
            tasks.append((name, j, kwa_ref, vwt_ref, jnp.maximum(kt, 0), q, mask))

    def scores(task):
        _, _, k_ref, _, kt, q, _ = task
        return _dot_nt(k_ref[kt], q)

    sel_tasks = tasks[:n_sel]
    win_tasks = tasks[n_sel:]
    tasks = []
    while sel_tasks or win_tasks:
        if sel_tasks:
            tasks.append(sel_tasks.pop(0))
        if win_tasks:
            tasks.append(win_tasks.pop(0))

    results = {}
    states = {}
    s_next = scores(tasks[0])
    for t, task in enumerate(tasks):
        name, j, _, vt_ref, kt, _, mask = task
        s = s_next
        if t + 1 < len(tasks):
            s_next = scores(tasks[t + 1])
        if j == 0:
            states[name] = _softmax_init(nq)
        state = states[name]
        if name == "sel" and 1 <= j <= max_kd_a + 1:
            switch = j == kd_a + 1
            prev = results.get("sa", state)
            results["sa"] = tuple(jnp.where(switch, x, y) for x, y in zip(state, prev))
            state = tuple(jnp.where(switch, x, y) for x, y in zip(_softmax_init(nq), state))
        if mask is not None:
            s = mask + s
        state = _softmax_step(state, s, lambda p: _dot(vt_ref[kt], p))
        states[name] = state
        last = (name == "sel" and j == n_sel - 1) or (name != "sel" and j == n_win - 1)
        if last:
            results["sb" if name == "sel" else name] = state

    def finish(key):
        _, acc = results[key]
        return own_rows(acc[:LANES] / acc[LANES:LANES + 1])

    for ng_ref, o_ref, branches in ((nga_ref, oa_ref, (o_cmp_a, finish("sa"), finish("wa"))),
                                    (ngb_ref, ob_ref, (o_cmp_b, finish("sb"), finish("wb")))):
        ng_t = ng_ref[...].T
        outs = []
        for h in range(NSA_HPG):
            cols = slice(h * tq, (h + 1) * tq)
            tot = jnp.zeros((HEAD_DIM, tq), F32)
            for r, o in enumerate(branches):
                i0 = h * 3 + r
                i1 = (NSA_HPG + h) * 3 + r
                gate = _sigmoid(jnp.where(g == 0, ng_t[i0:i0 + 1], ng_t[i1:i1 + 1]))
                tot = tot + gate * o[:, cols]
            outs.append(tot.T)
        o_ref[...] = jnp.concatenate(outs, axis=1)


def _nsa_prompt(qp, qr, ng, kc, vc, kv, win, kvt, wint, ovl, batch, seq, layer):
    n_blocks = -(-seq // SEL_LEN)
    nqt = seq // Q_TILE
    n_kt = seq // KV_TILE
    assert nqt % 2 == 0 and KV_TILE == 2 * Q_TILE and WINDOW % KV_TILE == 0
    gw = NSA_HPG * HEAD_DIM
    half = nqt // 2
    qa = pl.BlockSpec((Q_TILE, gw), lambda b, g, i: (b * nqt + i, g))
    qb = pl.BlockSpec((Q_TILE, gw), lambda b, g, i: (b * nqt + nqt - 1 - i, g))
    na = pl.BlockSpec((Q_TILE, LANES), lambda b, g, i: (b * nqt + i, 0))
    nb = pl.BlockSpec((Q_TILE, LANES), lambda b, g, i: (b * nqt + nqt - 1 - i, 0))
    cspec = pl.BlockSpec((1,) + kc.shape[1:], lambda b, g, i: (b, 0, 0))
    ospec = pl.BlockSpec((Q_TILE, gw), lambda b, g, i: (b * half + i, g))

    def rowspec(col):
        return pl.BlockSpec((seq, LANES), lambda b, g, i: (b, col))

    kvt_spec = pl.BlockSpec((None, None, None, NSA_GROUPS, HEAD_DIM, seq), lambda b, g, i: (b, layer, 3, 0, 0, 0))
    wint_spec = pl.BlockSpec((None, None, NSA_GROUPS, HEAD_DIM, seq), lambda b, g, i: (b, 1, 0, 0, 0))

    oshape = jax.ShapeDtypeStruct((batch * half * Q_TILE, qp.shape[1]), F32)
    return pl.pallas_call(
        functools.partial(_nsa_prompt_kernel, seq=seq, n_blocks=n_blocks),
        name="nsa_prompt",
        grid=(batch, NSA_GROUPS, half),
        in_specs=[qa, qa, na, qb, qb, nb, cspec, cspec, rowspec(2), kvt_spec, rowspec(0), wint_spec,
                  pl.BlockSpec(ovl.shape, lambda b, g, i: (0, 0))],
        out_specs=[ospec, ospec],
        out_shape=[oshape, oshape],
        scratch_shapes=[pltpu.VMEM((n_kt, KV_TILE, 2 * LANES), BF16), pltpu.VMEM((n_kt, V_ROWS, KV_TILE), BF16),
                        pltpu.VMEM((n_kt, KV_TILE, LANES), BF16), pltpu.VMEM((n_kt, V_ROWS, KV_TILE), BF16)],
        compiler_params=_cparams(3),
    )(qp, qr, ng, qp, qr, ng, kc, vc, kv, kvt, win, wint, ovl)


def _gla_level_matrix(c):
    t = np.arange(c)[:, None]
    u = np.arange(c)[None, :]
    mats = [(u <= t), (u > t)]
    m = c
    while m >= 2:
        split = (t // m) * m + m // 2
        upper = (t % m) >= m // 2
        mats.append(np.where(upper, (u >= split) & (u <= t), (u > t) & (u < split)))
        m //= 2
    return np.concatenate(mats, axis=0).astype(np.float32)


def _gla_prompt_kernel(q_ref, k_ref, v_ref, la_ref, w_ref, o_ref, st_ref, e_ref, s_ref, *, seq):
    c = GLA_CHUNK
    n_levels = int(math.log2(c))
    s_ref[...] = jnp.zeros_like(s_ref)
    row = lax.broadcasted_iota(jnp.int32, (c, c), 0)
    col = lax.broadcasted_iota(jnp.int32, (c, c), 1)
    rowl = lax.broadcasted_iota(jnp.int32, (c, LANES), 0)
    lanel = lax.broadcasted_iota(jnp.int32, (c, LANES), 1)

    def chunk(ci, _):
        rows = pl.ds(pl.multiple_of(ci * c, c), c)
        la = la_ref[rows, :]
        hi = la.astype(BF16)
        lo = (la - hi.astype(F32)).astype(BF16)
        e_ref[...] = jnp.exp(_dot(w_ref[...], hi) + _dot(w_ref[...], lo))
        for pair in range(GLA_HEADS // 2):
            lanes = slice(pair * LANES, (pair + 1) * LANES)
            q = q_ref[rows, lanes]
            k = k_ref[rows, lanes]
            q0 = (q * e_ref[0:c, lanes]).astype(BF16)
            kdec = k * e_ref[c:2 * c, lanes]
            a_last = e_ref[c - 1:c, lanes]
            qls, kls = [], []
            for lv in range(n_levels):
                m = c >> lv
                x = e_ref[(2 + lv) * c:(3 + lv) * c, lanes]
                upper = (rowl % m) >= (m // 2)
                qls.append(jnp.where(upper, q * x, 0.0).astype(BF16))
                kls.append(jnp.where(upper, 0.0, k * x))
            for hh in range(2):
                head = pair * 2 + hh
                mine = (lanel // HEAD_DIM) == hh
                a = jnp.where(row == col, _dot_nt(q.astype(BF16), jnp.where(mine, k, 0.0).astype(BF16)), 0.0)
                for lv in range(n_levels):
                    m = c >> lv
                    same = (row // m) == (col // m)
                    a = a + jnp.where(same, _dot_nt(qls[lv], jnp.where(mine, kls[lv], 0.0).astype(BF16)), 0.0)
                v = v_ref[rows, head * LANES:(head + 1) * LANES]
                st = s_ref[head]
                o = _dot(a.astype(BF16), v.astype(BF16)) + _dot_nt(q0, st.astype(BF16))
                o_ref[rows, head * LANES:(head + 1) * LANES] = o
                kd = jnp.where(mine, kdec, 0.0).astype(BF16)
                s_ref[head] = st * a_last + _dot(v.T.astype(BF16), kd)
        return 0

    lax.fori_loop(0, seq // c, chunk, 0)
    for head in range(GLA_HEADS):
        st = s_ref[head].T
        off = (head % 2) * HEAD_DIM
        st_ref[0, head] = st[off:off + HEAD_DIM]


def _gla_prompt(gq, gk, gv, la, wlev, batch, seq):
    dk2 = gq.shape[1]
    dv4 = gv.shape[1]
    return pl.pallas_call(
        functools.partial(_gla_prompt_kernel, seq=seq),
        name="gla_prompt",
        grid=(batch,),
        in_specs=[pl.BlockSpec((seq, dk2), lambda b: (b, 0)), pl.BlockSpec((seq, dk2), lambda b: (b, 0)),
                  pl.BlockSpec((seq, dv4), lambda b: (b, 0)), pl.BlockSpec((seq, dk2), lambda b: (b, 0)),
                  pl.BlockSpec(wlev.shape, lambda b: (0, 0))],
        out_specs=[pl.BlockSpec((seq, dv4), lambda b: (b, 0)),
                   pl.BlockSpec((1, GLA_HEADS, HEAD_DIM, LANES), lambda b: (b, 0, 0, 0))],
        out_shape=[jax.ShapeDtypeStruct((gq.shape[0], dv4), F32),
                   jax.ShapeDtypeStruct((batch, GLA_HEADS, HEAD_DIM, LANES), F32)],
        scratch_shapes=[pltpu.VMEM((wlev.shape[0], dk2), F32), pltpu.VMEM((GLA_HEADS, LANES, LANES), F32)],
        compiler_params=_cparams(1),
    )(gq, gk, gv, la, wlev)


def _outproj_kernel(h_ref, *refs, alpha, tiles_per_seq, pre_ln):
    if tiles_per_seq:
        lo_ref, hi_ref = refs[:2]
        refs = refs[2:]
        t = pl.program_id(0) % tiles_per_seq
        hi = hi_ref[...]
        n_q = hi.shape[0] // Q_TILE
        mirrored = jnp.concatenate([hi[k * Q_TILE:(k + 1) * Q_TILE] for k in reversed(range(n_q))], axis=0)
        on = jnp.where(t < tiles_per_seq // 2, lo_ref[...], mirrored)
    else:
        on = refs[0][...]
        refs = refs[1:]
    og_ref, wb_ref, gn_ref, wn_ref, wg_ref, wo_ref, g1_ref, b1_ref = refs[:8]
    o_ref = refs[-1]
    h = h_ref[...]
    if pre_ln:
        h = _layer_norm(h, refs[8][...], refs[9][...])
    hb = h.astype(BF16)
    dm = h.shape[1]
    gw = og_ref.shape[1]
    g_r = _dot(hb, wb_ref[:, :gw])
    parts = []
    for head in range(GLA_HEADS):
        x = og_ref[:, head * LANES:(head + 1) * LANES]
        x = x * lax.rsqrt(jnp.mean(x * x, axis=-1, keepdims=True) + EPS) * gn_ref[...]
        gr = g_r[:, head * LANES:(head + 1) * LANES]
        parts.append(x * (gr * _sigmoid(gr)))
    og = jnp.concatenate(parts, axis=1).astype(BF16)
    a = _dot(on.astype(BF16), wn_ref[...])
    c = _dot(og, wg_ref[...])
    m_a = _dot(hb, wb_ref[:, gw:gw + dm])
    m_c = _dot(hb, wb_ref[:, gw + dm:gw + 2 * dm])
    mix = (_sigmoid(m_a) * a + _sigmoid(m_c) * c).astype(BF16)
    y = _dot(mix, wo_ref[...])
    o_ref[...] = _layer_norm(alpha * h + y, g1_ref[...], b1_ref[...])


def _outproj(h, o_nsa, o_gla, w_b, gn, w_nsa, w_gla, w_out, g1, b1, alpha, seq=None, ln=None):
    n, d = h.shape
    tm = _row_tile(n, MLP_ROW_TILE)
    row = lambda w: pl.BlockSpec((tm, w), lambda i: (i, 0))
    full = lambda a: pl.BlockSpec(a.shape, lambda i: (0,) * a.ndim)
    if isinstance(o_nsa, tuple):
        assert tm % Q_TILE == 0 and (seq // 2) % tm == 0
        tps = seq // tm
        hs = tps // 2
        qw = o_nsa[0].shape[1]
        lo_spec = pl.BlockSpec((tm, qw), lambda i: (i // tps * hs + jnp.minimum(i % tps, hs - 1), 0))
        hi_spec = pl.BlockSpec((tm, qw), lambda i: (i // tps * hs + jnp.clip(tps - 1 - i % tps, 0, hs - 1), 0))
        nsa_specs, nsa_args = [lo_spec, hi_spec], list(o_nsa)
    else:
        tps = 0
        nsa_specs, nsa_args = [row(o_nsa.shape[1])], [o_nsa]
    ln_args = [] if ln is None else list(ln)
    return pl.pallas_call(
        functools.partial(_outproj_kernel, alpha=alpha, tiles_per_seq=tps, pre_ln=ln is not None),
        name="outproj",
        grid=(n // tm,),
        in_specs=[row(d)] + nsa_specs + [row(o_gla.shape[1]), full(w_b), full(gn), full(w_nsa),
                                         full(w_gla), full(w_out), full(g1), full(b1)] + [full(a) for a in ln_args],
        out_specs=row(d),
        out_shape=jax.ShapeDtypeStruct((n, d), F32),
        compiler_params=_cparams(1),
    )(h, *nsa_args, o_gla, w_b, gn, w_nsa, w_gla, w_out, g1, b1, *ln_args)


def _mlp_kernel(h_ref, w1_ref, w2_ref, g_ref, b_ref, o_ref, *, alpha):
    h = h_ref[...]
    hb = h.astype(BF16)
    dff = w1_ref.shape[1]
    step = 1024
    f = jnp.zeros(h.shape, F32)
    for c0 in range(0, dff, step):
        u = jnp.maximum(_dot(hb, w1_ref[:, c0:c0 + step]), 0.0)
        f = f + _dot((u * u).astype(BF16), w2_ref[c0:c0 + step, :])
    o_ref[...] = _layer_norm(alpha * h + f, g_ref[...], b_ref[...])


def _mlp(h, w1, w2, g, b, alpha):
    n, d = h.shape
    tm = _row_tile(n, MLP_ROW_TILE)
    row = pl.BlockSpec((tm, d), lambda i: (i, 0))
    full = lambda a: pl.BlockSpec(a.shape, lambda i: (0,) * a.ndim)
    return pl.pallas_call(
        functools.partial(_mlp_kernel, alpha=alpha),
        name="mlp",
        grid=(n // tm,),
        in_specs=[row, full(w1), full(w2), full(g), full(b)],
        out_specs=row,
        out_shape=jax.ShapeDtypeStruct((n, d), F32),
        compiler_params=_cparams(1),
    )(h, w1, w2, g, b)


def _softmax_rows(parts, extra, valid_extra=None):
    m = extra
    for s in parts:
        m = jnp.maximum(m, jnp.max(s, axis=-1, keepdims=True))
    ps = [jnp.exp(s - m) for s in parts]
    pe = jnp.exp(extra - m)
    tot = pe
    for p in ps:
        tot = tot + jnp.sum(p, axis=-1, keepdims=True)
    inv = 1.0 / tot
    return [p * inv for p in ps], pe * inv


def _nsa_decode_kernel(pt_ref, *refs, n_pages, n_blocks, past):
    del pt_ref
    nb = NSA_DEC_TILE
    pages_all = [refs[bb * n_pages:(bb + 1) * n_pages] for bb in range(nb)]
    rest = refs[nb * n_pages:]
    (wc_ref, pe_ref, w2_ref, qp_ref, qr_ref, ng_ref, kvn_ref, winn_ref, wb_ref, ovl_ref, gg_ref, selr_ref,
     exp_ref, perm_ref) = rest[:14]
    o_ref, wo_ref = rest[-2:]
    page = pages_all[0][0].shape[-1]
    wbuf = wb_ref.shape[-1]
    n_rows = qp_ref.shape[1]
    seq_chunks = past // CMP_STRIDE
    n_chunks = nb * seq_chunks
    cpp = page // CMP_STRIDE
    n_lp = CMP_STRIDE // 2

    def chunk_pairs(slot):
        gathered = []
        for pages in pages_all:
            for pg in pages:
                xt = pg[slot].reshape(NSA_GROUPS * HEAD_DIM, page).astype(BF16)
                gathered.append(_dot_nt(perm_ref[...], xt))
        half = n_lp * cpp
        return [jnp.concatenate(
            [jnp.concatenate([x[lp * cpp:(lp + 1) * cpp], x[half + lp * cpp:half + (lp + 1) * cpp]], axis=1)
             for x in gathered], axis=0) for lp in range(n_lp)]

    pairs = [chunk_pairs(slot) for slot in range(2)]
    kc_all, vc_all = _compress_rows(lambda slot, lp: pairs[slot][lp], n_chunks, wc_ref, pe_ref, w2_ref, seq_chunks)
    streams = [_nsa_decode_one(bb, kc_all[bb * seq_chunks:(bb + 1) * seq_chunks],
                               vc_all[bb * seq_chunks:(bb + 1) * seq_chunks], pages_all[bb], qp_ref, qr_ref, ng_ref,
                               kvn_ref, winn_ref, wb_ref, ovl_ref, gg_ref, selr_ref, exp_ref, o_ref, wo_ref,
                               n_blocks, past) for bb in range(nb)]
    while streams:
        streams = [st for st in streams if next(st, "done") is None]


def _nsa_decode_one(bb, kc, vc, sel_pages, qp_ref, qr_ref, ng_ref, kvn_ref, winn_ref, wb_ref, ovl_ref, gg_ref,
                    selr_ref, exp_ref, o_ref, wo_ref, n_blocks, past):
    page = sel_pages[0].shape[-1]
    wbuf = wb_ref.shape[-1]
    n_rows = qp_ref.shape[1]
    row8 = lax.broadcasted_iota(jnp.int32, (n_rows, LANES), 0)
    lane8 = lax.broadcasted_iota(jnp.int32, (n_rows, LANES), 1)
    own = (lane8 // HEAD_DIM) == (row8 // NSA_HPG)

    def by_group(x0, x1):
        return jnp.where(lax.broadcasted_iota(jnp.int32, x0.shape, 0) < NSA_HPG, x0, x1)

    qp = qp_ref[bb] * SCALE
    qr = qr_ref[bb] * SCALE
    qrb = qr.astype(BF16)
    kvn = kvn_ref[bb]
    winn = winn_ref[bb]

    def new_key_scores(krow):
        prod = jnp.concatenate([qr, qr], axis=1) * krow
        return jnp.sum(jnp.where(own, prod, 0.0), axis=-1, keepdims=True)

    def new_value(vrow):
        v = jnp.broadcast_to(vrow, (n_rows, LANES))
        return by_group(v[:, :HEAD_DIM], v[:, HEAD_DIM:])

    nc = kc.shape[0]
    qp_pair = jnp.where(own, jnp.concatenate([qp, qp], axis=1), 0.0)
    s_cmp = _dot_nt(qp_pair.astype(BF16), kc.astype(BF16))
    s_sel = [by_group(_dot(qrb, pg[2, 0].astype(BF16)), _dot(qrb, pg[2, 1].astype(BF16))) for pg in sel_pages]
    s_win = by_group(_dot(qrb, wb_ref[bb, 0, 0].astype(BF16)), _dot(qrb, wb_ref[bb, 0, 1].astype(BF16)))
    pieces = _split3(jnp.broadcast_to(_sigmoid(ng_ref[bb]), (LANES, LANES)))
    gates = [sum(_dot_nt(selr_ref[r], piece) for piece in pieces)[:, :HEAD_DIM] for r in range(3)]
    yield

    cblk = lax.broadcasted_iota(jnp.int32, (n_rows, nc), 1)
    valid = cblk * CMP_STRIDE + (CMP_LEN - 1) <= past
    s = jnp.where(valid, s_cmp, NEG)
    p = jnp.where(valid, jnp.exp(s - jnp.max(s, axis=-1, keepdims=True)), 0.0)
    p = p / jnp.maximum(jnp.sum(p, axis=-1, keepdims=True), 1e-30)
    o = _dot(p.astype(BF16), vc.astype(BF16))
    imp_h = sum(_dot_nt(piece, ovl_ref[...]) for piece in _split3(p))
    ps, p_new_w = _softmax_rows([s_win], new_key_scores(winn[:, 0:LANES]))
    pb = ps[0].astype(BF16)
    o_win = by_group(_dot_nt(pb, wb_ref[bb, 1, 0].astype(BF16)), _dot_nt(pb, wb_ref[bb, 1, 1].astype(BF16)))
    yield
    o_cmp = by_group(o[:, :HEAD_DIM], o[:, HEAD_DIM:])
    o_win = o_win + p_new_w * new_value(winn[:, LANES:2 * LANES])
    imp_h = jnp.concatenate([imp_h, jnp.zeros((LANES - n_rows, LANES), F32)], axis=0)
    imp = sum(_dot(gg_ref[...], piece) for piece in _split3(imp_h))
    yield

    nbp = -(-n_blocks // 8) * 8
    sel = _select_mask(imp.T[:nbp], jnp.full((1, LANES), past, jnp.int32), n_blocks)
    bias_t = jnp.where(sel, 0.0, NEG)
    bias_t = jnp.concatenate([bias_t, jnp.zeros((LANES - nbp, LANES), F32)], axis=0)
    bias = bias_t.T[:n_rows]
    bias_keys = _dot(bias.astype(BF16), exp_ref[...])
    yield

    parts = [s + bias_keys[:, pi * page:(pi + 1) * page] for pi, s in enumerate(s_sel)]
    blk_new = past // SEL_LEN
    s_new = new_key_scores(kvn[:, 2 * LANES:3 * LANES]) + bias[:, blk_new:blk_new + 1]
    ps, p_new = _softmax_rows(parts, s_new)
    acc0 = jnp.zeros((n_rows, HEAD_DIM), F32)
    acc1 = jnp.zeros((n_rows, HEAD_DIM), F32)
    for pg, p in zip(sel_pages, ps):
        pb = p.astype(BF16)
        acc0 = acc0 + _dot_nt(pb, pg[3, 0].astype(BF16))
        acc1 = acc1 + _dot_nt(pb, pg[3, 1].astype(BF16))
    yield
    o_sel = by_group(acc0, acc1) + p_new * new_value(kvn[:, 3 * LANES:4 * LANES])
    o_ref[bb] = gates[0] * o_cmp + gates[1] * o_sel + gates[2] * o_win

    lane_w = lax.broadcasted_iota(jnp.int32, (HEAD_DIM, LANES), 1)
    n_col = wbuf // LANES
    for kv in range(2):
        tile = jnp.concatenate([winn[:, kv * LANES:(kv + 1) * LANES], jnp.zeros((LANES - 1, LANES), F32)], axis=0)
        new_t = tile.T
        for grp in range(NSA_GROUPS):
            col = new_t[grp * HEAD_DIM:(grp + 1) * HEAD_DIM, 0:1]
            rolled = [pltpu.roll(wb_ref[bb, kv, grp, :, c * LANES:(c + 1) * LANES], LANES - 1, 1)
                      for c in range(n_col)]
            for c in range(n_col):
                nxt = rolled[c + 1] if c + 1 < n_col else jnp.broadcast_to(col, (HEAD_DIM, LANES))
                wo_ref[bb, kv, grp, :, c * LANES:(c + 1) * LANES] = jnp.where(lane_w == LANES - 1, nxt, rolled[c])


def _nsa_decode(page_table, cache_t, layer, wc, pe_pair, w2, qp8, qr8, ng3, kvn3, winn3, win_t, ovl, gg,
                selr, expand, perm, win_prev):
    dec_b, n_pages = page_table.shape
    page = cache_t.shape[-1]
    past = n_pages * page
    wbuf = win_t.shape[-1]
    depth = win_t.shape[1]
    assert wbuf <= WINDOW and wbuf % LANES == 0
    n_blocks = -(-(past + 1) // SEL_LEN)
    nb = NSA_DEC_TILE
    assert dec_b % nb == 0
    full = lambda a: pl.BlockSpec(a.shape, lambda b, pt: (0,) * a.ndim)
    per_b = lambda a: pl.BlockSpec((nb,) + a.shape[1:], lambda b, pt: (b,) + (0,) * (a.ndim - 1))
    wshape = (nb, None, 2, NSA_GROUPS, HEAD_DIM, wbuf)

    def page_spec(bb, p):
        return pl.BlockSpec((None, None) + cache_t.shape[2:], lambda b, pt: (pt[b * nb + bb, p], layer, 0, 0, 0, 0))

    in_specs = [page_spec(bb, p) for bb in range(nb) for p in range(n_pages)] + [
        full(wc), full(pe_pair), full(w2), per_b(qp8), per_b(qr8), per_b(ng3), per_b(kvn3), per_b(winn3),
        pl.BlockSpec(wshape, lambda b, pt: (b, layer, 0, 0, 0, 0)), full(ovl), full(gg), full(selr), full(expand),
        full(perm)]
    args = [page_table] + [cache_t] * (nb * n_pages) + [
        wc, pe_pair, w2, qp8, qr8, ng3, kvn3, winn3, win_t, ovl, gg, selr, expand, perm]
    in_specs.append(pl.BlockSpec(memory_space=pl.ANY))
    aliases = {len(args): 1}
    args.append(win_prev)
    grid_spec = pltpu.PrefetchScalarGridSpec(
        num_scalar_prefetch=1,
        grid=(dec_b // nb,),
        in_specs=in_specs,
        out_specs=[per_b(qp8), pl.BlockSpec(wshape, lambda b, pt: (b, layer, 0, 0, 0, 0))])
    return pl.pallas_call(
        functools.partial(_nsa_decode_kernel, n_pages=n_pages, n_blocks=n_blocks, past=past),
        name="nsa_decode",
        grid_spec=grid_spec,
        out_shape=[jax.ShapeDtypeStruct(qp8.shape, F32),
                   jax.ShapeDtypeStruct((dec_b, depth, 2, NSA_GROUPS, HEAD_DIM, wbuf), F32)],
        input_output_aliases=aliases,
        compiler_params=_cparams(1),
    )(*args)


def _gla_decode_kernel(q_ref, k_ref, la_ref, v_ref, s_ref, o_ref, so_ref, qt_ref, kt_ref, at_ref):
    i = pl.program_id(0)
    bt = GLA_DEC_TILE
    n_tiles = qt_ref.shape[0]

    @pl.when(i == 0)
    def _():
        qt = q_ref[...].T
        kt = k_ref[...].T
        at = jnp.exp(la_ref[...]).T
        for j in range(n_tiles):
            qt_ref[j] = qt[:, j * bt:(j + 1) * bt]
            kt_ref[j] = kt[:, j * bt:(j + 1) * bt]
            at_ref[j] = at[:, j * bt:(j + 1) * bt]

    qt = qt_ref[i]
    kt = kt_ref[i]
    at = at_ref[i]
    for bb in range(bt):
        for head in range(GLA_HEADS):
            rows = slice(head * HEAD_DIM, (head + 1) * HEAD_DIM)
            v = v_ref[bb:bb + 1, head * LANES:(head + 1) * LANES]
            st = at[rows, bb:bb + 1] * s_ref[bb, 0, head] + kt[rows, bb:bb + 1] * v
            so_ref[bb, head] = st
            o_ref[bb:bb + 1, head * LANES:(head + 1) * LANES] = jnp.sum(qt[rows, bb:bb + 1] * st, axis=0,
                                                                        keepdims=True)


def _gla_decode(gq_s, gk_s, la_s, gv_s, state, layer):
    dec_b, dk4 = gq_s.shape
    bt = GLA_DEC_TILE
    n_tiles = dec_b // bt
    full = lambda a: pl.BlockSpec(a.shape, lambda i: (0,) * a.ndim)
    sblk = (bt, None, GLA_HEADS, HEAD_DIM, LANES)
    return pl.pallas_call(
        _gla_decode_kernel,
        name="gla_decode",
        grid=(n_tiles,),
        in_specs=[full(gq_s), full(gk_s), full(la_s), pl.BlockSpec((bt, gv_s.shape[1]), lambda i: (i, 0)),
                  pl.BlockSpec((bt, 1, GLA_HEADS, HEAD_DIM, LANES), lambda i: (i, layer, 0, 0, 0))],
        out_specs=[pl.BlockSpec((bt, gv_s.shape[1]), lambda i: (i, 0)),
                   pl.BlockSpec((bt, GLA_HEADS, HEAD_DIM, LANES), lambda i: (i, 0, 0, 0))],
        out_shape=[jax.ShapeDtypeStruct(gv_s.shape, F32),
                   jax.ShapeDtypeStruct((dec_b, GLA_HEADS, HEAD_DIM, LANES), F32)],
        scratch_shapes=[pltpu.VMEM((n_tiles, dk4, bt), F32)] * 3,
        compiler_params=_cparams(1),
    )(gq_s, gk_s, la_s, gv_s, state)


def _overlap_t():
    r, w = SEL_LEN // CMP_STRIDE, CMP_LEN // CMP_STRIDE
    off = (np.arange(r)[:, None] + np.arange(w)[None, :]).reshape(-1)
    j = np.arange(LANES)
    c = np.arange(LANES)
    ov = np.sum(c[None, :, None] == (r * j[:, None, None] + off[None, None, :]), axis=-1)
    return ov.astype(np.float32)


def _rope_tables(pos):
    half = HEAD_DIM // 2
    inv = ROPE_THETA ** (-jnp.arange(half, dtype=F32) / half)
    ang = pos[:, None] * inv[None, :]
    cos, sin = jnp.cos(ang), jnp.sin(ang)
    cs = jnp.concatenate([cos, cos, cos, cos], axis=1)
    sn = jnp.concatenate([-sin, sin, -sin, sin], axis=1)
    return cs, sn


def _compress_weights(pe, w1, w2):
    dh = HEAD_DIM
    w1r = w1.reshape(CMP_LEN, dh, -1)
    hid = w1r.shape[-1]
    z = jnp.zeros((dh, hid), w1.dtype)
    groups = []
    for g in range(NSA_GROUPS):
        mats = []
        for lp in range(CMP_STRIDE // 2):
            halves = []
            for base in (0, CMP_STRIDE):
                blocks = []
                for l in (2 * lp, 2 * lp + 1):
                    blocks += [w1r[base + l], z] if g == 0 else [z, w1r[base + l]]
                halves.append(jnp.concatenate(blocks, axis=0))
            mats.append(jnp.concatenate(halves, axis=1))
        groups.append(jnp.stack(mats))
    wc = jnp.stack(groups).astype(BF16)
    pe_flat = jnp.broadcast_to(pe.reshape(1, -1), (8, pe.size))
    return wc, pe_flat, w2.astype(BF16)


def kernel(x_prompt, x_sample, cache_nsa_kv, cache_win_kv, state_gla, page_table, ln_in_g, ln_in_b, w_in, cmp_k_pe, cmp_k_w1, cmp_k_w2, cmp_v_pe, cmp_v_w1, cmp_v_w2, gla_w_a2, gla_b_a, gla_norm_g, w_nsa_up, w_gla_up, w_out, ln1_g, ln1_b, mlp_w1, mlp_w2, ln2_g, ln2_b):
    batch, seq, dm = x_prompt.shape
    dec_b = x_sample.shape[0]
    depth = w_in.shape[0]
    n_phys, _, page = cache_nsa_kv.shape[:3]
    past = page_table.shape[1] * page
    wbuf = cache_win_kv.shape[2]
    alpha = (2.0 * depth) ** 0.25
    n_p = batch * seq
    qw = NSA_HEADS * HEAD_DIM

    h_p = x_prompt.reshape(n_p, dm)
    h_s = x_sample.reshape(dec_b, dm)
    ln_in = (ln_in_g[None, :], ln_in_b[None, :])

    cs_p, sn_p = _rope_tables(jnp.arange(seq, dtype=F32))
    cs_s, sn_s = _rope_tables(jnp.full((dec_b,), past, F32))
    ovl = jnp.asarray(_overlap_t())
    wlev = jnp.asarray(_gla_level_matrix(GLA_CHUNK)).astype(BF16)
    col = np.arange(LANES)
    gg = jnp.asarray(((col[:, None] // NSA_HPG == col[None, :] // NSA_HPG)
                      & (col[:, None] < NSA_HEADS) & (col[None, :] < NSA_HEADS)).astype(np.float32)).astype(BF16)
    selr = jnp.asarray(np.stack([col[None, :] == np.arange(NSA_HEADS)[:, None] * 3 + r
                                 for r in range(3)]).astype(np.float32)).astype(BF16)
    cache_t = jnp.transpose(cache_nsa_kv, (0, 1, 3, 4, 5, 2))
    win_t = jnp.transpose(cache_win_kv, (0, 1, 3, 4, 5, 2))
    expand = jnp.asarray((np.arange(past)[None, :] // SEL_LEN == col[:, None]).astype(np.float32)).astype(BF16)
    r_idx = np.arange(page)[None, :]
    m_idx = np.arange(page // 2)[:, None]
    cpp = page // CMP_STRIDE
    perm = jnp.asarray(np.concatenate([r_idx == (m_idx % cpp) * CMP_STRIDE + 2 * (m_idx // cpp) + par
                                       for par in range(2)]).astype(np.float32)).astype(BF16)
    win_buf = jnp.zeros(win_t.shape, F32)
    kvt = jnp.zeros((batch, depth, 4, NSA_GROUPS, HEAD_DIM, seq), F32)
    wn = min(WINDOW, seq)

    sizes = (qw, 6 * NSA_GROUPS * HEAD_DIM, 3 * NSA_HEADS, GLA_HEADS * HEAD_DIM, GLA_HEADS * HEAD_DIM,
             GLA_HEADS * LANES, GLA_RANK, GLA_HEADS * LANES, 2 * dm)
    pts = np.concatenate([[0], np.cumsum(sizes)])
    seg = lambda w, i: w[:, pts[i]:pts[i + 1]]

    win_p, gla_p, kv_s, gla_s = [], [], [], []
    for l in range(depth):
        wl = w_in[l]
        misc = jnp.concatenate([seg(wl, 2), seg(wl, 6), jnp.zeros((dm, LANES - 3 * NSA_HEADS - GLA_RANK), F32)], 1)
        w_a = jnp.concatenate([seg(wl, 0), seg(wl, 1), seg(wl, 3), seg(wl, 4), seg(wl, 5), misc], 1).astype(BF16)
        w_b = jnp.concatenate([seg(wl, 7), seg(wl, 8)], axis=1).astype(BF16)
        wa_pad = jnp.zeros((LANES, GLA_HEADS * HEAD_DIM), F32).at[3 * NSA_HEADS:3 * NSA_HEADS + GLA_RANK].set(
            gla_w_a2[l]).astype(BF16)
        ba = gla_b_a[l][None, :]
        w_nu, w_gu, w_o = w_nsa_up[l].astype(BF16), w_gla_up[l].astype(BF16), w_out[l].astype(BF16)
        w1, w2 = mlp_w1[l].astype(BF16), mlp_w2[l].astype(BF16)
        gn, g1, b1, g2, b2 = (gla_norm_g[l][None, :], ln1_g[l][None, :], ln1_b[l][None, :], ln2_g[l][None, :],
                              ln2_b[l][None, :])

        wck, pek, w2k = _compress_weights(cmp_k_pe[l], cmp_k_w1[l], cmp_k_w2[l])
        wcv, pev, w2v = _compress_weights(cmp_v_pe[l], cmp_v_w1[l], cmp_v_w2[l])
        wc = jnp.stack([wck, wcv])
        pe_pair = _pe_bias(jnp.stack([pek, pev]), jnp.stack([cmp_k_w1[l], cmp_v_w1[l]]))
        w2c = jnp.stack([w2k, w2v])

        qp, qr, kv, win, gq, gk, gv, la, ng, kvt, wint = _inproj(h_p, w_a, cs_p, sn_p, wa_pad, ba, (l, kvt),
                                                                 ln_in if l == 0 else None)
        win_p.append(wint[..., seq - wn:])
        kc, vc = _compress_prompt(kv, wc, pe_pair, w2c, batch, seq)
        o_nsa = _nsa_prompt(qp, qr, ng, kc, vc, kv, win, kvt, wint, ovl, batch, seq, l)
        o_gla, st_p = _gla_prompt(gq, gk, gv, la, wlev, batch, seq)
        gla_p.append(st_p)
        h_p = _outproj(h_p, tuple(o_nsa), o_gla, w_b, gn, w_nu, w_gu, w_o, g1, b1, alpha, seq,
                       ln_in if l == 0 else None)
        h_p = _mlp(h_p, w1, w2, g2, b2, alpha)

        qp, qr, kv, win, gq, gk, gv, la, ng = _inproj(h_s, w_a, cs_s, sn_s, wa_pad, ba,
                                                      ln=ln_in if l == 0 else None)
        o8, win_buf = _nsa_decode(page_table, cache_t, l, wc, pe_pair, w2c,
                                  qp.reshape(dec_b, NSA_HEADS, HEAD_DIM), qr.reshape(dec_b, NSA_HEADS, HEAD_DIM),
                                  ng[:, None, :], kv[:, None, :], win[:, None, :], win_t, ovl, gg, selr,
                                  expand, perm, win_buf)
        o_gla, st_s = _gla_decode(gq, gk, la, gv, state_gla, l)
        kv_s.append(kv.reshape(dec_b, 1, 4, NSA_GROUPS, HEAD_DIM))
        gla_s.append(st_s)
        h_s = _outproj(h_s, o8.reshape(dec_b, qw), o_gla, w_b, gn, w_nu, w_gu, w_o, g1, b1, alpha,
                       ln=ln_in if l == 0 else None)
        h_s = _mlp(h_s, w1, w2, g2, b2, alpha)

    return (h_p.reshape(batch, seq, dm), h_s.reshape(dec_b, 1, dm),
            jnp.transpose(kvt, (0, 1, 5, 2, 3, 4)), jnp.transpose(jnp.stack(win_p, axis=1), (0, 1, 5, 2, 3, 4)),
            jnp.stack(gla_p, axis=1), jnp.stack(kv_s, axis=1), jnp.transpose(win_buf, (0, 1, 5, 2, 3, 4)),
            jnp.stack(gla_s, axis=1))
```

```python
import functools
import math

import numpy as np
import jax
import jax.numpy as jnp
from jax import lax
from jax.experimental import pallas as pl
from jax.experimental.pallas import tpu as pltpu

F32 = jnp.float32
BF16 = jnp.bfloat16

NSA_HEADS = 8
NSA_GROUPS = 2
NSA_HPG = NSA_HEADS // NSA_GROUPS
HEAD_DIM = 64
SCALE = HEAD_DIM ** -0.5
CMP_LEN = 32
CMP_STRIDE = 16
SEL_LEN = 64
N_SEL = 16
WINDOW = 512
GLA_HEADS = 4
GLA_RANK = 16
GLA_TAU = 16.0
ROPE_THETA = 10000.0
EPS = 1e-5
BIG = 1e6
NEG = -1e30
LOG2E = math.log2(math.e)

LANES = 128
ROW_TILE = 256
MLP_ROW_TILE = 512
Q_TILE = 128
KV_TILE = 256
GLA_CHUNK = 256
GLA_DEC_TILE = 16
NSA_DEC_TILE = 4
VMEM_LIMIT = 56 * 1024 * 1024


def _cparams(n_axes):
    return pltpu.CompilerParams(dimension_semantics=("arbitrary",) * n_axes,
                                vmem_limit_bytes=VMEM_LIMIT)


def _dot(a, b):
    return jnp.dot(a, b, preferred_element_type=F32)


def _dot_nt(a, b):
    return lax.dot_general(a, b, (((1,), (1,)), ((), ())), preferred_element_type=F32)


def _split3(x):
    hi = x.astype(BF16)
    r = x - hi.astype(F32)
    mid = r.astype(BF16)
    lo = (r - mid.astype(F32)).astype(BF16)
    return hi, mid, lo


def _layer_norm(x, g, b):
    mu = jnp.mean(x, axis=-1, keepdims=True)
    xc = x - mu
    var = jnp.mean(xc * xc, axis=-1, keepdims=True)
    return xc * lax.rsqrt(var + EPS) * g + b


def _sigmoid(x):
    return 1.0 / (1.0 + jnp.exp(-x))


def _row_tile(n, tile=ROW_TILE):
    return tile if n % tile == 0 else n


C_Q, C_KV, C_GQ, C_GK, C_GV, C_MISC, C_END = 0, 512, 1280, 1536, 1792, 2304, 2432


def _rope128(x, cs, sn):
    lane = lax.broadcasted_iota(jnp.int32, x.shape, 1)
    first = (lane % HEAD_DIM) < (HEAD_DIM // 2)
    swapped = jnp.where(first, pltpu.roll(x, LANES - HEAD_DIM // 2, 1), pltpu.roll(x, HEAD_DIM // 2, 1))
    return x * cs + swapped * sn


def _inproj_kernel(h_ref, w_ref, cs_ref, sn_ref, wa_ref, ba_ref, *rest, with_t, pre_ln, first_of=0):
    n_out = 11 if with_t else 9
    qp_ref, qr_ref, kv_ref, win_ref, gq_ref, gk_ref, gv_ref, la_ref, ng_ref = rest[-n_out:][:9]
    kvt_ref, wint_ref = rest[-2:] if with_t else (None, None)
    if first_of:
        for other in range(1, first_of):
            kvt_ref[other] = jnp.zeros(kvt_ref.shape[1:], F32)
        kvt_ref = kvt_ref.at[0]
    h = h_ref[...]
    if pre_ln:
        h = _layer_norm(h, rest[0][...], rest[1][...])
    hb = h.astype(BF16)
    cs = cs_ref[...]
    sn = sn_ref[...]

    def seg(lo, hi):
        return _dot(hb, w_ref[:, lo:hi])

    for j2 in range(2):
        q2 = seg(C_Q + j2 * 2 * LANES, C_Q + (j2 + 1) * 2 * LANES)
        for jj in range(2):
            qj = q2[:, jj * LANES:(jj + 1) * LANES]
            j = 2 * j2 + jj
            qp_ref[:, j * LANES:(j + 1) * LANES] = qj
            qr_ref[:, j * LANES:(j + 1) * LANES] = _rope128(qj, cs, sn)
    for s2 in range(3):
        x2 = seg(C_KV + s2 * 2 * LANES, C_KV + (s2 + 1) * 2 * LANES)
        for ss in range(2):
            s = 2 * s2 + ss
            x = x2[:, ss * LANES:(ss + 1) * LANES]
            if s in (2, 4):
                x = _rope128(x, cs, sn)
            if s < 4:
                kv_ref[:, s * LANES:(s + 1) * LANES] = x
            else:
                win_ref[:, (s - 4) * LANES:(s - 3) * LANES] = x
            if with_t:
                xt = x.T.reshape(NSA_GROUPS, HEAD_DIM, x.shape[0])
                if s < 4:
                    kvt_ref[s] = xt
                else:
                    wint_ref[s - 4] = xt
    gq_ref[...] = seg(C_GQ, C_GK) * (HEAD_DIM ** -0.5)
    gk_ref[...] = seg(C_GK, C_GV)
    gv_ref[...] = seg(C_GV, C_MISC)
    misc = seg(C_MISC, C_END)
    ng_ref[...] = misc
    x = _dot(misc.astype(BF16), wa_ref[...]) + ba_ref[...]
    la_ref[...] = (jnp.minimum(x, 0.0) - jnp.log(1.0 + jnp.exp(-jnp.abs(x)))) * (1.0 / GLA_TAU)


def _inproj(h, w_a, cs_tab, sn_tab, wa_pad, ba, t_layout=None, ln=None):
    n, d = h.shape
    tm = _row_tile(n)
    tab_tiles = cs_tab.shape[0] // tm

    def tab_map(i):
        return (i % tab_tiles, 0)

    row = lambda w: pl.BlockSpec((tm, w), lambda i: (i, 0))
    full = lambda a: pl.BlockSpec(a.shape, lambda i: (0,) * a.ndim)
    widths = (512, 512, 512, 256, 256, 256, 512, 256, 128)
    out_specs = [row(w) for w in widths]
    out_shape = [jax.ShapeDtypeStruct((n, w), F32) for w in widths]
    in_specs = [row(d), full(w_a), pl.BlockSpec((tm, LANES), tab_map), pl.BlockSpec((tm, LANES), tab_map),
                full(wa_pad), full(ba)]
    args = [h, w_a, cs_tab, sn_tab, wa_pad, ba]
    if ln is not None:
        in_specs += [full(ln[0]), full(ln[1])]
        args += list(ln)
    aliases = {}
    first_of = 0
    if t_layout is not None:
        layer, kvt_buf = t_layout
        batch = n // cs_tab.shape[0]
        kshape = (batch, kvt_buf, 4, NSA_GROUPS, HEAD_DIM, cs_tab.shape[0]) if layer == 0 else kvt_buf.shape
        if layer == 0:
            first_of = kvt_buf
            out_specs.append(pl.BlockSpec((None, first_of, 4, NSA_GROUPS, HEAD_DIM, tm),
                                          lambda i: (i // tab_tiles, 0, 0, 0, 0, i % tab_tiles)))
        else:
            out_specs.append(pl.BlockSpec((None, None, 4, NSA_GROUPS, HEAD_DIM, tm),
                                          lambda i: (i // tab_tiles, layer, 0, 0, 0, i % tab_tiles)))
        out_shape.append(jax.ShapeDtypeStruct(kshape, F32))
        out_specs.append(pl.BlockSpec((None, 2, NSA_GROUPS, HEAD_DIM, tm),
                                      lambda i: (i // tab_tiles, 0, 0, 0, i % tab_tiles)))
        out_shape.append(jax.ShapeDtypeStruct((batch, 2, NSA_GROUPS, HEAD_DIM, cs_tab.shape[0]), F32))
        if layer > 0:
            in_specs.append(pl.BlockSpec(memory_space=pl.ANY))
            aliases = {len(args): len(widths)}
            args.append(kvt_buf)
    return pl.pallas_call(
        functools.partial(_inproj_kernel, with_t=t_layout is not None, pre_ln=ln is not None, first_of=first_of),
        name="inproj",
        grid=(n // tm,),
        in_specs=in_specs,
        out_specs=out_specs,
        out_shape=out_shape,
        input_output_aliases=aliases,
        compiler_params=_cparams(1),
    )(*args)


def _gelu_tanh(x):
    return 0.5 * x * (1.0 + jnp.tanh(math.sqrt(2.0 / math.pi) * (x + 0.044715 * x * x * x)))


def _compress_rows(chunk_pair, n_chunks, wc_ref, pe_ref, w2_ref, seq_chunks=None):
    seq_chunks = n_chunks if seq_chunks is None else seq_chunks
    outs = []
    for s in range(2):
        acc = [jnp.zeros((n_chunks, 2 * LANES), F32) for _ in range(NSA_GROUPS)]
        for lp in range(CMP_STRIDE // 2):
            a = chunk_pair(s, lp).astype(BF16)
            for g in range(NSA_GROUPS):
                acc[g] = acc[g] + _dot(a, wc_ref[s, g, lp])
        row = lax.broadcasted_iota(jnp.int32, (n_chunks, LANES), 0)
        parts = []
        for g in range(NSA_GROUPS):
            hid = acc[g][:, :LANES] + pltpu.roll(acc[g][:, LANES:], n_chunks - 1, 0) + pe_ref[s]
            parts.append(_dot(_gelu_tanh(hid).astype(BF16), w2_ref[s]))
        out = jnp.concatenate(parts, axis=1)
        outs.append(jnp.where(row % seq_chunks < seq_chunks - 1, out, 0.0))
    return outs


def _pe_bias_kernel(pe_ref, w1_ref, o_ref):
    for s in range(2):
        o_ref[s] = _dot(pe_ref[s].astype(BF16), w1_ref[s].astype(BF16))[0:1]


def _pe_bias(pe_flat, w1):
    hid = w1.shape[2]
    return pl.pallas_call(
        _pe_bias_kernel,
        name="pe_bias",
        out_shape=jax.ShapeDtypeStruct((2, 1, hid), F32),
    )(pe_flat, w1)


def _compress_kernel(xk_ref, xv_ref, wc_ref, pe_ref, w2_ref, kc_ref, vc_ref):
    x_refs = (xk_ref, xv_ref)
    n_chunks = xk_ref.shape[0] // CMP_STRIDE

    def chunk_pair(slot, lp):
        return jnp.concatenate([x_refs[slot][pl.ds(2 * lp, n_chunks, stride=CMP_STRIDE), :],
                                x_refs[slot][pl.ds(2 * lp + 1, n_chunks, stride=CMP_STRIDE), :]], axis=1)

    kc, vc = _compress_rows(chunk_pair, n_chunks, wc_ref, pe_ref, w2_ref)
    kc_ref[0] = kc
    vc_ref[0] = vc


def _compress_prompt(kv, wc, pe_pair, w2, batch, seq):
    nc = seq // CMP_STRIDE
    full = lambda a: pl.BlockSpec(a.shape, lambda b: (0,) * a.ndim)
    return pl.pallas_call(
        _compress_kernel,
        name="compress",
        grid=(batch,),
        in_specs=[pl.BlockSpec((seq, LANES), lambda b: (b, 0)), pl.BlockSpec((seq, LANES), lambda b: (b, 1)),
                  full(wc), full(pe_pair), full(w2)],
        out_specs=[pl.BlockSpec((1, nc, LANES), lambda b: (b, 0, 0))] * 2,
        out_shape=[jax.ShapeDtypeStruct((batch, nc, LANES), F32)] * 2,
        compiler_params=_cparams(1),
    )(kv, kv, wc, pe_pair, w2)


def _select_mask(imp_t, qpos, n_blocks):
    nbp, nq = imp_t.shape
    blk = lax.broadcasted_iota(jnp.int32, (nbp, nq), 0)
    cur = qpos // SEL_LEN
    causal = blk * SEL_LEN <= qpos
    forced = (blk == 0) | (blk == cur) | (blk == cur - 1)
    score = jnp.where(forced, BIG, jnp.where(causal, imp_t, -jnp.inf))
    score = jnp.where(blk < n_blocks, score, -jnp.inf)
    rank = jnp.zeros((nbp, nq), jnp.int32)
    for j in range(n_blocks):
        other = score[j:j + 1, :]
        rank = rank + jnp.where(blk > j, jnp.where(other >= score, 1, 0), jnp.where(other > score, 1, 0))
    return (rank < N_SEL) & (score > -jnp.inf)


V_ROWS = LANES + 16


def _softmax_step(state, s, pv_fn):
    m, acc = state
    m_new = jnp.maximum(m, jnp.max(s, axis=0, keepdims=True))
    alpha = jnp.exp2(m - m_new)
    p = jnp.exp2(s - m_new)
    return m_new, alpha * acc + pv_fn(p.astype(BF16))


def _softmax_init(nq):
    return (jnp.full((1, nq), NEG, F32), jnp.zeros((V_ROWS, nq), F32))


def _nsa_prompt_kernel(qpa_ref, qra_ref, nga_ref, qpb_ref, qrb_ref, ngb_ref, kc_ref, vc_ref, ks_ref, vs_ref,
                       kw_ref, vw_ref, ovl_ref, oa_ref, ob_ref, ksa_ref, vst_ref, kwa_ref, vwt_ref,
                       *, seq, n_blocks):
    g = pl.program_id(1)
    i = pl.program_id(2)
    tq = Q_TILE
    nqt = seq // tq
    n_kt = seq // KV_TILE
    nq = NSA_HPG * tq

    @pl.when((g == 0) & (i == 0))
    def _():
        for kt in range(n_kt):
            rows = pl.ds(kt * KV_TILE, KV_TILE)
            key = kt * KV_TILE + lax.broadcasted_iota(jnp.int32, (KV_TILE, LANES), 0)
            lane = lax.broadcasted_iota(jnp.int32, (KV_TILE, LANES), 1)
            onehot = jnp.where(key // SEL_LEN == lane, 1.0, 0.0).astype(BF16)
            ksa_ref[kt] = jnp.concatenate([ks_ref[rows, :].astype(BF16), onehot], axis=1)
            kwa_ref[kt] = kw_ref[rows, :].astype(BF16)
            cols = slice(kt * KV_TILE, (kt + 1) * KV_TILE)
            ones = jnp.ones((V_ROWS - LANES, KV_TILE), BF16)
            vst_ref[kt] = jnp.concatenate([vs_ref[:, :, cols].reshape(LANES, KV_TILE).astype(BF16), ones], axis=0)
            vwt_ref[kt] = jnp.concatenate([vw_ref[:, :, cols].reshape(LANES, KV_TILE).astype(BF16), ones], axis=0)

    lane_q = lax.broadcasted_iota(jnp.int32, (1, nq), 1) % tq
    lane1 = lax.broadcasted_iota(jnp.int32, (1, tq), 1)
    lane = lax.broadcasted_iota(jnp.int32, (tq, LANES), 1)
    own = (lane // HEAD_DIM) == g
    ovl = ovl_ref[...]
    kc = kc_ref[0].astype(BF16)
    vct = vc_ref[0].T.astype(BF16)
    nc = kc.shape[0]
    nbp = -(-n_blocks // 8) * 8

    def own_rows(x):
        return jnp.where(g == 0, x[:HEAD_DIM], x[HEAD_DIM:])

    def prepare(qp_ref, qr_ref, qs, max_pos):
        def stack_heads(ref, scale):
            parts = []
            for h in range(NSA_HPG):
                qh = ref[:, h * HEAD_DIM:(h + 1) * HEAD_DIM] * scale
                parts.append(jnp.where(own, jnp.concatenate([qh, qh], axis=1), 0.0))
            return parts

        qpos = qs + lane_q
        qp = jnp.concatenate(stack_heads(qp_ref, SCALE), axis=0).astype(BF16)
        s = _dot_nt(kc, qp)
        cblk = lax.broadcasted_iota(jnp.int32, (nc, nq), 0)
        valid = cblk * CMP_STRIDE + (CMP_LEN - 1) <= qpos
        s = jnp.where(valid, s, NEG)
        m = jnp.max(s, axis=0, keepdims=True)
        p = jnp.where(valid, jnp.exp(s - m), 0.0)
        p = p / jnp.maximum(jnp.sum(p, axis=0, keepdims=True), 1e-30)
        o_cmp = own_rows(_dot(vct, p.astype(BF16)))
        psum = p[:, 0:tq]
        for h in range(1, NSA_HPG):
            psum = psum + p[:, h * tq:(h + 1) * tq]
        imp_t = sum(_dot(ovl, piece) for piece in _split3(psum))
        if max_pos // SEL_LEN + 1 <= N_SEL:
            sel = lax.broadcasted_iota(jnp.int32, (nbp, tq), 0) * SEL_LEN <= qs + lane1
        else:
            sel = _select_mask(imp_t[:nbp], qs + lane1, n_blocks)
        bias_t = jnp.where(sel, 0.0, NEG)
        bias = jnp.concatenate([bias_t, jnp.zeros((LANES - nbp, tq), F32)], axis=0).T
        qr_parts = stack_heads(qr_ref, SCALE * LOG2E)
        q_sel = jnp.concatenate([jnp.concatenate([q, bias], axis=1) for q in qr_parts], axis=0).astype(BF16)
        q_win = jnp.concatenate(qr_parts, axis=0).astype(BF16)
        return qpos, o_cmp, q_sel, q_win

    qs_a = i * tq
    qs_b = (nqt - 1 - i) * tq
    qpos_a, o_cmp_a, q_sel_a, q_win_a = prepare(qpa_ref, qra_ref, qs_a, (nqt // 2) * tq - 1)
    qpos_b, o_cmp_b, q_sel_b, q_win_b = prepare(qpb_ref, qrb_ref, qs_b, seq - 1)

    kd_a = qs_a // KV_TILE
    kd_b = qs_b // KV_TILE
    max_kd_a = ((nqt // 2 - 1) * tq) // KV_TILE
    n_sel = n_kt + 1
    n_win = WINDOW // KV_TILE + 1

    krow = lax.broadcasted_iota(jnp.int32, (KV_TILE, nq), 0)
    off_a = qs_a % KV_TILE + lane_q
    off_b = qs_b % KV_TILE + lane_q
    edge_a = jnp.where(krow <= off_a, 0.0, NEG)
    edge_b = jnp.where(krow <= off_b, 0.0, NEG)
    start_a = jnp.where(krow >= off_a, 0.0, NEG)
    start_b = jnp.where(krow >= off_b, 0.0, NEG)

    def sel_task(j):
        if j == 0:
            return 0, q_sel_a, jnp.where(kd_a == 0, edge_a, 0.0)
        if j == n_sel - 1:
            return j - kd_a - 1, q_sel_b, edge_b
        if j > max_kd_a:
            return j - kd_a - 1, q_sel_b, None
        in_a = j <= kd_a
        return (jnp.where(in_a, j, j - kd_a - 1), jnp.where(in_a, q_sel_a, q_sel_b),
                jnp.where(j == kd_a, edge_a, 0.0))

    tasks = []
    for j in range(n_sel):
        kt, q, mask = sel_task(j)
        tasks.append(("sel", j, ksa_ref, vst_ref, kt, q, mask))
    for name, kd, q, edge, start in (("wa", kd_a, q_win_a, edge_a, start_a), ("wb", kd_b, q_win_b, edge_b, start_b)):
        for j in range(n_win):
            kt = kd - (n_win - 1) + j
            inside = jnp.where(kt >= 0, 0.0, NEG)
            if j == n_win - 1:
                mask = edge
            elif j == 0:
                mask = start + inside
            else:
                mask = inside
            tasks.append((name, j, kwa_ref, vwt_ref, jnp.maximum(kt, 0), q, mask))

    def scores(task):
        _, _, k_ref, _, kt, q, _ = task
        return _dot_nt(k_ref[kt], q)

    sel_tasks = tasks[:n_sel]
    win_tasks = tasks[n_sel:]
    tasks = []
    while sel_tasks or win_tasks:
        if sel_tasks:
            tasks.append(sel_tasks.pop(0))
        if win_tasks:
            tasks.append(win_tasks.pop(0))

    results = {}
    states = {}
    s_next = scores(tasks[0])
    for t, task in enumerate(tasks):
        name, j, _, vt_ref, kt, _, mask = task
        s = s_next
        if t + 1 < len(tasks):
            s_next = scores(tasks[t + 1])
        if j == 0:
            states[name] = _softmax_init(nq)
        state = states[name]
        if name == "sel" and 1 <= j <= max_kd_a + 1:
            switch = j == kd_a + 1
            prev = results.get("sa", state)
            results["sa"] = tuple(jnp.where(switch, x, y) for x, y in zip(state, prev))
            state = tuple(jnp.where(switch, x, y) for x, y in zip(_softmax_init(nq), state))
        if mask is not None:
            s = mask + s
        state = _softmax_step(state, s, lambda p: _dot(vt_ref[kt], p))
        states[name] = state
        last = (name == "sel" and j == n_sel - 1) or (name != "sel" and j == n_win - 1)
        if last:
            results["sb" if name == "sel" else name] = state

    def finish(key):
        _, acc = results[key]
        return own_rows(acc[:LANES] / acc[LANES:LANES + 1])

    for ng_ref, o_ref, branches in ((nga_ref, oa_ref, (o_cmp_a, finish("sa"), finish("wa"))),
                                    (ngb_ref, ob_ref, (o_cmp_b, finish("sb"), finish("wb")))):
        ng_t = ng_ref[...].T
        outs = []
        for h in range(NSA_HPG):
            cols = slice(h * tq, (h + 1) * tq)
            tot = jnp.zeros((HEAD_DIM, tq), F32)
            for r, o in enumerate(branches):
                i0 = h * 3 + r
                i1 = (NSA_HPG + h) * 3 + r
                gate = _sigmoid(jnp.where(g == 0, ng_t[i0:i0 + 1], ng_t[i1:i1 + 1]))
                tot = tot + gate * o[:, cols]
            outs.append(tot.T)
        o_ref[...] = jnp.concatenate(outs, axis=1)


def _nsa_prompt(qp, qr, ng, kc, vc, kv, win, kvt, wint, ovl, batch, seq, layer):
    n_blocks = -(-seq // SEL_LEN)
    nqt = seq // Q_TILE
    n_kt = seq // KV_TILE
    assert nqt % 2 == 0 and KV_TILE == 2 * Q_TILE and WINDOW % KV_TILE == 0
    gw = NSA_HPG * HEAD_DIM
    half = nqt // 2
    qa = pl.BlockSpec((Q_TILE, gw), lambda b, g, i: (b * nqt + i, g))
    qb = pl.BlockSpec((Q_TILE, gw), lambda b, g, i: (b * nqt + nqt - 1 - i, g))
    na = pl.BlockSpec((Q_TILE, LANES), lambda b, g, i: (b * nqt + i, 0))
    nb = pl.BlockSpec((Q_TILE, LANES), lambda b, g, i: (b * nqt + nqt - 1 - i, 0))
    cspec = pl.BlockSpec((1,) + kc.shape[1:], lambda b, g, i: (b, 0, 0))
    ospec = pl.BlockSpec((Q_TILE, gw), lambda b, g, i: (b * half + i, g))

    def rowspec(col):
        return pl.BlockSpec((seq, LANES), lambda b, g, i: (b, col))

    kvt_spec = pl.BlockSpec((None, None, None, NSA_GROUPS, HEAD_DIM, seq), lambda b, g, i: (b, layer, 3, 0, 0, 0))
    wint_spec = pl.BlockSpec((None, None, NSA_GROUPS, HEAD_DIM, seq), lambda b, g, i: (b, 1, 0, 0, 0))

    oshape = jax.ShapeDtypeStruct((batch * half * Q_TILE, qp.shape[1]), F32)
    return pl.pallas_call(
        functools.partial(_nsa_prompt_kernel, seq=seq, n_blocks=n_blocks),
        name="nsa_prompt",
        grid=(batch, NSA_GROUPS, half),
        in_specs=[qa, qa, na, qb, qb, nb, cspec, cspec, rowspec(2), kvt_spec, rowspec(0), wint_spec,
                  pl.BlockSpec(ovl.shape, lambda b, g, i: (0, 0))],
        out_specs=[ospec, ospec],
        out_shape=[oshape, oshape],
        scratch_shapes=[pltpu.VMEM((n_kt, KV_TILE, 2 * LANES), BF16), pltpu.VMEM((n_kt, V_ROWS, KV_TILE), BF16),
                        pltpu.VMEM((n_kt, KV_TILE, LANES), BF16), pltpu.VMEM((n_kt, V_ROWS, KV_TILE), BF16)],
        compiler_params=_cparams(3),
    )(qp, qr, ng, qp, qr, ng, kc, vc, kv, kvt, win, wint, ovl)


def _gla_level_matrix(c):
    t = np.arange(c)[:, None]
    u = np.arange(c)[None, :]
    mats = [(u <= t), (u > t)]
    m = c
    while m >= 2:
        split = (t // m) * m + m // 2
        upper = (t % m) >= m // 2
        mats.append(np.where(upper, (u >= split) & (u <= t), (u > t) & (u < split)))
        m //= 2
    return np.concatenate(mats, axis=0).astype(np.float32)


def _gla_prompt_kernel(q_ref, k_ref, v_ref, la_ref, w_ref, o_ref, st_ref, e_ref, s_ref, *, seq):
    c = GLA_CHUNK
    n_levels = int(math.log2(c))
    s_ref[...] = jnp.zeros_like(s_ref)
    row = lax.broadcasted_iota(jnp.int32, (c, c), 0)
    col = lax.broadcasted_iota(jnp.int32, (c, c), 1)
    rowl = lax.broadcasted_iota(jnp.int32, (c, LANES), 0)
    lanel = lax.broadcasted_iota(jnp.int32, (c, LANES), 1)

    def chunk(ci, _):
        rows = pl.ds(pl.multiple_of(ci * c, c), c)
        la = la_ref[rows, :]
        hi = la.astype(BF16)
        lo = (la - hi.astype(F32)).astype(BF16)
        e_ref[...] = jnp.exp(_dot(w_ref[...], hi) + _dot(w_ref[...], lo))
        for pair in range(GLA_HEADS // 2):
            lanes = slice(pair * LANES, (pair + 1) * LANES)
            q = q_ref[rows, lanes]
            k = k_ref[rows, lanes]
            q0 = (q * e_ref[0:c, lanes]).astype(BF16)
            kdec = k * e_ref[c:2 * c, lanes]
            a_last = e_ref[c - 1:c, lanes]
            qls, kls = [], []
            for lv in range(n_levels):
                m = c >> lv
                x = e_ref[(2 + lv) * c:(3 + lv) * c, lanes]
                upper = (rowl % m) >= (m // 2)
                qls.append(jnp.where(upper, q * x, 0.0).astype(BF16))
                kls.append(jnp.where(upper, 0.0, k * x))
            for hh in range(2):
                head = pair * 2 + hh
                mine = (lanel // HEAD_DIM) == hh
                a = jnp.where(row == col, _dot_nt(q.astype(BF16), jnp.where(mine, k, 0.0).astype(BF16)), 0.0)
                for lv in range(n_levels):
                    m = c >> lv
                    same = (row // m) == (col // m)
                    a = a + jnp.where(same, _dot_nt(qls[lv], jnp.where(mine, kls[lv], 0.0).astype(BF16)), 0.0)
                v = v_ref[rows, head * LANES:(head + 1) * LANES]
                st = s_ref[head]
                o = _dot(a.astype(BF16), v.astype(BF16)) + _dot_nt(q0, st.astype(BF16))
                o_ref[rows, head * LANES:(head + 1) * LANES] = o
                kd = jnp.where(mine, kdec, 0.0).astype(BF16)
                s_ref[head] = st * a_last + _dot(v.T.astype(BF16), kd)
        return 0

    lax.fori_loop(0, seq // c, chunk, 0)
    for head in range(GLA_HEADS):
        st = s_ref[head].T
        off = (head % 2) * HEAD_DIM
        st_ref[0, head] = st[off:off + HEAD_DIM]


def _gla_prompt(gq, gk, gv, la, wlev, batch, seq):
    dk2 = gq.shape[1]
    dv4 = gv.shape[1]
    return pl.pallas_call(
        functools.partial(_gla_prompt_kernel, seq=seq),
        name="gla_prompt",
        grid=(batch,),
        in_specs=[pl.BlockSpec((seq, dk2), lambda b: (b, 0)), pl.BlockSpec((seq, dk2), lambda b: (b, 0)),
                  pl.BlockSpec((seq, dv4), lambda b: (b, 0)), pl.BlockSpec((seq, dk2), lambda b: (b, 0)),
                  pl.BlockSpec(wlev.shape, lambda b: (0, 0))],
        out_specs=[pl.BlockSpec((seq, dv4), lambda b: (b, 0)),
                   pl.BlockSpec((1, GLA_HEADS, HEAD_DIM, LANES), lambda b: (b, 0, 0, 0))],
        out_shape=[jax.ShapeDtypeStruct((gq.shape[0], dv4), F32),
                   jax.ShapeDtypeStruct((batch, GLA_HEADS, HEAD_DIM, LANES), F32)],
        scratch_shapes=[pltpu.VMEM((wlev.shape[0], dk2), F32), pltpu.VMEM((GLA_HEADS, LANES, LANES), F32)],
        compiler_params=_cparams(1),
    )(gq, gk, gv, la, wlev)


def _outproj_kernel(h_ref, *refs, alpha, tiles_per_seq, pre_ln):
    if tiles_per_seq:
        lo_ref, hi_ref = refs[:2]
        refs = refs[2:]
        t = pl.program_id(0) % tiles_per_seq
        hi = hi_ref[...]
        n_q = hi.shape[0] // Q_TILE
        mirrored = jnp.concatenate([hi[k * Q_TILE:(k + 1) * Q_TILE] for k in reversed(range(n_q))], axis=0)
        on = jnp.where(t < tiles_per_seq // 2, lo_ref[...], mirrored)
    else:
        on = refs[0][...]
        refs = refs[1:]
    og_ref, wb_ref, gn_ref, wn_ref, wg_ref, wo_ref, g1_ref, b1_ref = refs[:8]
    o_ref = refs[-1]
    h = h_ref[...]
    if pre_ln:
        h = _layer_norm(h, refs[8][...], refs[9][...])
    hb = h.astype(BF16)
    dm = h.shape[1]
    gw = og_ref.shape[1]
    g_r = _dot(hb, wb_ref[:, :gw])
    parts = []
    for head in range(GLA_HEADS):
        x = og_ref[:, head * LANES:(head + 1) * LANES]
        x = x * lax.rsqrt(jnp.mean(x * x, axis=-1, keepdims=True) + EPS) * gn_ref[...]
        gr = g_r[:, head * LANES:(head + 1) * LANES]
        parts.append(x * (gr * _sigmoid(gr)))
    og = jnp.concatenate(parts, axis=1).astype(BF16)
    a = _dot(on.astype(BF16), wn_ref[...])
    c = _dot(og, wg_ref[...])
    m_a = _dot(hb, wb_ref[:, gw:gw + dm])
    m_c = _dot(hb, wb_ref[:, gw + dm:gw + 2 * dm])
    mix = (_sigmoid(m_a) * a + _sigmoid(m_c) * c).astype(BF16)
    y = _dot(mix, wo_ref[...])
    o_ref[...] = _layer_norm(alpha * h + y, g1_ref[...], b1_ref[...])


def _outproj(h, o_nsa, o_gla, w_b, gn, w_nsa, w_gla, w_out, g1, b1, alpha, seq=None, ln=None):
    n, d = h.shape
    tm = _row_tile(n)
    row = lambda w: pl.BlockSpec((tm, w), lambda i: (i, 0))
    full = lambda a: pl.BlockSpec(a.shape, lambda i: (0,) * a.ndim)
    if isinstance(o_nsa, tuple):
        assert tm % Q_TILE == 0 and (seq // 2) % tm == 0
        tps = seq // tm
        hs = tps // 2
        qw = o_nsa[0].shape[1]
        lo_spec = pl.BlockSpec((tm, qw), lambda i: (i // tps * hs + jnp.minimum(i % tps, hs - 1), 0))
        hi_spec = pl.BlockSpec((tm, qw), lambda i: (i // tps * hs + jnp.clip(tps - 1 - i % tps, 0, hs - 1), 0))
        nsa_specs, nsa_args = [lo_spec, hi_spec], list(o_nsa)
    else:
        tps = 0
        nsa_specs, nsa_args = [row(o_nsa.shape[1])], [o_nsa]
    ln_args = [] if ln is None else list(ln)
    return pl.pallas_call(
        functools.partial(_outproj_kernel, alpha=alpha, tiles_per_seq=tps, pre_ln=ln is not None),
        name="outproj",
        grid=(n // tm,),
        in_specs=[row(d)] + nsa_specs + [row(o_gla.shape[1]), full(w_b), full(gn), full(w_nsa),
                                         full(w_gla), full(w_out), full(g1), full(b1)] + [full(a) for a in ln_args],
        out_specs=row(d),
        out_shape=jax.ShapeDtypeStruct((n, d), F32),
        compiler_params=_cparams(1),
    )(h, *nsa_args, o_gla, w_b, gn, w_nsa, w_gla, w_out, g1, b1, *ln_args)


def _mlp_kernel(h_ref, w1_ref, w2_ref, g_ref, b_ref, o_ref, *, alpha):
    h = h_ref[...]
    hb = h.astype(BF16)
    dff = w1_ref.shape[1]
    step = 1024
    f = jnp.zeros(h.shape, F32)
    for c0 in range(0, dff, step):
        u = jnp.maximum(_dot(hb, w1_ref[:, c0:c0 + step]), 0.0)
        f = f + _dot((u * u).astype(BF16), w2_ref[c0:c0 + step, :])
    o_ref[...] = _layer_norm(alpha * h + f, g_ref[...], b_ref[...])


def _mlp(h, w1, w2, g, b, alpha):
    n, d = h.shape
    tm = _row_tile(n, MLP_ROW_TILE)
    row = pl.BlockSpec((tm, d), lambda i: (i, 0))
    full = lambda a: pl.BlockSpec(a.shape, lambda i: (0,) * a.ndim)
    return pl.pallas_call(
        functools.partial(_mlp_kernel, alpha=alpha),
        name="mlp",
        grid=(n // tm,),
        in_specs=[row, full(w1), full(w2), full(g), full(b)],
        out_specs=row,
        out_shape=jax.ShapeDtypeStruct((n, d), F32),
        compiler_params=_cparams(1),
    )(h, w1, w2, g, b)


def _softmax_rows(parts, extra, valid_extra=None):
    m = extra
    for s in parts:
        m = jnp.maximum(m, jnp.max(s, axis=-1, keepdims=True))
    ps = [jnp.exp(s - m) for s in parts]
    pe = jnp.exp(extra - m)
    tot = pe
    for p in ps:
        tot = tot + jnp.sum(p, axis=-1, keepdims=True)
    inv = 1.0 / tot
    return [p * inv for p in ps], pe * inv


def _nsa_decode_kernel(pt_ref, *refs, n_pages, n_blocks, past):
    del pt_ref
    nb = NSA_DEC_TILE
    pages_all = [refs[bb * n_pages:(bb + 1) * n_pages] for bb in range(nb)]
    rest = refs[nb * n_pages:]
    (wc_ref, pe_ref, w2_ref, qp_ref, qr_ref, ng_ref, kvn_ref, winn_ref, wb_ref, ovl_ref, gg_ref, selr_ref,
     exp_ref, perm_ref) = rest[:14]
    o_ref, wo_ref = rest[-2:]
    page = pages_all[0][0].shape[-1]
    wbuf = wb_ref.shape[-1]
    n_rows = qp_ref.shape[1]
    seq_chunks = past // CMP_STRIDE
    n_chunks = nb * seq_chunks
    cpp = page // CMP_STRIDE
    n_lp = CMP_STRIDE // 2

    def chunk_pairs(slot):
        gathered = []
        for pages in pages_all:
            for pg in pages:
                xt = pg[slot].reshape(NSA_GROUPS * HEAD_DIM, page).astype(BF16)
                gathered.append(_dot_nt(perm_ref[...], xt))
        half = n_lp * cpp
        return [jnp.concatenate(
            [jnp.concatenate([x[lp * cpp:(lp + 1) * cpp], x[half + lp * cpp:half + (lp + 1) * cpp]], axis=1)
             for x in gathered], axis=0) for lp in range(n_lp)]

    pairs = [chunk_pairs(slot) for slot in range(2)]
    kc_all, vc_all = _compress_rows(lambda slot, lp: pairs[slot][lp], n_chunks, wc_ref, pe_ref, w2_ref, seq_chunks)
    streams = [_nsa_decode_one(bb, kc_all[bb * seq_chunks:(bb + 1) * seq_chunks],
                               vc_all[bb * seq_chunks:(bb + 1) * seq_chunks], pages_all[bb], qp_ref, qr_ref, ng_ref,
                               kvn_ref, winn_ref, wb_ref, ovl_ref, gg_ref, selr_ref, exp_ref, o_ref, wo_ref,
                               n_blocks, past) for bb in range(nb)]
    while streams:
        streams = [st for st in streams if next(st, "done") is None]


def _nsa_decode_one(bb, kc, vc, sel_pages, qp_ref, qr_ref, ng_ref, kvn_ref, winn_ref, wb_ref, ovl_ref, gg_ref,
                    selr_ref, exp_ref, o_ref, wo_ref, n_blocks, past):
    page = sel_pages[0].shape[-1]
    wbuf = wb_ref.shape[-1]
    n_rows = qp_ref.shape[1]
    row8 = lax.broadcasted_iota(jnp.int32, (n_rows, LANES), 0)
    lane8 = lax.broadcasted_iota(jnp.int32, (n_rows, LANES), 1)
    own = (lane8 // HEAD_DIM) == (row8 // NSA_HPG)

    def by_group(x0, x1):
        return jnp.where(lax.broadcasted_iota(jnp.int32, x0.shape, 0) < NSA_HPG, x0, x1)

    qp = qp_ref[bb] * SCALE
    qr = qr_ref[bb] * SCALE
    qrb = qr.astype(BF16)
    kvn = kvn_ref[bb]
    winn = winn_ref[bb]

    def new_key_scores(krow):
        prod = jnp.concatenate([qr, qr], axis=1) * krow
        return jnp.sum(jnp.where(own, prod, 0.0), axis=-1, keepdims=True)

    def new_value(vrow):
        v = jnp.broadcast_to(vrow, (n_rows, LANES))
        return by_group(v[:, :HEAD_DIM], v[:, HEAD_DIM:])

    nc = kc.shape[0]
    qp_pair = jnp.where(own, jnp.concatenate([qp, qp], axis=1), 0.0)
    s_cmp = _dot_nt(qp_pair.astype(BF16), kc.astype(BF16))
    s_sel = [by_group(_dot(qrb, pg[2, 0].astype(BF16)), _dot(qrb, pg[2, 1].astype(BF16))) for pg in sel_pages]
    s_win = by_group(_dot(qrb, wb_ref[bb, 0, 0].astype(BF16)), _dot(qrb, wb_ref[bb, 0, 1].astype(BF16)))
    pieces = _split3(jnp.broadcast_to(_sigmoid(ng_ref[bb]), (LANES, LANES)))
    gates = [sum(_dot_nt(selr_ref[r], piece) for piece in pieces)[:, :HEAD_DIM] for r in range(3)]
    yield

    cblk = lax.broadcasted_iota(jnp.int32, (n_rows, nc), 1)
    valid = cblk * CMP_STRIDE + (CMP_LEN - 1) <= past
    s = jnp.where(valid, s_cmp, NEG)
    p = jnp.where(valid, jnp.exp(s - jnp.max(s, axis=-1, keepdims=True)), 0.0)
    p = p / jnp.maximum(jnp.sum(p, axis=-1, keepdims=True), 1e-30)
    o = _dot(p.astype(BF16), vc.astype(BF16))
    imp_h = sum(_dot_nt(piece, ovl_ref[...]) for piece in _split3(p))
    ps, p_new_w = _softmax_rows([s_win], new_key_scores(winn[:, 0:LANES]))
    pb = ps[0].astype(BF16)
    o_win = by_group(_dot_nt(pb, wb_ref[bb, 1, 0].astype(BF16)), _dot_nt(pb, wb_ref[bb, 1, 1].astype(BF16)))
    yield
    o_cmp = by_group(o[:, :HEAD_DIM], o[:, HEAD_DIM:])
    o_win = o_win + p_new_w * new_value(winn[:, LANES:2 * LANES])
    imp_h = jnp.concatenate([imp_h, jnp.zeros((LANES - n_rows, LANES), F32)], axis=0)
    imp = sum(_dot(gg_ref[...], piece) for piece in _split3(imp_h))
    yield

    nbp = -(-n_blocks // 8) * 8
    sel = _select_mask(imp.T[:nbp], jnp.full((1, LANES), past, jnp.int32), n_blocks)
    bias_t = jnp.where(sel, 0.0, NEG)
    bias_t = jnp.concatenate([bias_t, jnp.zeros((LANES - nbp, LANES), F32)], axis=0)
    bias = bias_t.T[:n_rows]
    bias_keys = _dot(bias.astype(BF16), exp_ref[...])
    yield

    parts = [s + bias_keys[:, pi * page:(pi + 1) * page] for pi, s in enumerate(s_sel)]
    blk_new = past // SEL_LEN
    s_new = new_key_scores(kvn[:, 2 * LANES:3 * LANES]) + bias[:, blk_new:blk_new + 1]
    ps, p_new = _softmax_rows(parts, s_new)
    acc0 = jnp.zeros((n_rows, HEAD_DIM), F32)
    acc1 = jnp.zeros((n_rows, HEAD_DIM), F32)
    for pg, p in zip(sel_pages, ps):
        pb = p.astype(BF16)
        acc0 = acc0 + _dot_nt(pb, pg[3, 0].astype(BF16))
        acc1 = acc1 + _dot_nt(pb, pg[3, 1].astype(BF16))
    yield
    o_sel = by_group(acc0, acc1) + p_new * new_value(kvn[:, 3 * LANES:4 * LANES])
    o_ref[bb] = gates[0] * o_cmp + gates[1] * o_sel + gates[2] * o_win

    lane_w = lax.broadcasted_iota(jnp.int32, (HEAD_DIM, LANES), 1)
    n_col = wbuf // LANES
    for kv in range(2):
        tile = jnp.concatenate([winn[:, kv * LANES:(kv + 1) * LANES], jnp.zeros((LANES - 1, LANES), F32)], axis=0)
        new_t = tile.T
        for grp in range(NSA_GROUPS):
            col = new_t[grp * HEAD_DIM:(grp + 1) * HEAD_DIM, 0:1]
            rolled = [pltpu.roll(wb_ref[bb, kv, grp, :, c * LANES:(c + 1) * LANES], LANES - 1, 1)
                      for c in range(n_col)]
            for c in range(n_col):
                nxt = rolled[c + 1] if c + 1 < n_col else jnp.broadcast_to(col, (HEAD_DIM, LANES))
                wo_ref[bb, kv, grp, :, c * LANES:(c + 1) * LANES] = jnp.where(lane_w == LANES - 1, nxt, rolled[c])


def _nsa_decode(page_table, cache_t, layer, wc, pe_pair, w2, qp8, qr8, ng3, kvn3, winn3, win_t, ovl, gg,
                selr, expand, perm, win_prev):
    dec_b, n_pages = page_table.shape
    page = cache_t.shape[-1]
    past = n_pages * page
    wbuf = win_t.shape[-1]
    depth = win_t.shape[1]
    assert wbuf <= WINDOW and wbuf % LANES == 0
    n_blocks = -(-(past + 1) // SEL_LEN)
    nb = NSA_DEC_TILE
    assert dec_b % nb == 0
    full = lambda a: pl.BlockSpec(a.shape, lambda b, pt: (0,) * a.ndim)
    per_b = lambda a: pl.BlockSpec((nb,) + a.shape[1:], lambda b, pt: (b,) + (0,) * (a.ndim - 1))
    wshape = (nb, None, 2, NSA_GROUPS, HEAD_DIM, wbuf)

    def page_spec(bb, p):
        return pl.BlockSpec((None, None) + cache_t.shape[2:], lambda b, pt: (pt[b * nb + bb, p], layer, 0, 0, 0, 0))

    in_specs = [page_spec(bb, p) for bb in range(nb) for p in range(n_pages)] + [
        full(wc), full(pe_pair), full(w2), per_b(qp8), per_b(qr8), per_b(ng3), per_b(kvn3), per_b(winn3),
        pl.BlockSpec(wshape, lambda b, pt: (b, layer, 0, 0, 0, 0)), full(ovl), full(gg), full(selr), full(expand),
        full(perm)]
    args = [page_table] + [cache_t] * (nb * n_pages) + [
        wc, pe_pair, w2, qp8, qr8, ng3, kvn3, winn3, win_t, ovl, gg, selr, expand, perm]
    in_specs.append(pl.BlockSpec(memory_space=pl.ANY))
    aliases = {len(args): 1}
    args.append(win_prev)
    grid_spec = pltpu.PrefetchScalarGridSpec(
        num_scalar_prefetch=1,
        grid=(dec_b // nb,),
        in_specs=in_specs,
        out_specs=[per_b(qp8), pl.BlockSpec(wshape, lambda b, pt: (b, layer, 0, 0, 0, 0))])
    return pl.pallas_call(
        functools.partial(_nsa_decode_kernel, n_pages=n_pages, n_blocks=n_blocks, past=past),
        name="nsa_decode",
        grid_spec=grid_spec,
        out_shape=[jax.ShapeDtypeStruct(qp8.shape, F32),
                   jax.ShapeDtypeStruct((dec_b, depth, 2, NSA_GROUPS, HEAD_DIM, wbuf), F32)],
        input_output_aliases=aliases,
        compiler_params=_cparams(1),
    )(*args)


def _gla_decode_kernel(q_ref, k_ref, la_ref, v_ref, s_ref, o_ref, so_ref, qt_ref, kt_ref, at_ref):
    i = pl.program_id(0)
    bt = GLA_DEC_TILE
    n_tiles = qt_ref.shape[0]

    @pl.when(i == 0)
    def _():
        qt = q_ref[...].T
        kt = k_ref[...].T
        at = jnp.exp(la_ref[...]).T
        for j in range(n_tiles):
            qt_ref[j] = qt[:, j * bt:(j + 1) * bt]
            kt_ref[j] = kt[:, j * bt:(j + 1) * bt]
            at_ref[j] = at[:, j * bt:(j + 1) * bt]

    qt = qt_ref[i]
    kt = kt_ref[i]
    at = at_ref[i]
    for bb in range(bt):
        for head in range(GLA_HEADS):
            rows = slice(head * HEAD_DIM, (head + 1) * HEAD_DIM)
            v = v_ref[bb:bb + 1, head * LANES:(head + 1) * LANES]
            st = at[rows, bb:bb + 1] * s_ref[bb, 0, head] + kt[rows, bb:bb + 1] * v
            so_ref[bb, head] = st
            o_ref[bb:bb + 1, head * LANES:(head + 1) * LANES] = jnp.sum(qt[rows, bb:bb + 1] * st, axis=0,
                                                                        keepdims=True)


def _gla_decode(gq_s, gk_s, la_s, gv_s, state, layer):
    dec_b, dk4 = gq_s.shape
    bt = GLA_DEC_TILE
    n_tiles = dec_b // bt
    full = lambda a: pl.BlockSpec(a.shape, lambda i: (0,) * a.ndim)
    sblk = (bt, None, GLA_HEADS, HEAD_DIM, LANES)
    return pl.pallas_call(
        _gla_decode_kernel,
        name="gla_decode",
        grid=(n_tiles,),
        in_specs=[full(gq_s), full(gk_s), full(la_s), pl.BlockSpec((bt, gv_s.shape[1]), lambda i: (i, 0)),
                  pl.BlockSpec((bt, 1, GLA_HEADS, HEAD_DIM, LANES), lambda i: (i, layer, 0, 0, 0))],
        out_specs=[pl.BlockSpec((bt, gv_s.shape[1]), lambda i: (i, 0)),
                   pl.BlockSpec((bt, GLA_HEADS, HEAD_DIM, LANES), lambda i: (i, 0, 0, 0))],
        out_shape=[jax.ShapeDtypeStruct(gv_s.shape, F32),
                   jax.ShapeDtypeStruct((dec_b, GLA_HEADS, HEAD_DIM, LANES), F32)],
        scratch_shapes=[pltpu.VMEM((n_tiles, dk4, bt), F32)] * 3,
        compiler_params=_cparams(1),
    )(gq_s, gk_s, la_s, gv_s, state)


def _overlap_t():
    r, w = SEL_LEN // CMP_STRIDE, CMP_LEN // CMP_STRIDE
    off = (np.arange(r)[:, None] + np.arange(w)[None, :]).reshape(-1)
    j = np.arange(LANES)
    c = np.arange(LANES)
    ov = np.sum(c[None, :, None] == (r * j[:, None, None] + off[None, None, :]), axis=-1)
    return ov.astype(np.float32)


def _rope_tables(pos):
    half = HEAD_DIM // 2
    inv = ROPE_THETA ** (-jnp.arange(half, dtype=F32) / half)
    ang = pos[:, None] * inv[None, :]
    cos, sin = jnp.cos(ang), jnp.sin(ang)
    cs = jnp.concatenate([cos, cos, cos, cos], axis=1)
    sn = jnp.concatenate([-sin, sin, -sin, sin], axis=1)
    return cs, sn


def _compress_weights(pe, w1, w2):
    dh = HEAD_DIM
    w1r = w1.reshape(CMP_LEN, dh, -1)
    hid = w1r.shape[-1]
    z = jnp.zeros((dh, hid), w1.dtype)
    groups = []
    for g in range(NSA_GROUPS):
        mats = []
        for lp in range(CMP_STRIDE // 2):
            halves = []
            for base in (0, CMP_STRIDE):
                blocks = []
                for l in (2 * lp, 2 * lp + 1):
                    blocks += [w1r[base + l], z] if g == 0 else [z, w1r[base + l]]
                halves.append(jnp.concatenate(blocks, axis=0))
            mats.append(jnp.concatenate(halves, axis=1))
        groups.append(jnp.stack(mats))
    wc = jnp.stack(groups).astype(BF16)
    pe_flat = jnp.broadcast_to(pe.reshape(1, -1), (8, pe.size))
    return wc, pe_flat, w2.astype(BF16)


def kernel(x_prompt, x_sample, cache_nsa_kv, cache_win_kv, state_gla, page_table, ln_in_g, ln_in_b, w_in, cmp_k_pe, cmp_k_w1, cmp_k_w2, cmp_v_pe, cmp_v_w1, cmp_v_w2, gla_w_a2, gla_b_a, gla_norm_g, w_nsa_up, w_gla_up, w_out, ln1_g, ln1_b, mlp_w1, mlp_w2, ln2_g, ln2_b):
    batch, seq, dm = x_prompt.shape
    dec_b = x_sample.shape[0]
    depth = w_in.shape[0]
    n_phys, _, page = cache_nsa_kv.shape[:3]
    past = page_table.shape[1] * page
    wbuf = cache_win_kv.shape[2]
    alpha = (2.0 * depth) ** 0.25
    n_p = batch * seq
    qw = NSA_HEADS * HEAD_DIM

    h_p = x_prompt.reshape(n_p, dm)
    h_s = x_sample.reshape(dec_b, dm)
    ln_in = (ln_in_g[None, :], ln_in_b[None, :])

    cs_p, sn_p = _rope_tables(jnp.arange(seq, dtype=F32))
    cs_s, sn_s = _rope_tables(jnp.full((dec_b,), past, F32))
    ovl = jnp.asarray(_overlap_t())
    wlev = jnp.asarray(_gla_level_matrix(GLA_CHUNK)).astype(BF16)
    col = np.arange(LANES)
    gg = jnp.asarray(((col[:, None] // NSA_HPG == col[None, :] // NSA_HPG)
                      & (col[:, None] < NSA_HEADS) & (col[None, :] < NSA_HEADS)).astype(np.float32)).astype(BF16)
    selr = jnp.asarray(np.stack([col[None, :] == np.arange(NSA_HEADS)[:, None] * 3 + r
                                 for r in range(3)]).astype(np.float32)).astype(BF16)
    cache_t = jnp.transpose(cache_nsa_kv, (0, 1, 3, 4, 5, 2))
    win_t = jnp.transpose(cache_win_kv, (0, 1, 3, 4, 5, 2))
    expand = jnp.asarray((np.arange(past)[None, :] // SEL_LEN == col[:, None]).astype(np.float32)).astype(BF16)
    r_idx = np.arange(page)[None, :]
    m_idx = np.arange(page // 2)[:, None]
    cpp = page // CMP_STRIDE
    perm = jnp.asarray(np.concatenate([r_idx == (m_idx % cpp) * CMP_STRIDE + 2 * (m_idx // cpp) + par
                                       for par in range(2)]).astype(np.float32)).astype(BF16)
    win_buf = jnp.zeros(win_t.shape, F32)
    kvt = depth
    wn = min(WINDOW, seq)

    sizes = (qw, 6 * NSA_GROUPS * HEAD_DIM, 3 * NSA_HEADS, GLA_HEADS * HEAD_DIM, GLA_HEADS * HEAD_DIM,
             GLA_HEADS * LANES, GLA_RANK, GLA_HEADS * LANES, 2 * dm)
    pts = np.concatenate([[0], np.cumsum(sizes)])
    seg = lambda w, i: w[:, pts[i]:pts[i + 1]]

    win_p, gla_p, kv_s, gla_s = [], [], [], []
    for l in range(depth):
        wl = w_in[l]
        misc = jnp.concatenate([seg(wl, 2), seg(wl, 6), jnp.zeros((dm, LANES - 3 * NSA_HEADS - GLA_RANK), F32)], 1)
        w_a = jnp.concatenate([seg(wl, 0), seg(wl, 1), seg(wl, 3), seg(wl, 4), seg(wl, 5), misc], 1).astype(BF16)
        w_b = jnp.concatenate([seg(wl, 7), seg(wl, 8)], axis=1).astype(BF16)
        wa_pad = jnp.zeros((LANES, GLA_HEADS * HEAD_DIM), F32).at[3 * NSA_HEADS:3 * NSA_HEADS + GLA_RANK].set(
            gla_w_a2[l]).astype(BF16)
        ba = gla_b_a[l][None, :]
        w_nu, w_gu, w_o = w_nsa_up[l].astype(BF16), w_gla_up[l].astype(BF16), w_out[l].astype(BF16)
        w1, w2 = mlp_w1[l].astype(BF16), mlp_w2[l].astype(BF16)
        gn, g1, b1, g2, b2 = (gla_norm_g[l][None, :], ln1_g[l][None, :], ln1_b[l][None, :], ln2_g[l][None, :],
                              ln2_b[l][None, :])

        wck, pek, w2k = _compress_weights(cmp_k_pe[l], cmp_k_w1[l], cmp_k_w2[l])
        wcv, pev, w2v = _compress_weights(cmp_v_pe[l], cmp_v_w1[l], cmp_v_w2[l])
        wc = jnp.stack([wck, wcv])
        pe_pair = _pe_bias(jnp.stack([pek, pev]), jnp.stack([cmp_k_w1[l], cmp_v_w1[l]]))
        w2c = jnp.stack([w2k, w2v])

        qp, qr, kv, win, gq, gk, gv, la, ng, kvt, wint = _inproj(h_p, w_a, cs_p, sn_p, wa_pad, ba, (l, kvt),
                                                                 ln_in if l == 0 else None)
        win_p.append(wint[..., seq - wn:])
        kc, vc = _compress_prompt(kv, wc, pe_pair, w2c, batch, seq)
        o_nsa = _nsa_prompt(qp, qr, ng, kc, vc, kv, win, kvt, wint, ovl, batch, seq, l)
        o_gla, st_p = _gla_prompt(gq, gk, gv, la, wlev, batch, seq)
        gla_p.append(st_p)
        h_p = _outproj(h_p, tuple(o_nsa), o_gla, w_b, gn, w_nu, w_gu, w_o, g1, b1, alpha, seq,
                       ln_in if l == 0 else None)
        h_p = _mlp(h_p, w1, w2, g2, b2, alpha)

        qp, qr, kv, win, gq, gk, gv, la, ng = _inproj(h_s, w_a, cs_s, sn_s, wa_pad, ba,
                                                      ln=ln_in if l == 0 else None)
        o8, win_buf = _nsa_decode(page_table, cache_t, l, wc, pe_pair, w2c,
                                  qp.reshape(dec_b, NSA_HEADS, HEAD_DIM), qr.reshape(dec_b, NSA_HEADS, HEAD_DIM),
                                  ng[:, None, :], kv[:, None, :], win[:, None, :], win_t, ovl, gg, selr,
                                  expand, perm, win_buf)
        o_gla, st_s = _gla_decode(gq, gk, la, gv, state_gla, l)
        kv_s.append(kv.reshape(dec_b, 1, 4, NSA_GROUPS, HEAD_DIM))
        gla_s.append(st_s)
        h_s = _outproj(h_s, o8.reshape(dec_b, qw), o_gla, w_b, gn, w_nu, w_gu, w_o, g1, b1, alpha,
                       ln=ln_in if l == 0 else None)
        h_s = _mlp(h_s, w1, w2, g2, b2, alpha)

    return (h_p.reshape(batch, seq, dm), h_s.reshape(dec_b, 1, dm),
            jnp.transpose(kvt, (0, 1, 5, 2, 3, 4)), jnp.transpose(jnp.stack(win_p, axis=1), (0, 1, 5, 2, 3, 4)),
            jnp.stack(gla_p, axis=1), jnp.stack(kv_s, axis=1), jnp.transpose(win_buf, (0, 1, 5, 2, 3, 4)),
            jnp.stack(gla_s, axis=1))
```
